```python
import jax, jax.numpy as jnp
from jax import lax
import numpy as np

D_MODEL = 2048
BATCH = 1
SEQ = 16384
DEPTH = 1

BLOCK = 128
ROPE_THETA = 10000.0
LN_EPS = 1e-5
A_HEADS = 16
A_KV_HEADS = 2
A_GROUP = A_HEADS // A_KV_HEADS
A_HEAD_DIM = 64
A_WINDOW = 128
A_WIDTH = A_HEADS * A_HEAD_DIM
A_KV_WIDTH = A_KV_HEADS * A_HEAD_DIM
B_PATTERNS = ((128, 1), (512, 4), (2048, 16))
B_N_GROUPS = len(B_PATTERNS)
B_HEADS_PER_GROUP = 4
B_HEAD_DIM = 128
B_GROUP_WIDTH = B_HEADS_PER_GROUP * B_HEAD_DIM
B_QKV_WIDTH = B_N_GROUPS * B_GROUP_WIDTH
B_OUT_WIDTH = B_GROUP_WIDTH
N_BRANCHES = 2
IN_SIZES = (A_WIDTH, A_KV_WIDTH, A_KV_WIDTH, A_WIDTH,
            B_QKV_WIDTH, B_QKV_WIDTH, B_QKV_WIDTH, B_OUT_WIDTH,
            D_MODEL, D_MODEL)
D_IN = sum(IN_SIZES)
IN_OFFSETS = tuple(int(o) for o in np.cumsum(IN_SIZES)[:-1])
DN_ALPHA = float((2 * DEPTH) ** 0.25)
DN_BETA = float((8 * DEPTH) ** -0.25)

kernel_name = "hybrid_swa_sink_dilated_gated_deepnorm"


def apply_rope(t, positions):
    half = t.shape[-1] // 2
    inv_freq = ROPE_THETA ** (-jnp.arange(half, dtype=jnp.float32) / half)
    ang = positions.astype(jnp.float32)[..., None] * inv_freq
    cos = jnp.cos(ang)[:, :, None, :]
    sin = jnp.sin(ang)[:, :, None, :]
    t1 = t[..., :half].astype(jnp.float32)
    t2 = t[..., half:].astype(jnp.float32)
    out = jnp.concatenate([t1 * cos - t2 * sin, t2 * cos + t1 * sin], axis=-1)
    return out.astype(t.dtype)


def pad_seq(t):
    pad = (-t.shape[1]) % BLOCK
    return jnp.pad(t, [(0, 0), (0, pad)] + [(0, 0)] * (t.ndim - 2))


def banded_attention(q, k, v, max_dist, sink=None):
    n, L, kvh, g, dh = q.shape
    nb = L // BLOCK
    qb = q.reshape(n, nb, BLOCK, kvh, g, dh)
    kb = k.reshape(n, nb, BLOCK, kvh, dh)
    vb = v.reshape(n, nb, BLOCK, kvh, dh)
    zero_blk = ((0, 0), (1, 0), (0, 0), (0, 0), (0, 0))
    kk = jnp.concatenate([jnp.pad(kb, zero_blk)[:, :-1], kb], axis=2)
    vv = jnp.concatenate([jnp.pad(vb, zero_blk)[:, :-1], vb], axis=2)
    s = jnp.einsum('ncqhgd,nckhd->nchgqk', qb, kk,
                   preferred_element_type=jnp.float32) * (dh ** -0.5)
    q_idx = jnp.arange(BLOCK)[:, None] + BLOCK
    k_idx = jnp.arange(2 * BLOCK)[None, :]
    dist = q_idx - k_idx
    band = (dist >= 0) & (dist <= max_dist)
    key_pos = jnp.arange(nb)[:, None] * BLOCK - BLOCK + jnp.arange(2 * BLOCK)[None, :]
    mask = band[None] & (key_pos >= 0)[:, None, :]
    s = jnp.where(mask[None, :, None, None], s, -jnp.inf)
    m = jnp.max(s, axis=-1)
    if sink is not None:
        sk = sink.astype(jnp.float32)[None, None, :, :, None]
        m = jnp.maximum(m, sk)
    p = jnp.exp(s - m[..., None])
    denom = jnp.sum(p, axis=-1)
    if sink is not None:
        denom = denom + jnp.exp(sk - m)
    o = jnp.einsum('nchgqk,nckhd->ncqhgd', p, vv.astype(jnp.float32))
    denom_t = denom.transpose(0, 1, 4, 2, 3)
    o = o / denom_t[..., None]
    lse = (m.transpose(0, 1, 4, 2, 3) + jnp.log(denom_t)).reshape(n, L, kvh, g)
    return o.reshape(n, L, kvh, g, dh), lse


def dilated_group(q, k, v, window, dilation):
    bn, s_len, h, dh = q.shape
    L = s_len // dilation

    def to_sub(t):
        return t.reshape(bn, L, dilation, h, dh).transpose(0, 2, 1, 3, 4).reshape(bn * dilation, L, h, dh)

    qs, ks, vs = pad_seq(to_sub(q)), pad_seq(to_sub(k)), pad_seq(to_sub(v))
    o, lse = banded_attention(qs[:, :, :, None, :], ks, vs, window // dilation)
    o = o[:, :L, :, 0].reshape(bn, dilation, L, h, dh).transpose(0, 2, 1, 3, 4).reshape(bn, s_len, h, dh)
    lse = lse[:, :L, :, 0].reshape(bn, dilation, L, h).transpose(0, 2, 1, 3).reshape(bn, s_len, h)
    return o, lse


def layer_norm(z, g, b):
    z = z.astype(jnp.float32)
    mu = jnp.mean(z, axis=-1, keepdims=True)
    var = jnp.mean(jnp.square(z - mu), axis=-1, keepdims=True)
    return (z - mu) * lax.rsqrt(var + LN_EPS) * g.astype(jnp.float32) + b.astype(jnp.float32)


def hybrid_layer(x, positions, w_in, b_gate, sinks, w_pa, w_pb, w_out, ln_g, ln_b):
    bn, s_len, _ = x.shape
    h = jnp.einsum('bsd,de->bse', x, w_in)
    (qa, ka, va, gate_a, qb, kb, vb, gate_b, mg_a, mg_b) = jnp.split(h, IN_OFFSETS, axis=-1)

    qa = apply_rope(qa.reshape(bn, s_len, A_HEADS, A_HEAD_DIM), positions)
    ka = apply_rope(ka.reshape(bn, s_len, A_KV_HEADS, A_HEAD_DIM), positions)
    va = va.reshape(bn, s_len, A_KV_HEADS, A_HEAD_DIM)
    qa = qa.reshape(bn, s_len, A_KV_HEADS, A_GROUP, A_HEAD_DIM)
    o_a, _ = banded_attention(pad_seq(qa), pad_seq(ka), pad_seq(va), A_WINDOW - 1,
                              sink=sinks.reshape(A_KV_HEADS, A_GROUP))
    o_a = o_a[:, :s_len].reshape(bn, s_len, A_WIDTH)

    n_b = B_N_GROUPS * B_HEADS_PER_GROUP
    qb = apply_rope(qb.reshape(bn, s_len, n_b, B_HEAD_DIM), positions)
    kb = apply_rope(kb.reshape(bn, s_len, n_b, B_HEAD_DIM), positions)
    vb = vb.reshape(bn, s_len, n_b, B_HEAD_DIM)
    outs, lses = [], []
    for gi, (window, dilation) in enumerate(B_PATTERNS):
        sl = slice(gi * B_HEADS_PER_GROUP, (gi + 1) * B_HEADS_PER_GROUP)
        o_g, lse_g = dilated_group(qb[:, :, sl], kb[:, :, sl], vb[:, :, sl], window, dilation)
        outs.append(o_g)
        lses.append(lse_g)
    wts = jax.nn.softmax(jnp.stack(lses, axis=0), axis=0)
    o_b = jnp.sum(wts[..., None] * jnp.stack(outs, axis=0), axis=0).reshape(bn, s_len, B_OUT_WIDTH)

    y_a = jnp.einsum('bse,ed->bsd', (o_a * jax.nn.silu(gate_a.astype(jnp.float32))).astype(x.dtype), w_pa)
    y_b = jnp.einsum('bse,ed->bsd', (o_b * jax.nn.silu(gate_b.astype(jnp.float32))).astype(x.dtype), w_pb)
    merged = jax.nn.sigmoid(mg_a + b_gate[0]) * y_a + jax.nn.sigmoid(mg_b + b_gate[1]) * y_b
    sub = jnp.einsum('bsd,de->bse', merged, w_out)

    out = layer_norm(DN_ALPHA * x.astype(jnp.float32) + sub.astype(jnp.float32), ln_g, ln_b)
    return out.astype(x.dtype)


def setup_inputs(seed: int = 0) -> dict:
    key = jax.random.key(seed)
    ks = jax.random.split(key, 10)
    x = jax.random.normal(ks[0], (BATCH, SEQ, D_MODEL), jnp.float32)
    positions = jnp.broadcast_to(jnp.arange(SEQ, dtype=jnp.int32)[None, :], (BATCH, SEQ))
    col_scale = np.concatenate([
        np.full((n,), DN_BETA if i in (2, 6) else 1.0, np.float32) for i, n in enumerate(IN_SIZES)])
    w_in = (jax.random.normal(ks[1], (DEPTH, D_MODEL, D_IN), jnp.float32)
            * (D_MODEL ** -0.5) * jnp.asarray(col_scale))
    b_gate = 0.01 * jax.random.normal(ks[2], (DEPTH, N_BRANCHES, D_MODEL), jnp.float32)
    sinks = 0.5 * jax.random.normal(ks[3], (DEPTH, A_HEADS), jnp.float32)
    w_pa = jax.random.normal(ks[4], (DEPTH, A_WIDTH, D_MODEL), jnp.float32) * (A_WIDTH ** -0.5) * DN_BETA
    w_pb = jax.random.normal(ks[5], (DEPTH, B_OUT_WIDTH, D_MODEL), jnp.float32) * (B_OUT_WIDTH ** -0.5) * DN_BETA
    w_out = jax.random.normal(ks[6], (DEPTH, D_MODEL, D_MODEL), jnp.float32) * (D_MODEL ** -0.5) * DN_BETA
    ln_g = 1.0 + 0.02 * jax.random.normal(ks[7], (DEPTH, D_MODEL), jnp.float32)
    ln_b = 0.02 * jax.random.normal(ks[8], (DEPTH, D_MODEL), jnp.float32)
    return {"x": x, "positions": positions, "w_in": w_in, "b_gate": b_gate, "sinks": sinks,
            "w_pa": w_pa, "w_pb": w_pb, "w_out": w_out, "ln_g": ln_g, "ln_b": ln_b}


def reference(x, positions, w_in, b_gate, sinks, w_pa, w_pb, w_out, ln_g, ln_b):
    for layer in range(DEPTH):
        x = hybrid_layer(x, positions, w_in[layer], b_gate[layer], sinks[layer], w_pa[layer],
                         w_pb[layer], w_out[layer], ln_g[layer], ln_b[layer])
    return x
```

```python
import functools

import jax
import jax.numpy as jnp
import numpy as np
from jax import lax
from jax.experimental import pallas as pl
from jax.experimental.pallas import tpu as pltpu

F32 = jnp.float32
BF16 = jnp.bfloat16

D_MODEL = 2048
ROPE_THETA = 10000.0
LN_EPS = 1e-5
BLOCK = 128
LANES = 128
A_HEADS = 16
A_KV_HEADS = 2
A_HEAD_DIM = 64
A_WINDOW = 128
A_WIDTH = A_HEADS * A_HEAD_DIM
A_KV_WIDTH = A_KV_HEADS * A_HEAD_DIM
B_PATTERNS = ((128, 1), (512, 4), (2048, 16))
B_HEADS_PER_GROUP = 4
B_HEAD_DIM = 128
B_GROUP_WIDTH = B_HEADS_PER_GROUP * B_HEAD_DIM
B_N_GROUPS = len(B_PATTERNS)
B_QKV_WIDTH = B_N_GROUPS * B_GROUP_WIDTH
DEPTH = 1
DN_ALPHA = float((2 * DEPTH) ** 0.25)

IN_SIZES = (A_WIDTH, A_KV_WIDTH, A_KV_WIDTH, A_WIDTH, B_QKV_WIDTH, B_QKV_WIDTH, B_QKV_WIDTH,
            B_GROUP_WIDTH, D_MODEL, D_MODEL)
IN_OFFSETS = tuple(int(o) for o in np.cumsum((0,) + IN_SIZES[:-1]))
(OFF_QA, OFF_KA, OFF_VA, OFF_GATE_A, OFF_QB, OFF_KB, OFF_VB, OFF_GATE_B, OFF_MG_A, OFF_MG_B) = IN_OFFSETS

VMEM_LIMIT_BYTES = 56 * 1024 * 1024

EP_NONE, EP_ROPE64, EP_ROPE128, EP_SILU, EP_SIGMOID = range(5)

PROJ_TM = 1024
PROJ_TN = 768


def _compiler_params(semantics):
    return pltpu.CompilerParams(dimension_semantics=semantics, vmem_limit_bytes=VMEM_LIMIT_BYTES)


def _rope_tables_kernel(pos_ref, freq_ref, cos_a_ref, sin_a1_ref, sin_a2_ref, cos_b_ref, sin_b_ref):
    ang = pos_ref[...] * freq_ref[...]
    c = jnp.cos(ang)
    s = jnp.sin(ang)
    lane = lax.broadcasted_iota(jnp.int32, c.shape, 1)

    def expand(t):
        r32 = pltpu.roll(t, 32, axis=1)
        r64 = pltpu.roll(t, 64, axis=1)
        r96 = pltpu.roll(t, 96, axis=1)
        t_a = jnp.where(lane < 32, t, jnp.where(lane < 64, r32, jnp.where(lane < 96, r64, r96)))
        t_b = jnp.where(lane < 64, r96, r32)
        return t_a, t_b

    c_a, c_b = expand(c)
    s_a, s_b = expand(s)
    first_half_a = jnp.bitwise_and(lane, A_HEAD_DIM - 1) < A_HEAD_DIM // 2
    cos_a_ref[...] = c_a
    sin_a1_ref[...] = jnp.where(first_half_a, -s_a, 0.0)
    sin_a2_ref[...] = jnp.where(first_half_a, 0.0, s_a)
    cos_b_ref[...] = c_b
    sin_b_ref[...] = jnp.where(lane < 64, -s_b, s_b)


def _rope_tables(positions):
    s_len = positions.shape[1]
    half_a, half_b = A_HEAD_DIM // 2, B_HEAD_DIM // 2
    inv_a = ROPE_THETA ** (-jnp.arange(half_a, dtype=F32) / half_a)
    inv_b = ROPE_THETA ** (-jnp.arange(half_b, dtype=F32) / half_b)
    freq = jnp.concatenate([inv_a, inv_b, jnp.zeros((LANES - half_a - half_b,), F32)])[None, :]
    pos = jnp.broadcast_to(positions.reshape(s_len, 1).astype(F32), (s_len, LANES))
    rows = 1024
    tab = jax.ShapeDtypeStruct((s_len, LANES), F32)
    spec = pl.BlockSpec((rows, LANES), lambda i: (i, 0))
    return pl.pallas_call(
        _rope_tables_kernel,
        grid=(s_len // rows,),
        in_specs=[spec, pl.BlockSpec((1, LANES), lambda i: (0, 0))],
        out_specs=[spec] * 5,
        out_shape=[tab] * 5,
        compiler_params=_compiler_params(("arbitrary",)),
        name="rope_tables",
    )(pos, freq)


def _proj_kernel(tile_kinds, n_axis, x_ref, w_ref, bias_ref, cos_a_ref, sin_a1_ref, sin_a2_ref,
                 cos_b_ref, sin_b_ref, out_ref, xb_ref):
    n = pl.program_id(n_axis)

    @pl.when(n == 0)
    def _():
        xb_ref[...] = x_ref[...].astype(BF16)

    acc = jnp.dot(xb_ref[...], w_ref[...], preferred_element_type=F32)

    def epilogue(kinds):
        for ci, kind in enumerate(kinds):
            cols = slice(ci * LANES, (ci + 1) * LANES)
            t = acc[:, cols]
            if kind == EP_ROPE64:
                t = (t * cos_a_ref[...] + pltpu.roll(t, 96, axis=1) * sin_a1_ref[...]
                     + pltpu.roll(t, 32, axis=1) * sin_a2_ref[...])
            elif kind == EP_ROPE128:
                t = t * cos_b_ref[...] + pltpu.roll(t, 64, axis=1) * sin_b_ref[...]
            elif kind == EP_SILU:
                t = t * jax.nn.sigmoid(t)
            elif kind == EP_SIGMOID:
                t = jax.nn.sigmoid(t + bias_ref[:, cols])
            out_ref[:, cols] = t.astype(out_ref.dtype)

    distinct = []
    for kinds in tile_kinds:
        if kinds not in distinct:
            distinct.append(kinds)
    for kinds in distinct:
        tiles = [i for i, k in enumerate(tile_kinds) if k == kinds]
        cond = n == tiles[0]
        for i in tiles[1:]:
            cond = cond | (n == i)
        pl.when(cond)(functools.partial(epilogue, kinds))


def _in_projection(x_view, w, bias, tables, chunk_kinds, dilation):
    d = dilation
    seq_l = x_view.shape[0]
    k_dim = w.shape[0]
    n_cols = w.shape[1]
    n_tiles = n_cols // PROJ_TN
    chunks_per_tile = PROJ_TN // LANES
    tile_kinds = tuple(tuple(chunk_kinds[t * chunks_per_tile:(t + 1) * chunks_per_tile]) for t in range(n_tiles))
    tm = min(PROJ_TM, seq_l)
    grid = (d, seq_l // tm, n_tiles)
    row_spec = lambda width: pl.BlockSpec((tm, width), lambda r, l, n: (l, r))
    return pl.pallas_call(
        functools.partial(_proj_kernel, tile_kinds, 2),
        grid=grid,
        in_specs=[row_spec(k_dim),
                  pl.BlockSpec((k_dim, PROJ_TN), lambda r, l, n: (0, n)),
                  pl.BlockSpec((1, PROJ_TN), lambda r, l, n: (0, n))] + [row_spec(LANES)] * 5,
        out_specs=pl.BlockSpec((None, tm, PROJ_TN), lambda r, l, n: (r, l, n)),
        out_shape=jax.ShapeDtypeStruct((d, seq_l, n_cols), BF16),
        scratch_shapes=[pltpu.VMEM((tm, k_dim), BF16)],
        compiler_params=_compiler_params(("arbitrary", "arbitrary", "arbitrary")),
        name=f"in_projection_d{dilation}",
    )(x_view, w, bias, *tables)


NAT_QA = 0
NAT_GATE_A = NAT_QA + A_WIDTH
NAT_MG_A = NAT_GATE_A + A_WIDTH
NAT_MG_B = NAT_MG_A + D_MODEL
NAT_QB0 = NAT_MG_B + D_MODEL
NAT_KB0 = NAT_QB0 + B_GROUP_WIDTH
NAT_VB0 = NAT_KB0 + B_GROUP_WIDTH
NAT_GATE_B = NAT_VB0 + B_GROUP_WIDTH
NAT_KA = NAT_GATE_B + B_GROUP_WIDTH
NAT_VA = NAT_KA + A_KV_WIDTH
NAT_WIDTH = NAT_VA + A_KV_WIDTH


def _pack_weights(w_in, b_gate):
    def cols(off, width):
        return w_in[:, off:off + width]

    g = B_GROUP_WIDTH
    nat = jnp.concatenate([
        cols(OFF_QA, A_WIDTH), cols(OFF_GATE_A, A_WIDTH), cols(OFF_MG_A, D_MODEL), cols(OFF_MG_B, D_MODEL),
        cols(OFF_QB, g), cols(OFF_KB, g), cols(OFF_VB, g), cols(OFF_GATE_B, g),
        cols(OFF_KA, A_KV_WIDTH), cols(OFF_VA, A_KV_WIDTH)], axis=1).astype(BF16)
    nat_kinds = ([EP_ROPE64] * (A_WIDTH // LANES) + [EP_SILU] * (A_WIDTH // LANES)
                 + [EP_SIGMOID] * (2 * D_MODEL // LANES)
                 + [EP_ROPE128] * (2 * g // LANES) + [EP_NONE] * (g // LANES) + [EP_SILU] * (g // LANES)
                 + [EP_ROPE64] * (A_KV_WIDTH // LANES) + [EP_NONE] * (A_KV_WIDTH // LANES))
    nat_bias = jnp.zeros((1, NAT_WIDTH), F32)
    nat_bias = nat_bias.at[0, NAT_MG_A:NAT_MG_A + D_MODEL].set(b_gate[0])
    nat_bias = nat_bias.at[0, NAT_MG_B:NAT_MG_B + D_MODEL].set(b_gate[1])
    dil = []
    for gi in range(1, B_N_GROUPS):
        dil.append(jnp.concatenate([cols(OFF_QB + gi * g, g), cols(OFF_KB + gi * g, g), cols(OFF_VB + gi * g, g)],
                                   axis=1).astype(BF16))
    dil_kinds = [EP_ROPE128] * (2 * g // LANES) + [EP_NONE] * (g // LANES)
    dil_bias = jnp.zeros((1, 3 * g), F32)
    return nat, nat_kinds, nat_bias, dil, dil_kinds, dil_bias


ATT_TQ = 512


def _band_bias(max_dist, reps):
    q_idx = np.arange(BLOCK)[:, None] + BLOCK
    k_idx = np.arange(2 * BLOCK)[None, :]
    dist = q_idx - k_idx
    band = (dist >= 0) & (dist <= max_dist)
    first = band & (k_idx >= BLOCK)
    both = np.stack([first, band]).astype(bool)
    bias = np.where(both, 0.0, -np.inf).astype(np.float32)
    return jnp.asarray(np.tile(bias, (1, 1, reps)))


def _swap_lane_halves(t):
    return pltpu.roll(t, LANES // 2, axis=1)


def _mixer_a_kernel(q_ref, gate_ref, k_ref, v_ref, kprev_ref, vprev_ref, sink_ref, bias_ref, out_ref,
                    kbuf_ref, vbuf_ref):
    step = pl.program_id(0)
    tq = q_ref.shape[0]
    n_blocks = tq // BLOCK
    pairs_per_group = A_HEADS // A_KV_HEADS // 2
    kbuf_ref[0:BLOCK, :] = kprev_ref[...]
    kbuf_ref[BLOCK:, :] = k_ref[...]
    vbuf_ref[0:BLOCK, :] = vprev_ref[...]
    vbuf_ref[BLOCK:, :] = v_ref[...]
    lane2 = lax.broadcasted_iota(jnp.int32, (2 * BLOCK, LANES), 1)
    low2 = lane2 < LANES // 2
    ones_low = jnp.where(low2, 1.0, 0.0).astype(BF16)
    ones_high = jnp.where(low2, 0.0, 1.0).astype(BF16)
    denom_cols = jnp.concatenate([ones_low, ones_high], axis=0)
    lane_o = lax.broadcasted_iota(jnp.int32, (pairs_per_group * BLOCK, LANES), 1)
    low_o = lane_o < LANES // 2

    for b in range(n_blocks):
        first = jnp.logical_and(step == 0, b == 0) if b == 0 else False
        bias = bias_ref[jnp.where(first, 0, 1)] if b == 0 else bias_ref[1]
        bias = jnp.concatenate([bias] * pairs_per_group, axis=0)
        rows = slice(b * BLOCK, (b + 1) * BLOCK)
        kk = kbuf_ref[b * BLOCK:(b + 2) * BLOCK, :].astype(F32)
        vv = vbuf_ref[b * BLOCK:(b + 2) * BLOCK, :].astype(F32)
        kk_sw = _swap_lane_halves(kk)
        vv_sw = _swap_lane_halves(vv)
        for g in range(A_KV_HEADS):
            if g == 0:
                k_top, k_bot = jnp.where(low2, kk, 0.0), jnp.where(low2, 0.0, kk_sw)
                v_top, v_bot = jnp.where(low2, vv, 0.0), jnp.where(low2, 0.0, vv_sw)
            else:
                k_top, k_bot = jnp.where(low2, kk_sw, 0.0), jnp.where(low2, 0.0, kk)
                v_top, v_bot = jnp.where(low2, vv_sw, 0.0), jnp.where(low2, 0.0, vv)
            k2 = jnp.concatenate([k_top, k_bot], axis=0).astype(BF16)
            v2 = jnp.concatenate([jnp.concatenate([v_top, v_bot], axis=0).astype(BF16), denom_cols], axis=1)
            pair_cols = [slice((g * pairs_per_group + p) * LANES, (g * pairs_per_group + p + 1) * LANES)
                         for p in range(pairs_per_group)]
            qs = jnp.concatenate([q_ref[rows, c] for c in pair_cols], axis=0)
            qs = qs * jnp.asarray(A_HEAD_DIM ** -0.5, BF16)
            s = lax.dot_general(qs, k2, (((1,), (1,)), ((), ())), preferred_element_type=F32) + bias
            s0, s1 = s[:, :2 * BLOCK], s[:, 2 * BLOCK:]
            sink0, sink1 = sink_ref[g, 0], sink_ref[g, 1]
            m0 = jnp.maximum(jnp.max(s0, axis=1, keepdims=True), sink0[:, :1])
            m1 = jnp.maximum(jnp.max(s1, axis=1, keepdims=True), sink1[:, :1])
            p = jnp.concatenate([jnp.exp(s0 - m0), jnp.exp(s1 - m1)], axis=1).astype(BF16)
            o2 = jnp.dot(p, v2, preferred_element_type=F32)
            sink_term = jnp.where(low_o, jnp.exp(sink0 - m0), jnp.exp(sink1 - m1))
            o = o2[:, :LANES] / (o2[:, LANES:] + sink_term)
            gate = jnp.concatenate([gate_ref[rows, c] for c in pair_cols], axis=0).astype(F32)
            y = (o * gate).astype(out_ref.dtype)
            for p_i, c in enumerate(pair_cols):
                out_ref[rows, c] = y[p_i * BLOCK:(p_i + 1) * BLOCK, :]


def _mixer_a(h_nat, sinks):
    s_len = h_nat.shape[0]
    tq = ATT_TQ
    pairs_per_group = A_HEADS // A_KV_HEADS // 2
    sk = sinks.astype(F32).reshape(A_KV_HEADS, pairs_per_group, 2)
    sk = jnp.transpose(sk, (0, 2, 1))
    sk = jnp.broadcast_to(sk[:, :, :, None, None], (A_KV_HEADS, 2, pairs_per_group, BLOCK, LANES))
    sk = sk.reshape(A_KV_HEADS, 2, pairs_per_group * BLOCK, LANES)
    bias = _band_bias(A_WINDOW - 1, 2)
    blocks_per_step = tq // BLOCK
    prev = lambda i: (jnp.maximum(i * blocks_per_step - 1, 0))
    return pl.pallas_call(
        _mixer_a_kernel,
        grid=(s_len // tq,),
        in_specs=[
            pl.BlockSpec((tq, A_WIDTH), lambda i: (i, NAT_QA // A_WIDTH)),
            pl.BlockSpec((tq, A_WIDTH), lambda i: (i, NAT_GATE_A // A_WIDTH)),
            pl.BlockSpec((tq, A_KV_WIDTH), lambda i: (i, NAT_KA // A_KV_WIDTH)),
            pl.BlockSpec((tq, A_KV_WIDTH), lambda i: (i, NAT_VA // A_KV_WIDTH)),
            pl.BlockSpec((BLOCK, A_KV_WIDTH), lambda i: (prev(i), NAT_KA // A_KV_WIDTH)),
            pl.BlockSpec((BLOCK, A_KV_WIDTH), lambda i: (prev(i), NAT_VA // A_KV_WIDTH)),
            pl.BlockSpec(sk.shape, lambda i: (0, 0, 0, 0)),
            pl.BlockSpec(bias.shape, lambda i: (0, 0, 0)),
        ],
        out_specs=pl.BlockSpec((tq, A_WIDTH), lambda i: (i, 0)),
        out_shape=jax.ShapeDtypeStruct((s_len, A_WIDTH), BF16),
        scratch_shapes=[pltpu.VMEM((tq + BLOCK, A_KV_WIDTH), BF16), pltpu.VMEM((tq + BLOCK, A_KV_WIDTH), BF16)],
        compiler_params=_compiler_params(("arbitrary",)),
        name="mixer_a",
    )(h_nat, h_nat, h_nat, h_nat, h_nat, h_nat, sk, bias)


def _mixer_b_kernel(q_ref, k_ref, v_ref, kprev_ref, vprev_ref, bias_ref, o_ref, lse_ref, kbuf_ref, vbuf_ref):
    step = pl.program_id(1)
    tq = q_ref.shape[0]
    n_blocks = tq // BLOCK
    kbuf_ref[0:BLOCK, :] = kprev_ref[...]
    kbuf_ref[BLOCK:, :] = k_ref[...]
    vbuf_ref[0:BLOCK, :] = vprev_ref[...]
    vbuf_ref[BLOCK:, :] = v_ref[...]
    ones = jnp.ones((2 * BLOCK, LANES), BF16)
    lane = lax.broadcasted_iota(jnp.int32, (BLOCK, LANES), 1)
    scale = B_HEAD_DIM ** -0.5
    for b in range(n_blocks):
        first = jnp.logical_and(step == 0, b == 0) if b == 0 else False
        bias = bias_ref[jnp.where(first, 0, 1)] if b == 0 else bias_ref[1]
        rows = slice(b * BLOCK, (b + 1) * BLOCK)
        lse_tile = jnp.zeros((BLOCK, LANES), F32)
        for h in range(B_HEADS_PER_GROUP):
            cols = slice(h * B_HEAD_DIM, (h + 1) * B_HEAD_DIM)
            kk = kbuf_ref[b * BLOCK:(b + 2) * BLOCK, cols]
            v2 = jnp.concatenate([vbuf_ref[b * BLOCK:(b + 2) * BLOCK, cols], ones], axis=1)
            s = lax.dot_general(q_ref[rows, cols], kk, (((1,), (1,)), ((), ())),
                                preferred_element_type=F32) * scale + bias
            m = jnp.max(s, axis=1, keepdims=True)
            p = jnp.exp(s - m).astype(BF16)
            o2 = jnp.dot(p, v2, preferred_element_type=F32)
            denom = o2[:, LANES:]
            o_ref[rows, cols] = (o2[:, :LANES] / denom).astype(o_ref.dtype)
            lse_h = m + jnp.log(denom)
            in_head = jnp.logical_and(lane >= h * 32, lane < (h + 1) * 32)
            lse_tile = jnp.where(in_head, lse_h, lse_tile)
        lse_ref[rows, :] = lse_tile


def _mixer_b_group(h_sub, col_blocks, dilation):
    d, seq_l, _ = h_sub.shape
    tq = min(ATT_TQ, seq_l)
    cq, ck, cv = col_blocks
    gw = B_GROUP_WIDTH
    blocks_per_step = tq // BLOCK
    prev = lambda i: jnp.maximum(i * blocks_per_step - 1, 0)
    bias = _band_bias(BLOCK, 1)
    cur = lambda c: pl.BlockSpec((None, tq, gw), lambda r, i: (r, i, c))
    prv = lambda c: pl.BlockSpec((None, BLOCK, gw), lambda r, i: (r, prev(i), c))
    o, lse = pl.pallas_call(
        _mixer_b_kernel,
        grid=(d, seq_l // tq),
        in_specs=[cur(cq), cur(ck), cur(cv), prv(ck), prv(cv), pl.BlockSpec(bias.shape, lambda r, i: (0, 0, 0))],
        out_specs=[pl.BlockSpec((tq, gw), lambda r, i: (i, r)),
                   pl.BlockSpec((tq, LANES), lambda r, i: (i, r))],
        out_shape=[jax.ShapeDtypeStruct((seq_l, d * gw), BF16), jax.ShapeDtypeStruct((seq_l, d * LANES), F32)],
        scratch_shapes=[pltpu.VMEM((tq + BLOCK, gw), BF16), pltpu.VMEM((tq + BLOCK, gw), BF16)],
        compiler_params=_compiler_params(("arbitrary", "arbitrary")),
        name=f"mixer_b_d{dilation}",
    )(h_sub, h_sub, h_sub, h_sub, h_sub, bias)
    return o.reshape(seq_l * d, gw), lse.reshape(seq_l * d, LANES)


TAIL_TM = 256


def _tail_kernel(x_ref, ya_ref, o0_ref, o1_ref, o2_ref, l0_ref, l1_ref, l2_ref, gate_b_ref, sig_a_ref, sig_b_ref,
                 w_pa_ref, w_pb_ref, w_out_ref, ln_g_ref, ln_b_ref, out_ref):
    lses = [l0_ref[...], l1_ref[...], l2_ref[...]]
    m = jnp.maximum(jnp.maximum(lses[0], lses[1]), lses[2])
    es = [jnp.exp(l - m) for l in lses]
    inv = 1.0 / (es[0] + es[1] + es[2])
    wts = [e * inv for e in es]
    outs = [o0_ref, o1_ref, o2_ref]
    yb_cols = []
    for h in range(B_HEADS_PER_GROUP):
        cols = slice(h * B_HEAD_DIM, (h + 1) * B_HEAD_DIM)
        acc = None
        for gi in range(B_N_GROUPS):
            w_h = jnp.broadcast_to(wts[gi][:, h * 32:h * 32 + 1], (wts[gi].shape[0], B_HEAD_DIM))
            term = w_h * outs[gi][:, cols].astype(F32)
            acc = term if acc is None else acc + term
        yb_cols.append((acc * gate_b_ref[:, cols].astype(F32)).astype(BF16))
    yb = jnp.concatenate(yb_cols, axis=1)
    y_a = jnp.dot(ya_ref[...], w_pa_ref[...], preferred_element_type=F32)
    y_b = jnp.dot(yb, w_pb_ref[...], preferred_element_type=F32)
    merged = sig_a_ref[...].astype(F32) * y_a + sig_b_ref[...].astype(F32) * y_b
    sub = jnp.dot(merged.astype(BF16), w_out_ref[...], preferred_element_type=F32)
    z = DN_ALPHA * x_ref[...] + sub
    mu = jnp.mean(z, axis=-1, keepdims=True)
    zc = z - mu
    var = jnp.mean(zc * zc, axis=-1, keepdims=True)
    out_ref[...] = (zc * lax.rsqrt(var + LN_EPS) * ln_g_ref[...] + ln_b_ref[...]).astype(out_ref.dtype)


def _tail(x2d, ya, o_groups, lse_groups, h_nat, w_pa, w_pb, w_out, ln_g, ln_b):
    s_len = x2d.shape[0]
    tm = TAIL_TM
    row = lambda width, cblk=0: pl.BlockSpec((tm, width), lambda i: (i, cblk))
    full = lambda a: pl.BlockSpec(a.shape, lambda i: (0,) * a.ndim)
    gw = B_GROUP_WIDTH
    return pl.pallas_call(
        _tail_kernel,
        grid=(s_len // tm,),
        in_specs=[row(D_MODEL), row(A_WIDTH), row(gw), row(gw), row(gw), row(LANES), row(LANES), row(LANES),
                  row(gw, NAT_GATE_B // gw), row(D_MODEL, NAT_MG_A // D_MODEL), row(D_MODEL, NAT_MG_B // D_MODEL),
                  full(w_pa), full(w_pb), full(w_out), full(ln_g), full(ln_b)],
        out_specs=row(D_MODEL),
        out_shape=jax.ShapeDtypeStruct((s_len, D_MODEL), x2d.dtype),
        compiler_params=_compiler_params(("arbitrary",)),
        name="tail",
    )(x2d, ya, *o_groups, *lse_groups, h_nat, h_nat, h_nat, w_pa, w_pb, w_out, ln_g, ln_b)


def _hybrid_layer(x, positions, w_in, b_gate, sinks, w_pa, w_pb, w_out, ln_g, ln_b):
    bn, s_len, d_model = x.shape
    assert bn == 1 and d_model == D_MODEL
    assert s_len % (BLOCK * B_PATTERNS[-1][1]) == 0
    x2d = x.reshape(s_len, d_model)
    tables = _rope_tables(positions)
    nat_w, nat_kinds, nat_bias, dil_w, dil_kinds, dil_bias = _pack_weights(w_in, b_gate)

    h_nat = _in_projection(x2d, nat_w, nat_bias, tables, nat_kinds, 1)
    gw = B_GROUP_WIDTH
    o_groups, lse_groups = [], []
    o, lse = _mixer_b_group(h_nat, (NAT_QB0 // gw, NAT_KB0 // gw, NAT_VB0 // gw), 1)
    o_groups.append(o)
    lse_groups.append(lse)
    for gi in range(1, B_N_GROUPS):
        window, dilation = B_PATTERNS[gi]
        assert window // dilation == BLOCK
        seq_l = s_len // dilation
        h_sub = _in_projection(x2d.reshape(seq_l, dilation * d_model), dil_w[gi - 1], dil_bias,
                               [t.reshape(seq_l, dilation * LANES) for t in tables], dil_kinds, dilation)
        o, lse = _mixer_b_group(h_sub, (0, 1, 2), dilation)
        o_groups.append(o)
        lse_groups.append(lse)

    h_nat2d = h_nat.reshape(s_len, NAT_WIDTH)
    ya = _mixer_a(h_nat2d, sinks)
    out = _tail(x2d, ya, o_groups, lse_groups, h_nat2d, w_pa.astype(BF16), w_pb.astype(BF16), w_out.astype(BF16),
                ln_g.reshape(1, d_model).astype(F32), ln_b.reshape(1, d_model).astype(F32))
    return out.reshape(bn, s_len, d_model)


def kernel(x, positions, w_in, b_gate, sinks, w_pa, w_pb, w_out, ln_g, ln_b):
    for layer in range(w_in.shape[0]):
        x = _hybrid_layer(x, positions, w_in[layer], b_gate[layer], sinks[layer], w_pa[layer], w_pb[layer],
                          w_out[layer], ln_g[layer], ln_b[layer])
    return x
```

```python
import functools

import jax
import jax.numpy as jnp
import numpy as np
from jax import lax
from jax.experimental import pallas as pl
from jax.experimental.pallas import tpu as pltpu

F32 = jnp.float32
BF16 = jnp.bfloat16

D_MODEL = 2048
ROPE_THETA = 10000.0
LN_EPS = 1e-5
BLOCK = 128
LANES = 128
A_HEADS = 16
A_KV_HEADS = 2
A_HEAD_DIM = 64
A_WINDOW = 128
A_WIDTH = A_HEADS * A_HEAD_DIM
A_KV_WIDTH = A_KV_HEADS * A_HEAD_DIM
B_PATTERNS = ((128, 1), (512, 4), (2048, 16))
B_HEADS_PER_GROUP = 4
B_HEAD_DIM = 128
B_GROUP_WIDTH = B_HEADS_PER_GROUP * B_HEAD_DIM
B_N_GROUPS = len(B_PATTERNS)
B_QKV_WIDTH = B_N_GROUPS * B_GROUP_WIDTH
B_LSE_LANES = LANES // B_HEADS_PER_GROUP
DEPTH = 1
DN_ALPHA = float((2 * DEPTH) ** 0.25)

IN_SIZES = (A_WIDTH, A_KV_WIDTH, A_KV_WIDTH, A_WIDTH, B_QKV_WIDTH, B_QKV_WIDTH, B_QKV_WIDTH,
            B_GROUP_WIDTH, D_MODEL, D_MODEL)
IN_OFFSETS = tuple(int(o) for o in np.cumsum((0,) + IN_SIZES[:-1]))
(OFF_QA, OFF_KA, OFF_VA, OFF_GATE_A, OFF_QB, OFF_KB, OFF_VB, OFF_GATE_B, OFF_MG_A, OFF_MG_B) = IN_OFFSETS
D_IN = sum(IN_SIZES)

VMEM_LIMIT_BYTES = 56 * 1024 * 1024

EP_NONE, EP_ROPE64, EP_ROPE128, EP_SILU, EP_SIGMOID = range(5)

PROJ_TM = 1024
PROJ_TN = 768

PK_MG_A = 0
PK_MG_B = PK_MG_A + D_MODEL
PK_B = PK_MG_B + D_MODEL
PK_GATE_B = PK_B + 3 * B_QKV_WIDTH
PK_A = PK_GATE_B + B_GROUP_WIDTH
PK_A_WIDTH = 2 * A_WIDTH + 2 * A_KV_WIDTH
A_COL_K = A_WIDTH
A_COL_V = A_COL_K + A_KV_WIDTH
A_COL_GATE = A_COL_V + A_KV_WIDTH
assert PK_A + PK_A_WIDTH == D_IN and PK_A % PK_A_WIDTH == 0 and PK_B % B_GROUP_WIDTH == 0
assert PK_GATE_B % B_GROUP_WIDTH == 0 and D_IN % PROJ_TN == 0


def _compiler_params(semantics):
    return pltpu.CompilerParams(dimension_semantics=semantics, vmem_limit_bytes=VMEM_LIMIT_BYTES)


def _rope_tables_kernel(pos_ref, freq_ref, cos_a_ref, sin_a1_ref, sin_a2_ref, cos_b_ref, sin_b_ref):
    ang = pos_ref[...] * freq_ref[...]
    c = jnp.cos(ang)
    s = jnp.sin(ang)
    lane = lax.broadcasted_iota(jnp.int32, c.shape, 1)

    def expand(t):
        r32 = pltpu.roll(t, 32, axis=1)
        r64 = pltpu.roll(t, 64, axis=1)
        r96 = pltpu.roll(t, 96, axis=1)
        t_a = jnp.where(lane < 32, t, jnp.where(lane < 64, r32, jnp.where(lane < 96, r64, r96)))
        t_b = jnp.where(lane < 64, r96, r32)
        return t_a, t_b

    c_a, c_b = expand(c)
    s_a, s_b = expand(s)
    first_half_a = jnp.bitwise_and(lane, A_HEAD_DIM - 1) < A_HEAD_DIM // 2
    cos_a_ref[...] = c_a
    sin_a1_ref[...] = jnp.where(first_half_a, -s_a, 0.0)
    sin_a2_ref[...] = jnp.where(first_half_a, 0.0, s_a)
    cos_b_ref[...] = c_b
    sin_b_ref[...] = jnp.where(lane < 64, -s_b, s_b)


def _rope_tables(positions):
    s_len = positions.shape[1]
    half_a, half_b = A_HEAD_DIM // 2, B_HEAD_DIM // 2
    inv_a = ROPE_THETA ** (-jnp.arange(half_a, dtype=F32) / half_a)
    inv_b = ROPE_THETA ** (-jnp.arange(half_b, dtype=F32) / half_b)
    freq = jnp.concatenate([inv_a, inv_b, jnp.zeros((LANES - half_a - half_b,), F32)])[None, :]
    pos = jnp.broadcast_to(positions.reshape(s_len, 1).astype(F32), (s_len, LANES))
    rows = 1024
    tab = jax.ShapeDtypeStruct((s_len, LANES), F32)
    spec = pl.BlockSpec((rows, LANES), lambda i: (i, 0))
    return pl.pallas_call(
        _rope_tables_kernel,
        grid=(s_len // rows,),
        in_specs=[spec, pl.BlockSpec((1, LANES), lambda i: (0, 0))],
        out_specs=[spec] * 5,
        out_shape=[tab] * 5,
        compiler_params=_compiler_params(("arbitrary",)),
        name="rope_tables",
    )(pos, freq)


def _proj_kernel(tile_kinds, x_ref, w_ref, bias_ref, cos_a_ref, sin_a1_ref, sin_a2_ref,
                 cos_b_ref, sin_b_ref, out_ref, xb_ref):
    n = pl.program_id(1)

    @pl.when(n == 0)
    def _():
        xb_ref[...] = x_ref[...].astype(BF16)

    acc = jnp.dot(xb_ref[...], w_ref[...], preferred_element_type=F32)

    def epilogue(kinds):
        for ci, kind in enumerate(kinds):
            cols = slice(ci * LANES, (ci + 1) * LANES)
            t = acc[:, cols]
            if kind == EP_ROPE64:
                t = (t * cos_a_ref[...] + pltpu.roll(t, 96, axis=1) * sin_a1_ref[...]
                     + pltpu.roll(t, 32, axis=1) * sin_a2_ref[...])
            elif kind == EP_ROPE128:
                t = t * cos_b_ref[...] + pltpu.roll(t, 64, axis=1) * sin_b_ref[...]
            elif kind == EP_SILU:
                t = t * jax.nn.sigmoid(t)
            elif kind == EP_SIGMOID:
                t = jax.nn.sigmoid(t + bias_ref[:, cols])
            out_ref[:, cols] = t.astype(out_ref.dtype)

    distinct = []
    for kinds in tile_kinds:
        if kinds not in distinct:
            distinct.append(kinds)
    for kinds in distinct:
        tiles = [i for i, k in enumerate(tile_kinds) if k == kinds]
        cond = n == tiles[0]
        for i in tiles[1:]:
            cond = cond | (n == i)
        pl.when(cond)(functools.partial(epilogue, kinds))


def _in_projection(x2d, w, bias, tables, chunk_kinds):
    s_len, k_dim = x2d.shape
    n_cols = w.shape[1]
    n_tiles = n_cols // PROJ_TN
    chunks_per_tile = PROJ_TN // LANES
    tile_kinds = tuple(tuple(chunk_kinds[t * chunks_per_tile:(t + 1) * chunks_per_tile]) for t in range(n_tiles))
    tm = min(PROJ_TM, s_len)
    row_spec = lambda width: pl.BlockSpec((tm, width), lambda m, n: (m, 0))
    return pl.pallas_call(
        functools.partial(_proj_kernel, tile_kinds),
        grid=(s_len // tm, n_tiles),
        in_specs=[row_spec(k_dim),
                  pl.BlockSpec((k_dim, PROJ_TN), lambda m, n: (0, n)),
                  pl.BlockSpec((1, PROJ_TN), lambda m, n: (0, n))] + [row_spec(LANES)] * 5,
        out_specs=pl.BlockSpec((tm, PROJ_TN), lambda m, n: (m, n)),
        out_shape=jax.ShapeDtypeStruct((s_len, n_cols), BF16),
        scratch_shapes=[pltpu.VMEM((tm, k_dim), BF16)],
        compiler_params=_compiler_params(("arbitrary", "arbitrary")),
        name="in_projection",
    )(x2d, w, bias, *tables)


def _pack_weights(w_in, b_gate):
    def cols(off, width):
        return w_in[:, off:off + width]

    g = B_GROUP_WIDTH
    parts = [cols(OFF_MG_A, D_MODEL), cols(OFF_MG_B, D_MODEL)]
    kinds = [EP_SIGMOID] * (2 * D_MODEL // LANES)
    for gi in range(B_N_GROUPS):
        parts += [cols(OFF_QB + gi * g, g), cols(OFF_KB + gi * g, g), cols(OFF_VB + gi * g, g)]
        kinds += [EP_ROPE128] * (2 * g // LANES) + [EP_NONE] * (g // LANES)
    parts += [cols(OFF_GATE_B, g), cols(OFF_QA, A_WIDTH), cols(OFF_KA, A_KV_WIDTH), cols(OFF_VA, A_KV_WIDTH),
              cols(OFF_GATE_A, A_WIDTH)]
    kinds += ([EP_SILU] * (g // LANES) + [EP_ROPE64] * ((A_WIDTH + A_KV_WIDTH) // LANES)
              + [EP_NONE] * (A_KV_WIDTH // LANES) + [EP_SILU] * (A_WIDTH // LANES))
    w = jnp.concatenate(parts, axis=1).astype(BF16)
    bias = jnp.concatenate([b_gate[0], b_gate[1], jnp.zeros((D_IN - 2 * D_MODEL,), F32)])[None, :]
    return w, kinds, bias


ATT_TOKENS = 512


def _band_bias(max_dist, reps):
    q_idx = np.arange(BLOCK)[:, None] + BLOCK
    k_idx = np.arange(2 * BLOCK)[None, :]
    dist = q_idx - k_idx
    band = (dist >= 0) & (dist <= max_dist)
    first = band & (k_idx >= BLOCK)
    both = np.stack([first, band]).astype(bool)
    bias = np.where(both, 0.0, -np.inf).astype(np.float32)
    return jnp.asarray(np.tile(bias, (1, 1, reps)))


def _block_bias(bias_ref, step, b):
    if b == 0:
        return bias_ref[jnp.where(step == 0, 0, 1)]
    return bias_ref[1]


def _swap_lane_halves(t):
    return pltpu.roll(t, LANES // 2, axis=1)


def _mixer_a_kernel(a_ref, sink_ref, bias_ref, out_ref, kbuf_ref, vbuf_ref):
    step = pl.program_id(0)
    tq = a_ref.shape[0]
    n_blocks = tq // BLOCK
    n_pairs = A_HEADS // 2
    pairs_per_group = n_pairs // A_KV_HEADS

    @pl.when(step == 0)
    def _():
        kbuf_ref[0:BLOCK, :] = jnp.zeros((BLOCK, A_KV_WIDTH), BF16)
        vbuf_ref[0:BLOCK, :] = jnp.zeros((BLOCK, A_KV_WIDTH), BF16)

    kbuf_ref[BLOCK:, :] = a_ref[:, A_COL_K:A_COL_K + A_KV_WIDTH]
    vbuf_ref[BLOCK:, :] = a_ref[:, A_COL_V:A_COL_V + A_KV_WIDTH]
    lane2 = lax.broadcasted_iota(jnp.int32, (2 * BLOCK, LANES), 1)
    low2 = lane2 < LANES // 2
    denom_cols = jnp.concatenate([jnp.where(low2, 1.0, 0.0), jnp.where(low2, 0.0, 1.0)], axis=0).astype(BF16)
    low_o = lax.broadcasted_iota(jnp.int32, (BLOCK, LANES), 1) < LANES // 2

    for b in range(n_blocks):
        bias = _block_bias(bias_ref, step, b)
        rows = slice(b * BLOCK, (b + 1) * BLOCK)
        kk = kbuf_ref[b * BLOCK:(b + 2) * BLOCK, :].astype(F32)
        vv = vbuf_ref[b * BLOCK:(b + 2) * BLOCK, :].astype(F32)
        kk_sw = _swap_lane_halves(kk)
        vv_sw = _swap_lane_halves(vv)
        k2, v2 = [], []
        for g in range(A_KV_HEADS):
            if g == 0:
                k_top, k_bot = jnp.where(low2, kk, 0.0), jnp.where(low2, 0.0, kk_sw)
                v_top, v_bot = jnp.where(low2, vv, 0.0), jnp.where(low2, 0.0, vv_sw)
            else:
                k_top, k_bot = jnp.where(low2, kk_sw, 0.0), jnp.where(low2, 0.0, kk)
                v_top, v_bot = jnp.where(low2, vv_sw, 0.0), jnp.where(low2, 0.0, vv)
            k2.append(jnp.concatenate([k_top, k_bot], axis=0).astype(BF16))
            v2.append(jnp.concatenate([jnp.concatenate([v_top, v_bot], axis=0).astype(BF16), denom_cols], axis=1))
        for p in range(n_pairs):
            g = p // pairs_per_group
            cols = slice(p * LANES, (p + 1) * LANES)
            qp = a_ref[rows, cols] * jnp.asarray(A_HEAD_DIM ** -0.5, BF16)
            s = lax.dot_general(qp, k2[g], (((1,), (1,)), ((), ())), preferred_element_type=F32) + bias
            sink0, sink1 = sink_ref[2 * p], sink_ref[2 * p + 1]
            m0 = jnp.maximum(jnp.max(s[:, :2 * BLOCK], axis=1, keepdims=True), sink0)
            m1 = jnp.maximum(jnp.max(s[:, 2 * BLOCK:], axis=1, keepdims=True), sink1)
            m0b = jnp.broadcast_to(m0, (BLOCK, LANES))
            m1b = jnp.broadcast_to(m1, (BLOCK, LANES))
            shifts = (m0b, m0b, m1b, m1b)
            prob = jnp.concatenate([jnp.exp(s[:, j * LANES:(j + 1) * LANES] - shifts[j]) for j in range(4)],
                                   axis=1).astype(BF16)
            o2 = jnp.dot(prob, v2[g], preferred_element_type=F32)
            sink_term = jnp.where(low_o, jnp.exp(sink0 - m0b), jnp.exp(sink1 - m1b))
            o = o2[:, :LANES] / (o2[:, LANES:] + sink_term)
            gate = a_ref[rows, A_COL_GATE + p * LANES:A_COL_GATE + (p + 1) * LANES].astype(F32)
            out_ref[rows, cols] = (o * gate).astype(out_ref.dtype)

    kbuf_ref[0:BLOCK, :] = kbuf_ref[tq:tq + BLOCK, :]
    vbuf_ref[0:BLOCK, :] = vbuf_ref[tq:tq + BLOCK, :]


def _mixer_a(h, sinks):
    s_len = h.shape[0]
    tq = ATT_TOKENS
    bias = _band_bias(A_WINDOW - 1, 2)
    return pl.pallas_call(
        _mixer_a_kernel,
        grid=(s_len // tq,),
        in_specs=[
            pl.BlockSpec((tq, PK_A_WIDTH), lambda i: (i, PK_A // PK_A_WIDTH)),
            pl.BlockSpec(memory_space=pltpu.SMEM),
            pl.BlockSpec(bias.shape, lambda i: (0, 0, 0)),
        ],
        out_specs=pl.BlockSpec((tq, A_WIDTH), lambda i: (i, 0)),
        out_shape=jax.ShapeDtypeStruct((s_len, A_WIDTH), BF16),
        scratch_shapes=[pltpu.VMEM((tq + BLOCK, A_KV_WIDTH), BF16), pltpu.VMEM((tq + BLOCK, A_KV_WIDTH), BF16)],
        compiler_params=_compiler_params(("arbitrary",)),
        name="mixer_a",
    )(h, sinks.astype(F32), bias)


def _mixer_b_kernel(d, nb, q_ref, k_ref, v_ref, bias_ref, o_ref, lse_ref, qsub, ksub, vsub, *stage):
    step = pl.program_id(0)
    n_heads = B_HEADS_PER_GROUP
    span = BLOCK * d

    @pl.when(step == 0)
    def _():
        ksub[:, 0:BLOCK, :] = jnp.zeros((d, BLOCK, B_GROUP_WIDTH), BF16)
        vsub[:, 0:BLOCK, :] = jnp.zeros((d, BLOCK, B_GROUP_WIDTH), BF16)

    if d == 1:
        qsub[0] = q_ref[...]
        ksub[0, BLOCK:, :] = k_ref[...]
        vsub[0, BLOCK:, :] = v_ref[...]
    else:
        slab, ostage, lstage = stage
        for ti, (src, dst, row0) in enumerate(((q_ref, qsub, 0), (k_ref, ksub, BLOCK), (v_ref, vsub, BLOCK))):
            for c in range(n_heads):
                cols = slice(c * LANES, (c + 1) * LANES)
                sl = slab.at[ti * n_heads + c]
                sl[...] = src[:, cols].astype(F32)
                for r in range(d):
                    for b in range(nb):
                        piece = sl[pl.ds(b * span + r, BLOCK, stride=d), :]
                        dst[r, row0 + b * BLOCK:row0 + (b + 1) * BLOCK, cols] = piece.astype(BF16)

    ones = jnp.ones((2 * BLOCK, LANES), BF16)
    lane = lax.broadcasted_iota(jnp.int32, (BLOCK, LANES), 1)
    scale = B_HEAD_DIM ** -0.5
    for r in range(d):
        for b in range(nb):
            bias = _block_bias(bias_ref, step, b)
            rows = slice(b * BLOCK, (b + 1) * BLOCK)
            lse_tile = jnp.zeros((BLOCK, LANES), F32)
            for h in range(n_heads):
                cols = slice(h * B_HEAD_DIM, (h + 1) * B_HEAD_DIM)
                kk = ksub[r, b * BLOCK:(b + 2) * BLOCK, cols]
                v2 = jnp.concatenate([vsub[r, b * BLOCK:(b + 2) * BLOCK, cols], ones], axis=1)
                s = lax.dot_general(qsub[r, rows, cols], kk, (((1,), (1,)), ((), ())),
                                    preferred_element_type=F32) * scale + bias
                m = jnp.max(s, axis=1, keepdims=True)
                mb = jnp.broadcast_to(m, (BLOCK, LANES))
                prob = jnp.concatenate([jnp.exp(s[:, :LANES] - mb), jnp.exp(s[:, LANES:] - mb)],
                                       axis=1).astype(BF16)
                o2 = jnp.dot(prob, v2, preferred_element_type=F32)
                denom = o2[:, LANES:]
                o = o2[:, :LANES] / denom
                lse_h = mb + jnp.log(denom)
                in_head = jnp.logical_and(lane >= h * B_LSE_LANES, lane < (h + 1) * B_LSE_LANES)
                lse_tile = jnp.where(in_head, lse_h, lse_tile)
                if d == 1:
                    o_ref[rows, cols] = o.astype(o_ref.dtype)
                else:
                    ostage[h, pl.ds(b * span + r, BLOCK, stride=d), :] = o
            if d == 1:
                lse_ref[rows, :] = lse_tile
            else:
                lstage[pl.ds(b * span + r, BLOCK, stride=d), :] = lse_tile

    if d > 1:
        for h in range(n_heads):
            o_ref[:, h * B_HEAD_DIM:(h + 1) * B_HEAD_DIM] = ostage[h].astype(o_ref.dtype)
        lse_ref[...] = lstage[...]
    ksub[:, 0:BLOCK, :] = ksub[:, nb * BLOCK:(nb + 1) * BLOCK, :]
    vsub[:, 0:BLOCK, :] = vsub[:, nb * BLOCK:(nb + 1) * BLOCK, :]


def _mixer_b_group(h, gi):
    s_len = h.shape[0]
    window, d = B_PATTERNS[gi]
    assert window // d == BLOCK
    nb = max(1, ATT_TOKENS // (BLOCK * d))
    t_rows = nb * BLOCK * d
    gw = B_GROUP_WIDTH
    col0 = (PK_B + gi * 3 * gw) // gw
    bias = _band_bias(BLOCK, 1)
    blk = lambda c: pl.BlockSpec((t_rows, gw), lambda i: (i, c))
    scratch = [pltpu.VMEM((d, nb * BLOCK, gw), BF16), pltpu.VMEM((d, (nb + 1) * BLOCK, gw), BF16),
               pltpu.VMEM((d, (nb + 1) * BLOCK, gw), BF16)]
    if d > 1:
        scratch += [pltpu.VMEM((3 * B_HEADS_PER_GROUP, t_rows, LANES), F32),
                    pltpu.VMEM((B_HEADS_PER_GROUP, t_rows, LANES), F32), pltpu.VMEM((t_rows, LANES), F32)]
    return pl.pallas_call(
        functools.partial(_mixer_b_kernel, d, nb),
        grid=(s_len // t_rows,),
        in_specs=[blk(col0), blk(col0 + 1), blk(col0 + 2), pl.BlockSpec(bias.shape, lambda i: (0, 0, 0))],
        out_specs=[pl.BlockSpec((t_rows, gw), lambda i: (i, 0)), pl.BlockSpec((t_rows, LANES), lambda i: (i, 0))],
        out_shape=[jax.ShapeDtypeStruct((s_len, gw), BF16), jax.ShapeDtypeStruct((s_len, LANES), F32)],
        scratch_shapes=scratch,
        compiler_params=_compiler_params(("arbitrary",)),
        name=f"mixer_b_d{d}",
    )(h, h, h, bias)


TAIL_TM = 256


def _tail_kernel(x_ref, ya_ref, o0_ref, o1_ref, o2_ref, l0_ref, l1_ref, l2_ref, gate_b_ref, sig_a_ref, sig_b_ref,
                 w_pa_ref, w_pb_ref, w_out_ref, ln_g_ref, ln_b_ref, out_ref):
    lses = [l0_ref[...], l1_ref[...], l2_ref[...]]
    m = jnp.maximum(jnp.maximum(lses[0], lses[1]), lses[2])
    es = [jnp.exp(l - m) for l in lses]
    inv = 1.0 / (es[0] + es[1] + es[2])
    wts = [e * inv for e in es]
    outs = [o0_ref, o1_ref, o2_ref]
    yb_cols = []
    for h in range(B_HEADS_PER_GROUP):
        cols = slice(h * B_HEAD_DIM, (h + 1) * B_HEAD_DIM)
        acc = None
        for gi in range(B_N_GROUPS):
            w_h = jnp.broadcast_to(wts[gi][:, h * B_LSE_LANES:h * B_LSE_LANES + 1], (wts[gi].shape[0], B_HEAD_DIM))
            term = w_h * outs[gi][:, cols].astype(F32)
            acc = term if acc is None else acc + term
        yb_cols.append((acc * gate_b_ref[:, cols].astype(F32)).astype(BF16))
    yb = jnp.concatenate(yb_cols, axis=1)
    y_a = jnp.dot(ya_ref[...], w_pa_ref[...], preferred_element_type=F32)
    y_b = jnp.dot(yb, w_pb_ref[...], preferred_element_type=F32)
    merged = sig_a_ref[...].astype(F32) * y_a + sig_b_ref[...].astype(F32) * y_b
    sub = jnp.dot(merged.astype(BF16), w_out_ref[...], preferred_element_type=F32)
    z = DN_ALPHA * x_ref[...] + sub
    mu = jnp.mean(z, axis=-1, keepdims=True)
    zc = z - mu
    var = jnp.mean(zc * zc, axis=-1, keepdims=True)
    out_ref[...] = (zc * lax.rsqrt(var + LN_EPS) * ln_g_ref[...] + ln_b_ref[...]).astype(out_ref.dtype)


def _tail(x2d, ya, o_groups, lse_groups, h, w_pa, w_pb, w_out, ln_g, ln_b):
    s_len = x2d.shape[0]
    tm = TAIL_TM
    row = lambda width, cblk=0: pl.BlockSpec((tm, width), lambda i: (i, cblk))
    full = lambda a: pl.BlockSpec(a.shape, lambda i: (0,) * a.ndim)
    gw = B_GROUP_WIDTH
    return pl.pallas_call(
        _tail_kernel,
        grid=(s_len // tm,),
        in_specs=[row(D_MODEL), row(A_WIDTH), row(gw), row(gw), row(gw), row(LANES), row(LANES), row(LANES),
                  row(gw, PK_GATE_B // gw), row(D_MODEL, PK_MG_A // D_MODEL), row(D_MODEL, PK_MG_B // D_MODEL),
                  full(w_pa), full(w_pb), full(w_out), full(ln_g), full(ln_b)],
        out_specs=row(D_MODEL),
        out_shape=jax.ShapeDtypeStruct((s_len, D_MODEL), x2d.dtype),
        compiler_params=_compiler_params(("arbitrary",)),
        name="tail",
    )(x2d, ya, *o_groups, *lse_groups, h, h, h, w_pa, w_pb, w_out, ln_g, ln_b)


def _hybrid_layer(x, positions, w_in, b_gate, sinks, w_pa, w_pb, w_out, ln_g, ln_b):
    bn, s_len, d_model = x.shape
    assert bn == 1 and d_model == D_MODEL
    assert s_len % (BLOCK * B_PATTERNS[-1][1]) == 0 and s_len % PROJ_TM == 0
    x2d = x.reshape(s_len, d_model)
    tables = _rope_tables(positions)
    w, kinds, bias = _pack_weights(w_in, b_gate)
    h = _in_projection(x2d, w, bias, tables, kinds)
    o_groups, lse_groups = [], []
    for gi in range(B_N_GROUPS):
        o, lse = _mixer_b_group(h, gi)
        o_groups.append(o)
        lse_groups.append(lse)
    ya = _mixer_a(h, sinks)
    out = _tail(x2d, ya, o_groups, lse_groups, h, w_pa.astype(BF16), w_pb.astype(BF16), w_out.astype(BF16),
                ln_g.reshape(1, d_model).astype(F32), ln_b.reshape(1, d_model).astype(F32))
    return out.reshape(bn, s_len, d_model)


def kernel(x, positions, w_in, b_gate, sinks, w_pa, w_pb, w_out, ln_g, ln_b):
    for layer in range(w_in.shape[0]):
        x = _hybrid_layer(x, positions, w_in[layer], b_gate[layer], sinks[layer], w_pa[layer], w_pb[layer],
                          w_out[layer], ln_g[layer], ln_b[layer])
    return x
```

```python
import functools

import jax
import jax.numpy as jnp
import numpy as np
from jax import lax
from jax.experimental import pallas as pl
from jax.experimental.pallas import tpu as pltpu

F32 = jnp.float32
BF16 = jnp.bfloat16

D_MODEL = 2048
ROPE_THETA = 10000.0
LN_EPS = 1e-5
BLOCK = 128
LANES = 128
A_HEADS = 16
A_KV_HEADS = 2
A_HEAD_DIM = 64
A_WINDOW = 128
A_WIDTH = A_HEADS * A_HEAD_DIM
A_KV_WIDTH = A_KV_HEADS * A_HEAD_DIM
B_PATTERNS = ((128, 1), (512, 4), (2048, 16))
B_HEADS_PER_GROUP = 4
B_HEAD_DIM = 128
B_GROUP_WIDTH = B_HEADS_PER_GROUP * B_HEAD_DIM
B_N_GROUPS = len(B_PATTERNS)
B_QKV_WIDTH = B_N_GROUPS * B_GROUP_WIDTH
B_LSE_LANES = LANES // B_HEADS_PER_GROUP
DEPTH = 1
DN_ALPHA = float((2 * DEPTH) ** 0.25)

IN_SIZES = (A_WIDTH, A_KV_WIDTH, A_KV_WIDTH, A_WIDTH, B_QKV_WIDTH, B_QKV_WIDTH, B_QKV_WIDTH,
            B_GROUP_WIDTH, D_MODEL, D_MODEL)
IN_OFFSETS = tuple(int(o) for o in np.cumsum((0,) + IN_SIZES[:-1]))
(OFF_QA, OFF_KA, OFF_VA, OFF_GATE_A, OFF_QB, OFF_KB, OFF_VB, OFF_GATE_B, OFF_MG_A, OFF_MG_B) = IN_OFFSETS
D_IN = sum(IN_SIZES)

VMEM_LIMIT_BYTES = 56 * 1024 * 1024

EP_NONE, EP_ROPE64, EP_ROPE128, EP_SILU, EP_SIGMOID = range(5)

PROJ_TM = 1024
PROJ_TN = 768

PK_MG_A = 0
PK_MG_B = PK_MG_A + D_MODEL
PK_B = PK_MG_B + D_MODEL
PK_GATE_B = PK_B + 3 * B_QKV_WIDTH
PK_A = PK_GATE_B + B_GROUP_WIDTH
PK_A_WIDTH = 2 * A_WIDTH + 2 * A_KV_WIDTH
A_COL_K = A_WIDTH
A_COL_V = A_COL_K + A_KV_WIDTH
A_COL_GATE = A_COL_V + A_KV_WIDTH
assert PK_A + PK_A_WIDTH == D_IN and PK_A % PK_A_WIDTH == 0 and PK_B % B_GROUP_WIDTH == 0
assert PK_GATE_B % B_GROUP_WIDTH == 0 and D_IN % PROJ_TN == 0


def _compiler_params(semantics):
    return pltpu.CompilerParams(dimension_semantics=semantics, vmem_limit_bytes=VMEM_LIMIT_BYTES)


def _rope_tables_kernel(pos_ref, freq_ref, cos_a_ref, sin_a1_ref, sin_a2_ref, cos_b_ref, sin_b_ref):
    ang = pos_ref[...] * freq_ref[...]
    c = jnp.cos(ang)
    s = jnp.sin(ang)
    lane = lax.broadcasted_iota(jnp.int32, c.shape, 1)

    def expand(t):
        r32 = pltpu.roll(t, 32, axis=1)
        r64 = pltpu.roll(t, 64, axis=1)
        r96 = pltpu.roll(t, 96, axis=1)
        t_a = jnp.where(lane < 32, t, jnp.where(lane < 64, r32, jnp.where(lane < 96, r64, r96)))
        t_b = jnp.where(lane < 64, r96, r32)
        return t_a, t_b

    c_a, c_b = expand(c)
    s_a, s_b = expand(s)
    first_half_a = jnp.bitwise_and(lane, A_HEAD_DIM - 1) < A_HEAD_DIM // 2
    cos_a_ref[...] = c_a
    sin_a1_ref[...] = jnp.where(first_half_a, -s_a, 0.0)
    sin_a2_ref[...] = jnp.where(first_half_a, 0.0, s_a)
    cos_b_ref[...] = c_b
    sin_b_ref[...] = jnp.where(lane < 64, -s_b, s_b)


def _rope_tables(positions):
    s_len = positions.shape[1]
    half_a, half_b = A_HEAD_DIM // 2, B_HEAD_DIM // 2
    inv_a = ROPE_THETA ** (-jnp.arange(half_a, dtype=F32) / half_a)
    inv_b = ROPE_THETA ** (-jnp.arange(half_b, dtype=F32) / half_b)
    freq = jnp.concatenate([inv_a, inv_b, jnp.zeros((LANES - half_a - half_b,), F32)])[None, :]
    pos = jnp.broadcast_to(positions.reshape(s_len, 1).astype(F32), (s_len, LANES))
    rows = 1024
    tab = jax.ShapeDtypeStruct((s_len, LANES), F32)
    spec = pl.BlockSpec((rows, LANES), lambda i: (i, 0))
    return pl.pallas_call(
        _rope_tables_kernel,
        grid=(s_len // rows,),
        in_specs=[spec, pl.BlockSpec((1, LANES), lambda i: (0, 0))],
        out_specs=[spec] * 5,
        out_shape=[tab] * 5,
        compiler_params=_compiler_params(("arbitrary",)),
        name="rope_tables",
    )(pos, freq)


def _proj_kernel(tile_kinds, x_ref, w_ref, bias_ref, cos_a_ref, sin_a1_ref, sin_a2_ref,
                 cos_b_ref, sin_b_ref, out_ref, xb_ref):
    n = pl.program_id(1)

    @pl.when(n == 0)
    def _():
        xb_ref[...] = x_ref[...].astype(BF16)

    def epilogue(kinds):
        acc = jnp.dot(xb_ref[...], w_ref[...], preferred_element_type=F32)
        for ci, kind in enumerate(kinds):
            cols = slice(ci * LANES, (ci + 1) * LANES)
            t = acc[:, cols]
            if kind == EP_ROPE64:
                t = (t * cos_a_ref[...] + pltpu.roll(t, 96, axis=1) * sin_a1_ref[...]
                     + pltpu.roll(t, 32, axis=1) * sin_a2_ref[...])
            elif kind == EP_ROPE128:
                t = t * cos_b_ref[...] + pltpu.roll(t, 64, axis=1) * sin_b_ref[...]
            elif kind == EP_SILU:
                t = t * jax.nn.sigmoid(t)
            elif kind == EP_SIGMOID:
                t = jax.nn.sigmoid(t + bias_ref[:, cols])
            out_ref[:, cols] = t.astype(out_ref.dtype)

    distinct = []
    for kinds in tile_kinds:
        if kinds not in distinct:
            distinct.append(kinds)
    for kinds in distinct:
        tiles = [i for i, k in enumerate(tile_kinds) if k == kinds]
        cond = n == tiles[0]
        for i in tiles[1:]:
            cond = cond | (n == i)
        pl.when(cond)(functools.partial(epilogue, kinds))


def _in_projection(x2d, w, bias, tables, chunk_kinds):
    s_len, k_dim = x2d.shape
    n_cols = w.shape[1]
    n_tiles = n_cols // PROJ_TN
    chunks_per_tile = PROJ_TN // LANES
    tile_kinds = tuple(tuple(chunk_kinds[t * chunks_per_tile:(t + 1) * chunks_per_tile]) for t in range(n_tiles))
    tm = min(PROJ_TM, s_len)
    row_spec = lambda width: pl.BlockSpec((tm, width), lambda m, n: (m, 0))
    return pl.pallas_call(
        functools.partial(_proj_kernel, tile_kinds),
        grid=(s_len // tm, n_tiles),
        in_specs=[row_spec(k_dim),
                  pl.BlockSpec((k_dim, PROJ_TN), lambda m, n: (0, n)),
                  pl.BlockSpec((1, PROJ_TN), lambda m, n: (0, n))] + [row_spec(LANES)] * 5,
        out_specs=pl.BlockSpec((tm, PROJ_TN), lambda m, n: (m, n)),
        out_shape=jax.ShapeDtypeStruct((s_len, n_cols), BF16),
        scratch_shapes=[pltpu.VMEM((tm, k_dim), BF16)],
        compiler_params=_compiler_params(("arbitrary", "arbitrary")),
        name="in_projection",
    )(x2d, w, bias, *tables)


PACK_COLS = 256
PACK_BLOCKS_PER_STEP = PROJ_TN // PACK_COLS


def _packed_layout():
    g = B_GROUP_WIDTH
    assert OFF_VA == OFF_KA + A_KV_WIDTH
    segments = [(OFF_MG_A, D_MODEL, [EP_SIGMOID]), (OFF_MG_B, D_MODEL, [EP_SIGMOID])]
    for gi in range(B_N_GROUPS):
        segments += [(OFF_QB + gi * g, g, [EP_ROPE128]), (OFF_KB + gi * g, g, [EP_ROPE128]),
                     (OFF_VB + gi * g, g, [EP_NONE])]
    segments += [(OFF_GATE_B, g, [EP_SILU]), (OFF_QA, A_WIDTH, [EP_ROPE64]),
                 (OFF_KA, 2 * A_KV_WIDTH, [EP_ROPE64] * (A_KV_WIDTH // LANES) + [EP_NONE] * (A_KV_WIDTH // LANES)),
                 (OFF_GATE_A, A_WIDTH, [EP_SILU])]
    perm, kinds = [], []
    for off, width, seg_kinds in segments:
        assert off % PACK_COLS == 0 and width % PACK_COLS == 0
        perm += [off // PACK_COLS + j for j in range(width // PACK_COLS)]
        kinds += seg_kinds * (width // LANES // len(seg_kinds))
    assert sorted(perm) == list(range(D_IN // PACK_COLS)) and len(kinds) == D_IN // LANES
    return np.asarray(perm, np.int32), kinds


def _pack_kernel(perm_ref, *refs):
    del perm_ref
    out_ref = refs[-1]
    for j, w_ref in enumerate(refs[:-1]):
        out_ref[:, j * PACK_COLS:(j + 1) * PACK_COLS] = w_ref[...].astype(out_ref.dtype)


def _pack_weights(w_in, b_gate):
    perm, kinds = _packed_layout()
    k_dim = w_in.shape[0]
    n_steps = D_IN // PROJ_TN
    src = lambda j: pl.BlockSpec((k_dim, PACK_COLS), lambda i, perm_ref: (0, perm_ref[i * PACK_BLOCKS_PER_STEP + j]))
    w = pl.pallas_call(
        _pack_kernel,
        grid_spec=pltpu.PrefetchScalarGridSpec(
            num_scalar_prefetch=1, grid=(n_steps,),
            in_specs=[src(j) for j in range(PACK_BLOCKS_PER_STEP)],
            out_specs=pl.BlockSpec((k_dim, PROJ_TN), lambda i, perm_ref: (0, i))),
        out_shape=jax.ShapeDtypeStruct((k_dim, D_IN), BF16),
        compiler_params=_compiler_params(("arbitrary",)),
        name="pack_weights",
    )(jnp.asarray(perm), *([w_in] * PACK_BLOCKS_PER_STEP))
    bias = jnp.concatenate([b_gate[0], b_gate[1], jnp.zeros((D_IN - 2 * D_MODEL,), F32)])[None, :]
    return w, kinds, bias


ATT_TOKENS = 512


def _band_bias(max_dist, reps):
    q_idx = np.arange(BLOCK)[:, None] + BLOCK
    k_idx = np.arange(2 * BLOCK)[None, :]
    dist = q_idx - k_idx
    band = (dist >= 0) & (dist <= max_dist)
    first = band & (k_idx >= BLOCK)
    both = np.stack([first, band]).astype(bool)
    bias = np.where(both, 0.0, -np.inf).astype(np.float32)
    return jnp.asarray(np.tile(bias, (1, 1, reps)))


def _block_bias(bias_ref, step, b):
    if b == 0:
        return bias_ref[jnp.where(step == 0, 0, 1)]
    return bias_ref[1]


def _swap_lane_halves(t):
    return pltpu.roll(t, LANES // 2, axis=1)


def _mixer_a_kernel(a_ref, sink_ref, bias_ref, out_ref, kbuf_ref, vbuf_ref):
    step = pl.program_id(0)
    tq = a_ref.shape[0]
    n_blocks = tq // BLOCK
    n_pairs = A_HEADS // 2
    pairs_per_group = n_pairs // A_KV_HEADS

    @pl.when(step == 0)
    def _():
        kbuf_ref[0:BLOCK, :] = jnp.zeros((BLOCK, A_KV_WIDTH), BF16)
        vbuf_ref[0:BLOCK, :] = jnp.zeros((BLOCK, A_KV_WIDTH), BF16)

    kbuf_ref[BLOCK:, :] = a_ref[:, A_COL_K:A_COL_K + A_KV_WIDTH]
    vbuf_ref[BLOCK:, :] = a_ref[:, A_COL_V:A_COL_V + A_KV_WIDTH]
    lane2 = lax.broadcasted_iota(jnp.int32, (2 * BLOCK, LANES), 1)
    low2 = lane2 < LANES // 2
    denom_cols = jnp.concatenate([jnp.where(low2, 1.0, 0.0), jnp.where(low2, 0.0, 1.0)], axis=0).astype(BF16)
    low_o = lax.broadcasted_iota(jnp.int32, (BLOCK, LANES), 1) < LANES // 2

    for b in range(n_blocks):
        bias = _block_bias(bias_ref, step, b)
        rows = slice(b * BLOCK, (b + 1) * BLOCK)
        kk = kbuf_ref[b * BLOCK:(b + 2) * BLOCK, :].astype(F32)
        vv = vbuf_ref[b * BLOCK:(b + 2) * BLOCK, :].astype(F32)
        kk_sw = _swap_lane_halves(kk)
        vv_sw = _swap_lane_halves(vv)
        k2, v2 = [], []
        for g in range(A_KV_HEADS):
            if g == 0:
                k_top, k_bot = jnp.where(low2, kk, 0.0), jnp.where(low2, 0.0, kk_sw)
                v_top, v_bot = jnp.where(low2, vv, 0.0), jnp.where(low2, 0.0, vv_sw)
            else:
                k_top, k_bot = jnp.where(low2, kk_sw, 0.0), jnp.where(low2, 0.0, kk)
                v_top, v_bot = jnp.where(low2, vv_sw, 0.0), jnp.where(low2, 0.0, vv)
            k2.append(jnp.concatenate([k_top, k_bot], axis=0).astype(BF16))
            v2.append(jnp.concatenate([jnp.concatenate([v_top, v_bot], axis=0).astype(BF16), denom_cols], axis=1))
        for p in range(n_pairs):
            g = p // pairs_per_group
            cols = slice(p * LANES, (p + 1) * LANES)
            qp = a_ref[rows, cols] * jnp.asarray(A_HEAD_DIM ** -0.5, BF16)
            s = lax.dot_general(qp, k2[g], (((1,), (1,)), ((), ())), preferred_element_type=F32) + bias
            sink0, sink1 = sink_ref[2 * p], sink_ref[2 * p + 1]
            m0 = jnp.maximum(jnp.max(s[:, :2 * BLOCK], axis=1, keepdims=True), sink0)
            m1 = jnp.maximum(jnp.max(s[:, 2 * BLOCK:], axis=1, keepdims=True), sink1)
            m0b = jnp.broadcast_to(m0, (BLOCK, LANES))
            m1b = jnp.broadcast_to(m1, (BLOCK, LANES))
            shifts = (m0b, m0b, m1b, m1b)
            prob = jnp.concatenate([jnp.exp(s[:, j * LANES:(j + 1) * LANES] - shifts[j]) for j in range(4)],
                                   axis=1).astype(BF16)
            o2 = jnp.dot(prob, v2[g], preferred_element_type=F32)
            sink_term = jnp.where(low_o, jnp.exp(sink0 - m0b), jnp.exp(sink1 - m1b))
            o = o2[:, :LANES] / (o2[:, LANES:] + sink_term)
            gate = a_ref[rows, A_COL_GATE + p * LANES:A_COL_GATE + (p + 1) * LANES].astype(F32)
            out_ref[rows, cols] = (o * gate).astype(out_ref.dtype)

    kbuf_ref[0:BLOCK, :] = kbuf_ref[tq:tq + BLOCK, :]
    vbuf_ref[0:BLOCK, :] = vbuf_ref[tq:tq + BLOCK, :]


def _mixer_a(h, sinks):
    s_len = h.shape[0]
    tq = ATT_TOKENS
    bias = _band_bias(A_WINDOW - 1, 2)
    return pl.pallas_call(
        _mixer_a_kernel,
        grid=(s_len // tq,),
        in_specs=[
            pl.BlockSpec((tq, PK_A_WIDTH), lambda i: (i, PK_A // PK_A_WIDTH)),
            pl.BlockSpec(memory_space=pltpu.SMEM),
            pl.BlockSpec(bias.shape, lambda i: (0, 0, 0)),
        ],
        out_specs=pl.BlockSpec((tq, A_WIDTH), lambda i: (i, 0)),
        out_shape=jax.ShapeDtypeStruct((s_len, A_WIDTH), BF16),
        scratch_shapes=[pltpu.VMEM((tq + BLOCK, A_KV_WIDTH), BF16), pltpu.VMEM((tq + BLOCK, A_KV_WIDTH), BF16)],
        compiler_params=_compiler_params(("arbitrary",)),
        name="mixer_a",
    )(h, sinks.astype(F32), bias)


def _mixer_b_kernel(d, nb, q_ref, k_ref, v_ref, bias_ref, o_ref, lse_ref, qsub, ksub, vsub, *stage):
    step = pl.program_id(0)
    n_heads = B_HEADS_PER_GROUP
    span = BLOCK * d

    @pl.when(step == 0)
    def _():
        ksub[:, 0:BLOCK, :] = jnp.zeros((d, BLOCK, B_GROUP_WIDTH), BF16)
        vsub[:, 0:BLOCK, :] = jnp.zeros((d, BLOCK, B_GROUP_WIDTH), BF16)

    if d == 1:
        qsub[0] = q_ref[...]
        ksub[0, BLOCK:, :] = k_ref[...]
        vsub[0, BLOCK:, :] = v_ref[...]
    else:
        slab, ostage, lstage = stage
        for ti, (src, dst, row0) in enumerate(((q_ref, qsub, 0), (k_ref, ksub, BLOCK), (v_ref, vsub, BLOCK))):
            for c in range(n_heads):
                cols = slice(c * LANES, (c + 1) * LANES)
                sl = slab.at[ti * n_heads + c]
                sl[...] = src[:, cols].astype(F32)
                for r in range(d):
                    for b in range(nb):
                        piece = sl[pl.ds(b * span + r, BLOCK, stride=d), :]
                        dst[r, row0 + b * BLOCK:row0 + (b + 1) * BLOCK, cols] = piece.astype(BF16)

    ones = jnp.ones((2 * BLOCK, LANES), BF16)
    lane = lax.broadcasted_iota(jnp.int32, (BLOCK, LANES), 1)
    scale = B_HEAD_DIM ** -0.5
    for r in range(d):
        for b in range(nb):
            bias = _block_bias(bias_ref, step, b)
            rows = slice(b * BLOCK, (b + 1) * BLOCK)
            lse_tile = jnp.zeros((BLOCK, LANES), F32)
            for h in range(n_heads):
                cols = slice(h * B_HEAD_DIM, (h + 1) * B_HEAD_DIM)
                kk = ksub[r, b * BLOCK:(b + 2) * BLOCK, cols]
                v2 = jnp.concatenate([vsub[r, b * BLOCK:(b + 2) * BLOCK, cols], ones], axis=1)
                s = lax.dot_general(qsub[r, rows, cols], kk, (((1,), (1,)), ((), ())),
                                    preferred_element_type=F32) * scale + bias
                m = jnp.max(s, axis=1, keepdims=True)
                mb = jnp.broadcast_to(m, (BLOCK, LANES))
                prob = jnp.concatenate([jnp.exp(s[:, :LANES] - mb), jnp.exp(s[:, LANES:] - mb)],
                                       axis=1).astype(BF16)
                o2 = jnp.dot(prob, v2, preferred_element_type=F32)
                denom = o2[:, LANES:]
                o = o2[:, :LANES] / denom
                lse_h = mb + jnp.log(denom)
                in_head = jnp.logical_and(lane >= h * B_LSE_LANES, lane < (h + 1) * B_LSE_LANES)
                lse_tile = jnp.where(in_head, lse_h, lse_tile)
                if d == 1:
                    o_ref[rows, cols] = o.astype(o_ref.dtype)
                else:
                    ostage[h, pl.ds(b * span + r, BLOCK, stride=d), :] = o
            if d == 1:
                lse_ref[rows, :] = lse_tile
            else:
                lstage[pl.ds(b * span + r, BLOCK, stride=d), :] = lse_tile

    if d > 1:
        for h in range(n_heads):
            o_ref[:, h * B_HEAD_DIM:(h + 1) * B_HEAD_DIM] = ostage[h].astype(o_ref.dtype)
        lse_ref[...] = lstage[...]
    ksub[:, 0:BLOCK, :] = ksub[:, nb * BLOCK:(nb + 1) * BLOCK, :]
    vsub[:, 0:BLOCK, :] = vsub[:, nb * BLOCK:(nb + 1) * BLOCK, :]


def _mixer_b_group(h, gi):
    s_len = h.shape[0]
    window, d = B_PATTERNS[gi]
    assert window // d == BLOCK
    nb = max(1, ATT_TOKENS // (BLOCK * d))
    t_rows = nb * BLOCK * d
    gw = B_GROUP_WIDTH
    col0 = (PK_B + gi * 3 * gw) // gw
    bias = _band_bias(BLOCK, 1)
    blk = lambda c: pl.BlockSpec((t_rows, gw), lambda i: (i, c))
    scratch = [pltpu.VMEM((d, nb * BLOCK, gw), BF16), pltpu.VMEM((d, (nb + 1) * BLOCK, gw), BF16),
               pltpu.VMEM((d, (nb + 1) * BLOCK, gw), BF16)]
    if d > 1:
        scratch += [pltpu.VMEM((3 * B_HEADS_PER_GROUP, t_rows, LANES), F32),
                    pltpu.VMEM((B_HEADS_PER_GROUP, t_rows, LANES), F32), pltpu.VMEM((t_rows, LANES), F32)]
    return pl.pallas_call(
        functools.partial(_mixer_b_kernel, d, nb),
        grid=(s_len // t_rows,),
        in_specs=[blk(col0), blk(col0 + 1), blk(col0 + 2), pl.BlockSpec(bias.shape, lambda i: (0, 0, 0))],
        out_specs=[pl.BlockSpec((t_rows, gw), lambda i: (i, 0)), pl.BlockSpec((t_rows, LANES), lambda i: (i, 0))],
        out_shape=[jax.ShapeDtypeStruct((s_len, gw), BF16), jax.ShapeDtypeStruct((s_len, LANES), F32)],
        scratch_shapes=scratch,
        compiler_params=_compiler_params(("arbitrary",)),
        name=f"mixer_b_d{d}",
    )(h, h, h, bias)


TAIL_TM = 256


def _tail_kernel(x_ref, ya_ref, o0_ref, o1_ref, o2_ref, l0_ref, l1_ref, l2_ref, gate_b_ref, sig_a_ref, sig_b_ref,
                 w_pa_ref, w_pb_ref, w_out_ref, ln_g_ref, ln_b_ref, out_ref):
    lses = [l0_ref[...], l1_ref[...], l2_ref[...]]
    m = jnp.maximum(jnp.maximum(lses[0], lses[1]), lses[2])
    es = [jnp.exp(l - m) for l in lses]
    inv = 1.0 / (es[0] + es[1] + es[2])
    wts = [e * inv for e in es]
    outs = [o0_ref, o1_ref, o2_ref]
    yb_cols = []
    for h in range(B_HEADS_PER_GROUP):
        cols = slice(h * B_HEAD_DIM, (h + 1) * B_HEAD_DIM)
        acc = None
        for gi in range(B_N_GROUPS):
            w_h = jnp.broadcast_to(wts[gi][:, h * B_LSE_LANES:h * B_LSE_LANES + 1], (wts[gi].shape[0], B_HEAD_DIM))
            term = w_h * outs[gi][:, cols].astype(F32)
            acc = term if acc is None else acc + term
        yb_cols.append((acc * gate_b_ref[:, cols].astype(F32)).astype(BF16))
    yb = jnp.concatenate(yb_cols, axis=1)
    y_a = jnp.dot(ya_ref[...], w_pa_ref[...], preferred_element_type=F32)
    y_b = jnp.dot(yb, w_pb_ref[...], preferred_element_type=F32)
    merged = sig_a_ref[...].astype(F32) * y_a + sig_b_ref[...].astype(F32) * y_b
    sub = jnp.dot(merged.astype(BF16), w_out_ref[...], preferred_element_type=F32)
    z = DN_ALPHA * x_ref[...] + sub
    mu = jnp.mean(z, axis=-1, keepdims=True)
    zc = z - mu
    var = jnp.mean(zc * zc, axis=-1, keepdims=True)
    out_ref[...] = (zc * lax.rsqrt(var + LN_EPS) * ln_g_ref[...] + ln_b_ref[...]).astype(out_ref.dtype)


def _tail(x2d, ya, o_groups, lse_groups, h, w_pa, w_pb, w_out, ln_g, ln_b):
    s_len = x2d.shape[0]
    tm = TAIL_TM
    row = lambda width, cblk=0: pl.BlockSpec((tm, width), lambda i: (i, cblk))
    full = lambda a: pl.BlockSpec(a.shape, lambda i: (0,) * a.ndim)
    gw = B_GROUP_WIDTH
    return pl.pallas_call(
        _tail_kernel,
        grid=(s_len // tm,),
        in_specs=[row(D_MODEL), row(A_WIDTH), row(gw), row(gw), row(gw), row(LANES), row(LANES), row(LANES),
                  row(gw, PK_GATE_B // gw), row(D_MODEL, PK_MG_A // D_MODEL), row(D_MODEL, PK_MG_B // D_MODEL),
                  full(w_pa), full(w_pb), full(w_out), full(ln_g), full(ln_b)],
        out_specs=row(D_MODEL),
        out_shape=jax.ShapeDtypeStruct((s_len, D_MODEL), x2d.dtype),
        compiler_params=_compiler_params(("arbitrary",)),
        name="tail",
    )(x2d, ya, *o_groups, *lse_groups, h, h, h, w_pa, w_pb, w_out, ln_g, ln_b)


def _hybrid_layer(x, positions, w_in, b_gate, sinks, w_pa, w_pb, w_out, ln_g, ln_b):
    bn, s_len, d_model = x.shape
    assert bn == 1 and d_model == D_MODEL
    assert s_len % (BLOCK * B_PATTERNS[-1][1]) == 0 and s_len % PROJ_TM == 0
    x2d = x.reshape(s_len, d_model)
    tables = _rope_tables(positions)
    w, kinds, bias = _pack_weights(w_in, b_gate)
    h = _in_projection(x2d, w, bias, tables, kinds)
    o_groups, lse_groups = [], []
    for gi in range(B_N_GROUPS):
        o, lse = _mixer_b_group(h, gi)
        o_groups.append(o)
        lse_groups.append(lse)
    ya = _mixer_a(h, sinks)
    out = _tail(x2d, ya, o_groups, lse_groups, h, w_pa.astype(BF16), w_pb.astype(BF16), w_out.astype(BF16),
                ln_g.reshape(1, d_model).astype(F32), ln_b.reshape(1, d_model).astype(F32))
    return out.reshape(bn, s_len, d_model)


def kernel(x, positions, w_in, b_gate, sinks, w_pa, w_pb, w_out, ln_g, ln_b):
    for layer in range(w_in.shape[0]):
        x = _hybrid_layer(x, positions, w_in[layer], b_gate[layer], sinks[layer], w_pa[layer], w_pb[layer],
                          w_out[layer], ln_g[layer], ln_b[layer])
    return x
```

```python
import functools

import jax
import jax.numpy as jnp
import numpy as np
from jax import lax
from jax.experimental import pallas as pl
from jax.experimental.pallas import tpu as pltpu

F32 = jnp.float32
BF16 = jnp.bfloat16

D_MODEL = 2048
ROPE_THETA = 10000.0
LN_EPS = 1e-5
BLOCK = 128
LANES = 128
A_HEADS = 16
A_KV_HEADS = 2
A_HEAD_DIM = 64
A_WINDOW = 128
A_WIDTH = A_HEADS * A_HEAD_DIM
A_KV_WIDTH = A_KV_HEADS * A_HEAD_DIM
B_PATTERNS = ((128, 1), (512, 4), (2048, 16))
B_HEADS_PER_GROUP = 4
B_HEAD_DIM = 128
B_GROUP_WIDTH = B_HEADS_PER_GROUP * B_HEAD_DIM
B_N_GROUPS = len(B_PATTERNS)
B_QKV_WIDTH = B_N_GROUPS * B_GROUP_WIDTH
B_LSE_LANES = LANES // B_HEADS_PER_GROUP
DEPTH = 1
DN_ALPHA = float((2 * DEPTH) ** 0.25)

IN_SIZES = (A_WIDTH, A_KV_WIDTH, A_KV_WIDTH, A_WIDTH, B_QKV_WIDTH, B_QKV_WIDTH, B_QKV_WIDTH,
            B_GROUP_WIDTH, D_MODEL, D_MODEL)
IN_OFFSETS = tuple(int(o) for o in np.cumsum((0,) + IN_SIZES[:-1]))
(OFF_QA, OFF_KA, OFF_VA, OFF_GATE_A, OFF_QB, OFF_KB, OFF_VB, OFF_GATE_B, OFF_MG_A, OFF_MG_B) = IN_OFFSETS
D_IN = sum(IN_SIZES)

VMEM_LIMIT_BYTES = 56 * 1024 * 1024

EP_NONE, EP_ROPE64, EP_ROPE128, EP_SILU, EP_SIGMOID = range(5)

PROJ_TM = 1024
PROJ_TN = 1280

PK_MG_A = 0
PK_MG_B = PK_MG_A + D_MODEL
PK_B = PK_MG_B + D_MODEL
PK_GATE_B = PK_B + 3 * B_QKV_WIDTH
PK_A = PK_GATE_B + B_GROUP_WIDTH
PK_A_WIDTH = 2 * A_WIDTH + 2 * A_KV_WIDTH
A_COL_K = A_WIDTH
A_COL_V = A_COL_K + A_KV_WIDTH
A_COL_GATE = A_COL_V + A_KV_WIDTH
assert PK_A + PK_A_WIDTH == D_IN and PK_A % PK_A_WIDTH == 0 and PK_B % B_GROUP_WIDTH == 0
assert PK_GATE_B % B_GROUP_WIDTH == 0 and D_IN % PROJ_TN == 0


def _compiler_params(semantics):
    return pltpu.CompilerParams(dimension_semantics=semantics, vmem_limit_bytes=VMEM_LIMIT_BYTES)


TAB_COS_A, TAB_SIN_A_LOW, TAB_SIN_A_HIGH, TAB_COS_B, TAB_SIN_B = range(5)


def _rope_tables(pos, freq):
    ang = pos * freq
    c = jnp.cos(ang)
    s = jnp.sin(ang)
    lane = lax.broadcasted_iota(jnp.int32, c.shape, 1)

    def expand(t):
        r32 = pltpu.roll(t, 32, axis=1)
        r64 = pltpu.roll(t, 64, axis=1)
        r96 = pltpu.roll(t, 96, axis=1)
        t_a = jnp.where(lane < 32, t, jnp.where(lane < 64, r32, jnp.where(lane < 96, r64, r96)))
        t_b = jnp.where(lane < 64, r96, r32)
        return t_a, t_b

    c_a, c_b = expand(c)
    s_a, s_b = expand(s)
    first_half_a = jnp.bitwise_and(lane, A_HEAD_DIM - 1) < A_HEAD_DIM // 2
    return (c_a, jnp.where(first_half_a, -s_a, 0.0), jnp.where(first_half_a, 0.0, s_a),
            c_b, jnp.where(lane < 64, -s_b, s_b))


def _rope_inputs(positions):
    s_len = positions.shape[1]
    half_a, half_b = A_HEAD_DIM // 2, B_HEAD_DIM // 2
    inv_a = ROPE_THETA ** (-jnp.arange(half_a, dtype=F32) / half_a)
    inv_b = ROPE_THETA ** (-jnp.arange(half_b, dtype=F32) / half_b)
    freq = jnp.concatenate([inv_a, inv_b, jnp.zeros((LANES - half_a - half_b,), F32)])[None, :]
    pos = jnp.broadcast_to(positions.reshape(s_len, 1).astype(F32), (s_len, LANES))
    return pos, freq


def _proj_kernel(tile_kinds, table_rows, x_ref, w_ref, bias_ref, pos_ref, freq_ref, out_ref, xb_ref, tab_ref):
    n = pl.program_id(1)
    tm = x_ref.shape[0]
    has_rope = [any(k in (EP_ROPE64, EP_ROPE128) for k in kinds) for kinds in tile_kinds]
    n_table_tiles = has_rope.index(True)
    assert n_table_tiles * table_rows >= tm and len(set(tile_kinds[:n_table_tiles])) == 1

    @pl.when(n == 0)
    def _():
        xb_ref[...] = x_ref[...].astype(BF16)

    def body(kinds, fill_tables):
        acc = jnp.dot(xb_ref[...], w_ref[...], preferred_element_type=F32)
        if fill_tables:
            start = pl.multiple_of(jnp.minimum(n * table_rows, tm - table_rows), 8)
            tables = _rope_tables(pos_ref[pl.ds(start, table_rows), :], freq_ref[...])
            for ti, t in enumerate(tables):
                tab_ref[ti, pl.ds(start, table_rows), :] = t
        for ci, kind in enumerate(kinds):
            cols = slice(ci * LANES, (ci + 1) * LANES)
            t = acc[:, cols]
            if kind == EP_ROPE64:
                t = (t * tab_ref[TAB_COS_A] + pltpu.roll(t, 96, axis=1) * tab_ref[TAB_SIN_A_LOW]
                     + pltpu.roll(t, 32, axis=1) * tab_ref[TAB_SIN_A_HIGH])
            elif kind == EP_ROPE128:
                t = t * tab_ref[TAB_COS_B] + pltpu.roll(t, 64, axis=1) * tab_ref[TAB_SIN_B]
            elif kind == EP_SILU:
                t = t * jax.nn.sigmoid(t)
            elif kind == EP_SIGMOID:
                t = jax.nn.sigmoid(t + bias_ref[:, cols])
            out_ref[:, cols] = t.astype(out_ref.dtype)

    branches = [(tile_kinds[0], list(range(n_table_tiles)), True)]
    for i in range(n_table_tiles, len(tile_kinds)):
        same = [br for br in branches if br[0] == tile_kinds[i] and not br[2]]
        if same:
            same[0][1].append(i)
        else:
            branches.append((tile_kinds[i], [i], False))
    for kinds, tiles, fill_tables in branches:
        cond = n == tiles[0]
        for i in tiles[1:]:
            cond = cond | (n == i)
        pl.when(cond)(functools.partial(body, kinds, fill_tables))


def _in_projection(x2d, w, bias, pos, freq, chunk_kinds):
    s_len, k_dim = x2d.shape
    n_cols = w.shape[1]
    n_tiles = n_cols // PROJ_TN
    chunks_per_tile = PROJ_TN // LANES
    tile_kinds = tuple(tuple(chunk_kinds[t * chunks_per_tile:(t + 1) * chunks_per_tile]) for t in range(n_tiles))
    tm = min(PROJ_TM, s_len)
    n_table_tiles = [any(k in (EP_ROPE64, EP_ROPE128) for k in kinds) for kinds in tile_kinds].index(True)
    table_rows = -(-tm // (8 * n_table_tiles)) * 8
    row_spec = lambda width: pl.BlockSpec((tm, width), lambda m, n: (m, 0))
    return pl.pallas_call(
        functools.partial(_proj_kernel, tile_kinds, table_rows),
        grid=(s_len // tm, n_tiles),
        in_specs=[row_spec(k_dim),
                  pl.BlockSpec((k_dim, PROJ_TN), lambda m, n: (0, n)),
                  pl.BlockSpec((1, PROJ_TN), lambda m, n: (0, n)),
                  row_spec(LANES),
                  pl.BlockSpec((1, LANES), lambda m, n: (0, 0))],
        out_specs=pl.BlockSpec((tm, PROJ_TN), lambda m, n: (m, n)),
        out_shape=jax.ShapeDtypeStruct((s_len, n_cols), BF16),
        scratch_shapes=[pltpu.VMEM((tm, k_dim), BF16), pltpu.VMEM((5, tm, LANES), F32)],
        compiler_params=_compiler_params(("arbitrary", "arbitrary")),
        name="in_projection",
    )(x2d, w, bias, pos, freq)


PACK_COLS = 256
PACK_BLOCKS_PER_STEP = PROJ_TN // PACK_COLS


def _packed_layout():
    g = B_GROUP_WIDTH
    assert OFF_VA == OFF_KA + A_KV_WIDTH
    segments = [(OFF_MG_A, D_MODEL, [EP_SIGMOID]), (OFF_MG_B, D_MODEL, [EP_SIGMOID])]
    for gi in range(B_N_GROUPS):
        segments += [(OFF_QB + gi * g, g, [EP_ROPE128]), (OFF_KB + gi * g, g, [EP_ROPE128]),
                     (OFF_VB + gi * g, g, [EP_NONE])]
    segments += [(OFF_GATE_B, g, [EP_SILU]), (OFF_QA, A_WIDTH, [EP_ROPE64]),
                 (OFF_KA, 2 * A_KV_WIDTH, [EP_ROPE64] * (A_KV_WIDTH // LANES) + [EP_NONE] * (A_KV_WIDTH // LANES)),
                 (OFF_GATE_A, A_WIDTH, [EP_SILU])]
    perm, kinds = [], []
    for off, width, seg_kinds in segments:
        assert off % PACK_COLS == 0 and width % PACK_COLS == 0
        perm += [off // PACK_COLS + j for j in range(width // PACK_COLS)]
        kinds += seg_kinds * (width // LANES // len(seg_kinds))
    assert sorted(perm) == list(range(D_IN // PACK_COLS)) and len(kinds) == D_IN // LANES
    return np.asarray(perm, np.int32), kinds


def _pack_kernel(perm_ref, *refs):
    del perm_ref
    out_ref = refs[-1]
    for j, w_ref in enumerate(refs[:-1]):
        out_ref[:, j * PACK_COLS:(j + 1) * PACK_COLS] = w_ref[...].astype(out_ref.dtype)


def _pack_weights(w_in, b_gate):
    perm, kinds = _packed_layout()
    k_dim = w_in.shape[0]
    n_steps = D_IN // PROJ_TN
    src = lambda j: pl.BlockSpec((k_dim, PACK_COLS), lambda i, perm_ref: (0, perm_ref[i * PACK_BLOCKS_PER_STEP + j]))
    w = pl.pallas_call(
        _pack_kernel,
        grid_spec=pltpu.PrefetchScalarGridSpec(
            num_scalar_prefetch=1, grid=(n_steps,),
            in_specs=[src(j) for j in range(PACK_BLOCKS_PER_STEP)],
            out_specs=pl.BlockSpec((k_dim, PROJ_TN), lambda i, perm_ref: (0, i))),
        out_shape=jax.ShapeDtypeStruct((k_dim, D_IN), BF16),
        compiler_params=_compiler_params(("arbitrary",)),
        name="pack_weights",
    )(jnp.asarray(perm), *([w_in] * PACK_BLOCKS_PER_STEP))
    bias = jnp.concatenate([b_gate[0], b_gate[1], jnp.zeros((D_IN - 2 * D_MODEL,), F32)])[None, :]
    return w, kinds, bias


ATT_TOKENS = 512
MIXER_B_TOKENS = 1024
DEINTERLEAVE_MAX_STRIDE = 4


def _band_bias(max_dist, reps):
    q_idx = np.arange(BLOCK)[:, None] + BLOCK
    k_idx = np.arange(2 * BLOCK)[None, :]
    dist = q_idx - k_idx
    band = (dist >= 0) & (dist <= max_dist)
    first = band & (k_idx >= BLOCK)
    both = np.stack([first, band]).astype(bool)
    bias = np.where(both, 0.0, -np.inf).astype(np.float32)
    return jnp.asarray(np.tile(bias, (1, 1, reps)))


def _block_bias(bias_ref, step, b):
    if b == 0:
        return bias_ref[jnp.where(step == 0, 0, 1)]
    return bias_ref[1]


def _swap_lane_halves(t):
    return pltpu.roll(t, LANES // 2, axis=1)


def _mixer_a_kernel(a_ref, sink_ref, bias_ref, out_ref, kbuf_ref, vbuf_ref):
    step = pl.program_id(0)
    tq = a_ref.shape[0]
    n_blocks = tq // BLOCK
    n_pairs = A_HEADS // 2
    pairs_per_group = n_pairs // A_KV_HEADS

    @pl.when(step == 0)
    def _():
        kbuf_ref[0:BLOCK, :] = jnp.zeros((BLOCK, A_KV_WIDTH), BF16)
        vbuf_ref[0:BLOCK, :] = jnp.zeros((BLOCK, A_KV_WIDTH), BF16)

    kbuf_ref[BLOCK:, :] = a_ref[:, A_COL_K:A_COL_K + A_KV_WIDTH]
    vbuf_ref[BLOCK:, :] = a_ref[:, A_COL_V:A_COL_V + A_KV_WIDTH]
    lane2 = lax.broadcasted_iota(jnp.int32, (2 * BLOCK, LANES), 1)
    low2 = lane2 < LANES // 2
    denom_cols = jnp.concatenate([jnp.where(low2, 1.0, 0.0), jnp.where(low2, 0.0, 1.0)], axis=0).astype(BF16)
    low_o = lax.broadcasted_iota(jnp.int32, (BLOCK, LANES), 1) < LANES // 2

    for b in range(n_blocks):
        bias = _block_bias(bias_ref, step, b)
        rows = slice(b * BLOCK, (b + 1) * BLOCK)
        kk = kbuf_ref[b * BLOCK:(b + 2) * BLOCK, :].astype(F32)
        vv = vbuf_ref[b * BLOCK:(b + 2) * BLOCK, :].astype(F32)
        kk_sw = _swap_lane_halves(kk)
        vv_sw = _swap_lane_halves(vv)
        k2, v2 = [], []
        for g in range(A_KV_HEADS):
            if g == 0:
                k_top, k_bot = jnp.where(low2, kk, 0.0), jnp.where(low2, 0.0, kk_sw)
                v_top, v_bot = jnp.where(low2, vv, 0.0), jnp.where(low2, 0.0, vv_sw)
            else:
                k_top, k_bot = jnp.where(low2, kk_sw, 0.0), jnp.where(low2, 0.0, kk)
                v_top, v_bot = jnp.where(low2, vv_sw, 0.0), jnp.where(low2, 0.0, vv)
            k2.append(jnp.concatenate([k_top, k_bot], axis=0).astype(BF16))
            v2.append(jnp.concatenate([jnp.concatenate([v_top, v_bot], axis=0).astype(BF16), denom_cols], axis=1))
        for p in range(n_pairs):
            g = p // pairs_per_group
            cols = slice(p * LANES, (p + 1) * LANES)
            qp = a_ref[rows, cols] * jnp.asarray(A_HEAD_DIM ** -0.5, BF16)
            s = lax.dot_general(qp, k2[g], (((1,), (1,)), ((), ())), preferred_element_type=F32) + bias
            sink0, sink1 = sink_ref[2 * p], sink_ref[2 * p + 1]
            m0 = jnp.maximum(jnp.max(s[:, :2 * BLOCK], axis=1, keepdims=True), sink0)
            m1 = jnp.maximum(jnp.max(s[:, 2 * BLOCK:], axis=1, keepdims=True), sink1)
            m0b = jnp.broadcast_to(m0, (BLOCK, LANES))
            m1b = jnp.broadcast_to(m1, (BLOCK, LANES))
            shifts = (m0b, m0b, m1b, m1b)
            prob = jnp.concatenate([jnp.exp(s[:, j * LANES:(j + 1) * LANES] - shifts[j]) for j in range(4)],
                                   axis=1).astype(BF16)
            o2 = jnp.dot(prob, v2[g], preferred_element_type=F32)
            sink_term = jnp.where(low_o, jnp.exp(sink0 - m0b), jnp.exp(sink1 - m1b))
            o = o2[:, :LANES] / (o2[:, LANES:] + sink_term)
            gate = a_ref[rows, A_COL_GATE + p * LANES:A_COL_GATE + (p + 1) * LANES].astype(F32)
            out_ref[rows, cols] = (o * gate).astype(out_ref.dtype)

    kbuf_ref[0:BLOCK, :] = kbuf_ref[tq:tq + BLOCK, :]
    vbuf_ref[0:BLOCK, :] = vbuf_ref[tq:tq + BLOCK, :]


def _mixer_a(h, sinks):
    s_len = h.shape[0]
    tq = ATT_TOKENS
    bias = _band_bias(A_WINDOW - 1, 2)
    return pl.pallas_call(
        _mixer_a_kernel,
        grid=(s_len // tq,),
        in_specs=[
            pl.BlockSpec((tq, PK_A_WIDTH), lambda i: (i, PK_A // PK_A_WIDTH)),
            pl.BlockSpec(memory_space=pltpu.SMEM),
            pl.BlockSpec(bias.shape, lambda i: (0, 0, 0)),
        ],
        out_specs=pl.BlockSpec((tq, A_WIDTH), lambda i: (i, 0)),
        out_shape=jax.ShapeDtypeStruct((s_len, A_WIDTH), BF16),
        scratch_shapes=[pltpu.VMEM((tq + BLOCK, A_KV_WIDTH), BF16), pltpu.VMEM((tq + BLOCK, A_KV_WIDTH), BF16)],
        compiler_params=_compiler_params(("arbitrary",)),
        name="mixer_a",
    )(h, sinks.astype(F32), bias)


def _mixer_b_kernel(d, nb, q_ref, k_ref, v_ref, bias_ref, o_ref, lse_ref, qsub, ksub, vsub, *stage):
    step = pl.program_id(0)
    n_heads = B_HEADS_PER_GROUP
    span = BLOCK * d

    @pl.when(step == 0)
    def _():
        ksub[:, 0:BLOCK, :] = jnp.zeros((d, BLOCK, B_GROUP_WIDTH), BF16)
        vsub[:, 0:BLOCK, :] = jnp.zeros((d, BLOCK, B_GROUP_WIDTH), BF16)

    if d == 1:
        qsub[0] = q_ref[...]
        ksub[0, BLOCK:, :] = k_ref[...]
        vsub[0, BLOCK:, :] = v_ref[...]
    else:
        slab, mid, ostage, lstage = stage
        f1 = DEINTERLEAVE_MAX_STRIDE if d > DEINTERLEAVE_MAX_STRIDE else 1
        f2 = d // f1
        t_rows = nb * span
        for src, dst, row0 in ((q_ref, qsub, 0), (k_ref, ksub, BLOCK), (v_ref, vsub, BLOCK)):
            for c in range(n_heads):
                cols = slice(c * LANES, (c + 1) * LANES)
                sl = slab.at[c]
                sl[...] = src[:, cols].astype(F32)
                for r1 in range(f1):
                    if f1 > 1:
                        md = mid.at[c, r1]
                        md[...] = sl[pl.ds(r1, t_rows // f1, stride=f1), :]
                    else:
                        md = sl
                    for r2 in range(f2):
                        for b in range(nb):
                            piece = md[pl.ds(b * BLOCK * f2 + r2, BLOCK, stride=f2), :]
                            dst[r1 + f1 * r2, row0 + b * BLOCK:row0 + (b + 1) * BLOCK, cols] = piece.astype(BF16)

    ones = jnp.ones((2 * BLOCK, LANES), BF16)
    lane = lax.broadcasted_iota(jnp.int32, (BLOCK, LANES), 1)
    scale = B_HEAD_DIM ** -0.5
    for r in range(d):
        for b in range(nb):
            bias = _block_bias(bias_ref, step, b)
            rows = slice(b * BLOCK, (b + 1) * BLOCK)
            lse_tile = jnp.zeros((BLOCK, LANES), F32)
            for h in range(n_heads):
                cols = slice(h * B_HEAD_DIM, (h + 1) * B_HEAD_DIM)
                kk = ksub[r, b * BLOCK:(b + 2) * BLOCK, cols]
                v2 = jnp.concatenate([vsub[r, b * BLOCK:(b + 2) * BLOCK, cols], ones], axis=1)
                s = lax.dot_general(qsub[r, rows, cols], kk, (((1,), (1,)), ((), ())),
                                    preferred_element_type=F32) * scale + bias
                m = jnp.max(s, axis=1, keepdims=True)
                mb = jnp.broadcast_to(m, (BLOCK, LANES))
                prob = jnp.concatenate([jnp.exp(s[:, :LANES] - mb), jnp.exp(s[:, LANES:] - mb)],
                                       axis=1).astype(BF16)
                o2 = jnp.dot(prob, v2, preferred_element_type=F32)
                denom = o2[:, LANES:]
                o = o2[:, :LANES] / denom
                lse_h = mb + jnp.log(denom)
                in_head = jnp.logical_and(lane >= h * B_LSE_LANES, lane < (h + 1) * B_LSE_LANES)
                lse_tile = jnp.where(in_head, lse_h, lse_tile)
                if d == 1:
                    o_ref[rows, cols] = o.astype(o_ref.dtype)
                else:
                    ostage[h, pl.ds(b * span + r, BLOCK, stride=d), :] = o
            if d == 1:
                lse_ref[rows, :] = lse_tile
            else:
                lstage[pl.ds(b * span + r, BLOCK, stride=d), :] = lse_tile

    if d > 1:
        for h in range(n_heads):
            o_ref[:, h * B_HEAD_DIM:(h + 1) * B_HEAD_DIM] = ostage[h].astype(o_ref.dtype)
        lse_ref[...] = lstage[...]
    ksub[:, 0:BLOCK, :] = ksub[:, nb * BLOCK:(nb + 1) * BLOCK, :]
    vsub[:, 0:BLOCK, :] = vsub[:, nb * BLOCK:(nb + 1) * BLOCK, :]


def _mixer_b_group(h, gi):
    s_len = h.shape[0]
    window, d = B_PATTERNS[gi]
    assert window // d == BLOCK
    nb = max(1, MIXER_B_TOKENS // (BLOCK * d))
    t_rows = nb * BLOCK * d
    f1 = DEINTERLEAVE_MAX_STRIDE if d > DEINTERLEAVE_MAX_STRIDE else 1
    assert d % f1 == 0 and d // f1 <= DEINTERLEAVE_MAX_STRIDE
    gw = B_GROUP_WIDTH
    col0 = (PK_B + gi * 3 * gw) // gw
    bias = _band_bias(BLOCK, 1)
    blk = lambda c: pl.BlockSpec((t_rows, gw), lambda i: (i, c))
    scratch = [pltpu.VMEM((d, nb * BLOCK, gw), BF16), pltpu.VMEM((d, (nb + 1) * BLOCK, gw), BF16),
               pltpu.VMEM((d, (nb + 1) * BLOCK, gw), BF16)]
    if d > 1:
        mid_rows = t_rows // f1 if f1 > 1 else 8
        scratch += [pltpu.VMEM((B_HEADS_PER_GROUP, t_rows, LANES), F32),
                    pltpu.VMEM((B_HEADS_PER_GROUP, f1, mid_rows, LANES), F32),
                    pltpu.VMEM((B_HEADS_PER_GROUP, t_rows, LANES), F32), pltpu.VMEM((t_rows, LANES), F32)]
    return pl.pallas_call(
        functools.partial(_mixer_b_kernel, d, nb),
        grid=(s_len // t_rows,),
        in_specs=[blk(col0), blk(col0 + 1), blk(col0 + 2), pl.BlockSpec(bias.shape, lambda i: (0, 0, 0))],
        out_specs=[pl.BlockSpec((t_rows, gw), lambda i: (i, 0)), pl.BlockSpec((t_rows, LANES), lambda i: (i, 0))],
        out_shape=[jax.ShapeDtypeStruct((s_len, gw), BF16), jax.ShapeDtypeStruct((s_len, LANES), F32)],
        scratch_shapes=scratch,
        compiler_params=_compiler_params(("arbitrary",)),
        name=f"mixer_b_d{d}",
    )(h, h, h, bias)


TAIL_TM = 256


def _tail_kernel(x_ref, ya_ref, o0_ref, o1_ref, o2_ref, l0_ref, l1_ref, l2_ref, gate_b_ref, sig_a_ref, sig_b_ref,
                 w_pa_ref, w_pb_ref, w_out_ref, ln_g_ref, ln_b_ref, out_ref):
    lses = [l0_ref[...], l1_ref[...], l2_ref[...]]
    m = jnp.maximum(jnp.maximum(lses[0], lses[1]), lses[2])
    es = [jnp.exp(l - m) for l in lses]
    inv = 1.0 / (es[0] + es[1] + es[2])
    wts = [e * inv for e in es]
    outs = [o0_ref, o1_ref, o2_ref]
    yb_cols = []
    for h in range(B_HEADS_PER_GROUP):
        cols = slice(h * B_HEAD_DIM, (h + 1) * B_HEAD_DIM)
        acc = None
        for gi in range(B_N_GROUPS):
            w_h = jnp.broadcast_to(wts[gi][:, h * B_LSE_LANES:h * B_LSE_LANES + 1], (wts[gi].shape[0], B_HEAD_DIM))
            term = w_h * outs[gi][:, cols].astype(F32)
            acc = term if acc is None else acc + term
        yb_cols.append((acc * gate_b_ref[:, cols].astype(F32)).astype(BF16))
    yb = jnp.concatenate(yb_cols, axis=1)
    y_a = jnp.dot(ya_ref[...], w_pa_ref[...], preferred_element_type=F32)
    y_b = jnp.dot(yb, w_pb_ref[...], preferred_element_type=F32)
    merged = sig_a_ref[...].astype(F32) * y_a + sig_b_ref[...].astype(F32) * y_b
    sub = jnp.dot(merged.astype(BF16), w_out_ref[...], preferred_element_type=F32)
    z = DN_ALPHA * x_ref[...] + sub
    mu = jnp.mean(z, axis=-1, keepdims=True)
    zc = z - mu
    var = jnp.mean(zc * zc, axis=-1, keepdims=True)
    out_ref[...] = (zc * lax.rsqrt(var + LN_EPS) * ln_g_ref[...] + ln_b_ref[...]).astype(out_ref.dtype)


def _tail(x2d, ya, o_groups, lse_groups, h, w_pa, w_pb, w_out, ln_g, ln_b):
    s_len = x2d.shape[0]
    tm = TAIL_TM
    row = lambda width, cblk=0: pl.BlockSpec((tm, width), lambda i: (i, cblk))
    full = lambda a: pl.BlockSpec(a.shape, lambda i: (0,) * a.ndim)
    gw = B_GROUP_WIDTH
    return pl.pallas_call(
        _tail_kernel,
        grid=(s_len // tm,),
        in_specs=[row(D_MODEL), row(A_WIDTH), row(gw), row(gw), row(gw), row(LANES), row(LANES), row(LANES),
                  row(gw, PK_GATE_B // gw), row(D_MODEL, PK_MG_A // D_MODEL), row(D_MODEL, PK_MG_B // D_MODEL),
                  full(w_pa), full(w_pb), full(w_out), full(ln_g), full(ln_b)],
        out_specs=row(D_MODEL),
        out_shape=jax.ShapeDtypeStruct((s_len, D_MODEL), x2d.dtype),
        compiler_params=_compiler_params(("arbitrary",)),
        name="tail",
    )(x2d, ya, *o_groups, *lse_groups, h, h, h, w_pa, w_pb, w_out, ln_g, ln_b)


def _hybrid_layer(x, positions, w_in, b_gate, sinks, w_pa, w_pb, w_out, ln_g, ln_b):
    bn, s_len, d_model = x.shape
    assert bn == 1 and d_model == D_MODEL
    assert s_len % (BLOCK * B_PATTERNS[-1][1]) == 0 and s_len % PROJ_TM == 0
    x2d = x.reshape(s_len, d_model)
    pos, freq = _rope_inputs(positions)
    w, kinds, bias = _pack_weights(w_in, b_gate)
    h = _in_projection(x2d, w, bias, pos, freq, kinds)
    o_groups, lse_groups = [], []
    for gi in range(B_N_GROUPS):
        o, lse = _mixer_b_group(h, gi)
        o_groups.append(o)
        lse_groups.append(lse)
    ya = _mixer_a(h, sinks)
    out = _tail(x2d, ya, o_groups, lse_groups, h, w_pa.astype(BF16), w_pb.astype(BF16), w_out.astype(BF16),
                ln_g.reshape(1, d_model).astype(F32), ln_b.reshape(1, d_model).astype(F32))
    return out.reshape(bn, s_len, d_model)


def kernel(x, positions, w_in, b_gate, sinks, w_pa, w_pb, w_out, ln_g, ln_b):
    for layer in range(w_in.shape[0]):
        x = _hybrid_layer(x, positions, w_in[layer], b_gate[layer], sinks[layer], w_pa[layer], w_pb[layer],
                          w_out[layer], ln_g[layer], ln_b[layer])
    return x
```

```python
import functools

import jax
import jax.numpy as jnp
import numpy as np
from jax import lax
from jax.experimental import pallas as pl
from jax.experimental.pallas import tpu as pltpu

F32 = jnp.float32
BF16 = jnp.bfloat16

D_MODEL = 2048
ROPE_THETA = 10000.0
LN_EPS = 1e-5
BLOCK = 128
LANES = 128
A_HEADS = 16
A_KV_HEADS = 2
A_HEAD_DIM = 64
A_WINDOW = 128
A_WIDTH = A_HEADS * A_HEAD_DIM
A_KV_WIDTH = A_KV_HEADS * A_HEAD_DIM
B_PATTERNS = ((128, 1), (512, 4), (2048, 16))
B_HEADS_PER_GROUP = 4
B_HEAD_DIM = 128
B_GROUP_WIDTH = B_HEADS_PER_GROUP * B_HEAD_DIM
B_N_GROUPS = len(B_PATTERNS)
B_QKV_WIDTH = B_N_GROUPS * B_GROUP_WIDTH
B_LSE_LANES = LANES // B_HEADS_PER_GROUP
DEPTH = 1
DN_ALPHA = float((2 * DEPTH) ** 0.25)

IN_SIZES = (A_WIDTH, A_KV_WIDTH, A_KV_WIDTH, A_WIDTH, B_QKV_WIDTH, B_QKV_WIDTH, B_QKV_WIDTH,
            B_GROUP_WIDTH, D_MODEL, D_MODEL)
IN_OFFSETS = tuple(int(o) for o in np.cumsum((0,) + IN_SIZES[:-1]))
(OFF_QA, OFF_KA, OFF_VA, OFF_GATE_A, OFF_QB, OFF_KB, OFF_VB, OFF_GATE_B, OFF_MG_A, OFF_MG_B) = IN_OFFSETS
D_IN = sum(IN_SIZES)

VMEM_LIMIT_BYTES = 56 * 1024 * 1024

EP_NONE, EP_ROPE64, EP_ROPE128, EP_SILU, EP_SIGMOID = range(5)

PROJ_TM = 1024
PROJ_TN = 1280

PK_MG_A = 0
PK_MG_B = PK_MG_A + D_MODEL
PK_B = PK_MG_B + D_MODEL
PK_GATE_B = PK_B + 3 * B_QKV_WIDTH
PK_A = PK_GATE_B + B_GROUP_WIDTH
PK_A_WIDTH = 2 * A_WIDTH + 2 * A_KV_WIDTH
A_COL_K = A_WIDTH
A_COL_V = A_COL_K + A_KV_WIDTH
A_COL_GATE = A_COL_V + A_KV_WIDTH
assert PK_A + PK_A_WIDTH == D_IN and PK_A % PK_A_WIDTH == 0 and PK_B % B_GROUP_WIDTH == 0
assert PK_GATE_B % B_GROUP_WIDTH == 0 and D_IN % PROJ_TN == 0


def _compiler_params(semantics):
    return pltpu.CompilerParams(dimension_semantics=semantics, vmem_limit_bytes=VMEM_LIMIT_BYTES)


TAB_COS_A, TAB_SIN_A_LOW, TAB_SIN_A_HIGH, TAB_COS_B, TAB_SIN_B = range(5)


def _rope_tables(pos, freq):
    ang = pos * freq
    c = jnp.cos(ang)
    s = jnp.sin(ang)
    lane = lax.broadcasted_iota(jnp.int32, c.shape, 1)

    def expand(t):
        r32 = pltpu.roll(t, 32, axis=1)
        r64 = pltpu.roll(t, 64, axis=1)
        r96 = pltpu.roll(t, 96, axis=1)
        t_a = jnp.where(lane < 32, t, jnp.where(lane < 64, r32, jnp.where(lane < 96, r64, r96)))
        t_b = jnp.where(lane < 64, r96, r32)
        return t_a, t_b

    c_a, c_b = expand(c)
    s_a, s_b = expand(s)
    first_half_a = jnp.bitwise_and(lane, A_HEAD_DIM - 1) < A_HEAD_DIM // 2
    return (c_a, jnp.where(first_half_a, -s_a, 0.0), jnp.where(first_half_a, 0.0, s_a),
            c_b, jnp.where(lane < 64, -s_b, s_b))


def _rope_inputs(positions):
    s_len = positions.shape[1]
    half_a, half_b = A_HEAD_DIM // 2, B_HEAD_DIM // 2
    inv_a = ROPE_THETA ** (-jnp.arange(half_a, dtype=F32) / half_a)
    inv_b = ROPE_THETA ** (-jnp.arange(half_b, dtype=F32) / half_b)
    freq = jnp.concatenate([inv_a, inv_b, jnp.zeros((LANES - half_a - half_b,), F32)])[None, :]
    pos = jnp.broadcast_to(positions.reshape(s_len, 1).astype(F32), (s_len, LANES))
    return pos, freq


def _proj_kernel(tile_kinds, table_rows, x_ref, w_ref, bias_ref, pos_ref, freq_ref, out_ref, xb_ref, tab_ref):
    n = pl.program_id(1)
    tm = x_ref.shape[0]
    has_rope = [any(k in (EP_ROPE64, EP_ROPE128) for k in kinds) for kinds in tile_kinds]
    n_table_tiles = has_rope.index(True)
    assert n_table_tiles * table_rows >= tm and len(set(tile_kinds[:n_table_tiles])) == 1

    @pl.when(n == 0)
    def _():
        xb_ref[...] = x_ref[...].astype(BF16)

    def body(kinds, fill_tables):
        acc = jnp.dot(xb_ref[...], w_ref[...], preferred_element_type=F32)
        if fill_tables:
            start = pl.multiple_of(jnp.minimum(n * table_rows, tm - table_rows), 8)
            tables = _rope_tables(pos_ref[pl.ds(start, table_rows), :], freq_ref[...])
            for ti, t in enumerate(tables):
                tab_ref[ti, pl.ds(start, table_rows), :] = t
        for ci, kind in enumerate(kinds):
            cols = slice(ci * LANES, (ci + 1) * LANES)
            t = acc[:, cols]
            if kind == EP_ROPE64:
                t = (t * tab_ref[TAB_COS_A] + pltpu.roll(t, 96, axis=1) * tab_ref[TAB_SIN_A_LOW]
                     + pltpu.roll(t, 32, axis=1) * tab_ref[TAB_SIN_A_HIGH])
            elif kind == EP_ROPE128:
                t = t * tab_ref[TAB_COS_B] + pltpu.roll(t, 64, axis=1) * tab_ref[TAB_SIN_B]
            elif kind == EP_SILU:
                t = t * jax.nn.sigmoid(t)
            elif kind == EP_SIGMOID:
                t = jax.nn.sigmoid(t + bias_ref[:, cols])
            out_ref[:, cols] = t.astype(out_ref.dtype)

    branches = [(tile_kinds[0], list(range(n_table_tiles)), True)]
    for i in range(n_table_tiles, len(tile_kinds)):
        same = [br for br in branches if br[0] == tile_kinds[i] and not br[2]]
        if same:
            same[0][1].append(i)
        else:
            branches.append((tile_kinds[i], [i], False))
    for kinds, tiles, fill_tables in branches:
        cond = n == tiles[0]
        for i in tiles[1:]:
            cond = cond | (n == i)
        pl.when(cond)(functools.partial(body, kinds, fill_tables))


def _in_projection(x2d, w, bias, pos, freq, chunk_kinds):
    s_len, k_dim = x2d.shape
    n_cols = w.shape[1]
    n_tiles = n_cols // PROJ_TN
    chunks_per_tile = PROJ_TN // LANES
    tile_kinds = tuple(tuple(chunk_kinds[t * chunks_per_tile:(t + 1) * chunks_per_tile]) for t in range(n_tiles))
    tm = min(PROJ_TM, s_len)
    n_table_tiles = [any(k in (EP_ROPE64, EP_ROPE128) for k in kinds) for kinds in tile_kinds].index(True)
    table_rows = -(-tm // (8 * n_table_tiles)) * 8
    row_spec = lambda width: pl.BlockSpec((tm, width), lambda m, n: (m, 0))
    return pl.pallas_call(
        functools.partial(_proj_kernel, tile_kinds, table_rows),
        grid=(s_len // tm, n_tiles),
        in_specs=[row_spec(k_dim),
                  pl.BlockSpec((k_dim, PROJ_TN), lambda m, n: (0, n)),
                  pl.BlockSpec((1, PROJ_TN), lambda m, n: (0, n)),
                  row_spec(LANES),
                  pl.BlockSpec((1, LANES), lambda m, n: (0, 0))],
        out_specs=pl.BlockSpec((tm, PROJ_TN), lambda m, n: (m, n)),
        out_shape=jax.ShapeDtypeStruct((s_len, n_cols), BF16),
        scratch_shapes=[pltpu.VMEM((tm, k_dim), BF16), pltpu.VMEM((5, tm, LANES), F32)],
        compiler_params=_compiler_params(("arbitrary", "arbitrary")),
        name="in_projection",
    )(x2d, w, bias, pos, freq)


PACK_COLS = 256
PACK_BLOCKS_PER_STEP = PROJ_TN // PACK_COLS


def _packed_layout():
    g = B_GROUP_WIDTH
    assert OFF_VA == OFF_KA + A_KV_WIDTH
    segments = [(OFF_MG_A, D_MODEL, [EP_SIGMOID]), (OFF_MG_B, D_MODEL, [EP_SIGMOID])]
    for gi in range(B_N_GROUPS):
        segments += [(OFF_QB + gi * g, g, [EP_ROPE128]), (OFF_KB + gi * g, g, [EP_ROPE128]),
                     (OFF_VB + gi * g, g, [EP_NONE])]
    segments += [(OFF_GATE_B, g, [EP_SILU]), (OFF_QA, A_WIDTH, [EP_ROPE64]),
                 (OFF_KA, 2 * A_KV_WIDTH, [EP_ROPE64] * (A_KV_WIDTH // LANES) + [EP_NONE] * (A_KV_WIDTH // LANES)),
                 (OFF_GATE_A, A_WIDTH, [EP_SILU])]
    perm, kinds = [], []
    for off, width, seg_kinds in segments:
        assert off % PACK_COLS == 0 and width % PACK_COLS == 0
        perm += [off // PACK_COLS + j for j in range(width // PACK_COLS)]
        kinds += seg_kinds * (width // LANES // len(seg_kinds))
    assert sorted(perm) == list(range(D_IN // PACK_COLS)) and len(kinds) == D_IN // LANES
    return np.asarray(perm, np.int32), kinds


def _pack_kernel(perm_ref, *refs):
    del perm_ref
    out_ref = refs[-1]
    for j, w_ref in enumerate(refs[:-1]):
        out_ref[:, j * PACK_COLS:(j + 1) * PACK_COLS] = w_ref[...].astype(out_ref.dtype)


def _pack_weights(w_in, b_gate):
    perm, kinds = _packed_layout()
    k_dim = w_in.shape[0]
    n_steps = D_IN // PROJ_TN
    src = lambda j: pl.BlockSpec((k_dim, PACK_COLS), lambda i, perm_ref: (0, perm_ref[i * PACK_BLOCKS_PER_STEP + j]))
    w = pl.pallas_call(
        _pack_kernel,
        grid_spec=pltpu.PrefetchScalarGridSpec(
            num_scalar_prefetch=1, grid=(n_steps,),
            in_specs=[src(j) for j in range(PACK_BLOCKS_PER_STEP)],
            out_specs=pl.BlockSpec((k_dim, PROJ_TN), lambda i, perm_ref: (0, i))),
        out_shape=jax.ShapeDtypeStruct((k_dim, D_IN), BF16),
        compiler_params=_compiler_params(("arbitrary",)),
        name="pack_weights",
    )(jnp.asarray(perm), *([w_in] * PACK_BLOCKS_PER_STEP))
    bias = jnp.concatenate([b_gate[0], b_gate[1], jnp.zeros((D_IN - 2 * D_MODEL,), F32)])[None, :]
    return w, kinds, bias


ATT_TOKENS = 512
MIXER_B_TOKENS = 1024
DEINTERLEAVE_MAX_STRIDE = 4


def _band_bias(max_dist, reps):
    q_idx = np.arange(BLOCK)[:, None] + BLOCK
    k_idx = np.arange(2 * BLOCK)[None, :]
    dist = q_idx - k_idx
    band = (dist >= 0) & (dist <= max_dist)
    first = band & (k_idx >= BLOCK)
    both = np.stack([first, band]).astype(bool)
    bias = np.where(both, 0.0, -np.inf).astype(np.float32)
    return jnp.asarray(np.tile(bias, (1, 1, reps)))


def _block_bias(bias_ref, step, b):
    if b == 0:
        return bias_ref[jnp.where(step == 0, 0, 1)]
    return bias_ref[1]


def _swap_lane_halves(t):
    return pltpu.roll(t, LANES // 2, axis=1)


def _mixer_a_kernel(a_ref, sink_ref, bias_ref, out_ref, kbuf_ref, vbuf_ref):
    step = pl.program_id(0)
    tq = a_ref.shape[0]
    n_blocks = tq // BLOCK
    n_pairs = A_HEADS // 2
    pairs_per_group = n_pairs // A_KV_HEADS

    @pl.when(step == 0)
    def _():
        kbuf_ref[0:BLOCK, :] = jnp.zeros((BLOCK, A_KV_WIDTH), BF16)
        vbuf_ref[0:BLOCK, :] = jnp.zeros((BLOCK, A_KV_WIDTH), BF16)

    kbuf_ref[BLOCK:, :] = a_ref[:, A_COL_K:A_COL_K + A_KV_WIDTH]
    vbuf_ref[BLOCK:, :] = a_ref[:, A_COL_V:A_COL_V + A_KV_WIDTH]
    lane2 = lax.broadcasted_iota(jnp.int32, (2 * BLOCK, LANES), 1)
    low2 = lane2 < LANES // 2
    denom_cols = jnp.concatenate([jnp.where(low2, 1.0, 0.0), jnp.where(low2, 0.0, 1.0)], axis=0).astype(BF16)
    low_o = lax.broadcasted_iota(jnp.int32, (BLOCK, LANES), 1) < LANES // 2

    for b in range(n_blocks):
        bias = _block_bias(bias_ref, step, b)
        rows = slice(b * BLOCK, (b + 1) * BLOCK)
        kk = kbuf_ref[b * BLOCK:(b + 2) * BLOCK, :].astype(F32)
        vv = vbuf_ref[b * BLOCK:(b + 2) * BLOCK, :].astype(F32)
        kk_sw = _swap_lane_halves(kk)
        vv_sw = _swap_lane_halves(vv)
        k2, v2 = [], []
        for g in range(A_KV_HEADS):
            if g == 0:
                k_top, k_bot = jnp.where(low2, kk, 0.0), jnp.where(low2, 0.0, kk_sw)
                v_top, v_bot = jnp.where(low2, vv, 0.0), jnp.where(low2, 0.0, vv_sw)
            else:
                k_top, k_bot = jnp.where(low2, kk_sw, 0.0), jnp.where(low2, 0.0, kk)
                v_top, v_bot = jnp.where(low2, vv_sw, 0.0), jnp.where(low2, 0.0, vv)
            k2.append(jnp.concatenate([k_top, k_bot], axis=0).astype(BF16))
            v2.append(jnp.concatenate([jnp.concatenate([v_top, v_bot], axis=0).astype(BF16), denom_cols], axis=1))
        for p in range(n_pairs):
            g = p // pairs_per_group
            cols = slice(p * LANES, (p + 1) * LANES)
            qp = a_ref[rows, cols] * jnp.asarray(A_HEAD_DIM ** -0.5, BF16)
            s = lax.dot_general(qp, k2[g], (((1,), (1,)), ((), ())), preferred_element_type=F32) + bias
            sink0, sink1 = sink_ref[2 * p], sink_ref[2 * p + 1]
            m0 = jnp.maximum(jnp.max(s[:, :2 * BLOCK], axis=1, keepdims=True), sink0)
            m1 = jnp.maximum(jnp.max(s[:, 2 * BLOCK:], axis=1, keepdims=True), sink1)
            m0b = jnp.broadcast_to(m0, (BLOCK, LANES))
            m1b = jnp.broadcast_to(m1, (BLOCK, LANES))
            shifts = (m0b, m0b, m1b, m1b)
            prob = jnp.concatenate([jnp.exp(s[:, j * LANES:(j + 1) * LANES] - shifts[j]) for j in range(4)],
                                   axis=1).astype(BF16)
            o2 = jnp.dot(prob, v2[g], preferred_element_type=F32)
            sink_term = jnp.where(low_o, jnp.exp(sink0 - m0b), jnp.exp(sink1 - m1b))
            o = o2[:, :LANES] / (o2[:, LANES:] + sink_term)
            gate = a_ref[rows, A_COL_GATE + p * LANES:A_COL_GATE + (p + 1) * LANES].astype(F32)
            out_ref[rows, cols] = (o * gate).astype(out_ref.dtype)

    kbuf_ref[0:BLOCK, :] = kbuf_ref[tq:tq + BLOCK, :]
    vbuf_ref[0:BLOCK, :] = vbuf_ref[tq:tq + BLOCK, :]


def _mixer_a(h, sinks):
    s_len = h.shape[0]
    tq = ATT_TOKENS
    bias = _band_bias(A_WINDOW - 1, 2)
    return pl.pallas_call(
        _mixer_a_kernel,
        grid=(s_len // tq,),
        in_specs=[
            pl.BlockSpec((tq, PK_A_WIDTH), lambda i: (i, PK_A // PK_A_WIDTH)),
            pl.BlockSpec(memory_space=pltpu.SMEM),
            pl.BlockSpec(bias.shape, lambda i: (0, 0, 0)),
        ],
        out_specs=pl.BlockSpec((tq, A_WIDTH), lambda i: (i, 0)),
        out_shape=jax.ShapeDtypeStruct((s_len, A_WIDTH), BF16),
        scratch_shapes=[pltpu.VMEM((tq + BLOCK, A_KV_WIDTH), BF16), pltpu.VMEM((tq + BLOCK, A_KV_WIDTH), BF16)],
        compiler_params=_compiler_params(("arbitrary",)),
        name="mixer_a",
    )(h, sinks.astype(F32), bias)


def _mixer_b_kernel(d, nb, q_ref, k_ref, v_ref, bias_ref, o_ref, lse_ref, qsub, ksub, vsub, *stage):
    step = pl.program_id(0)
    n_heads = B_HEADS_PER_GROUP
    span = BLOCK * d

    @pl.when(step == 0)
    def _():
        ksub[:, 0:BLOCK, :] = jnp.zeros((d, BLOCK, B_GROUP_WIDTH), BF16)
        vsub[:, 0:BLOCK, :] = jnp.zeros((d, BLOCK, B_GROUP_WIDTH), BF16)

    if d == 1:
        qsub[0] = q_ref[...]
        ksub[0, BLOCK:, :] = k_ref[...]
        vsub[0, BLOCK:, :] = v_ref[...]
    else:
        slab, mid, ostage, lstage = stage
        f1 = DEINTERLEAVE_MAX_STRIDE if d > DEINTERLEAVE_MAX_STRIDE else 1
        f2 = d // f1
        t_rows = nb * span
        for src, dst, row0 in ((q_ref, qsub, 0), (k_ref, ksub, BLOCK), (v_ref, vsub, BLOCK)):
            for c in range(n_heads):
                cols = slice(c * LANES, (c + 1) * LANES)
                sl = slab.at[c]
                sl[...] = src[:, cols].astype(F32)
                for r1 in range(f1):
                    if f1 > 1:
                        md = mid.at[c, r1]
                        md[...] = sl[pl.ds(r1, t_rows // f1, stride=f1), :]
                    else:
                        md = sl
                    for r2 in range(f2):
                        for b in range(nb):
                            piece = md[pl.ds(b * BLOCK * f2 + r2, BLOCK, stride=f2), :]
                            dst[r1 + f1 * r2, row0 + b * BLOCK:row0 + (b + 1) * BLOCK, cols] = piece.astype(BF16)

    ones = jnp.ones((2 * BLOCK, LANES), BF16)
    lane = lax.broadcasted_iota(jnp.int32, (BLOCK, LANES), 1)
    scale = B_HEAD_DIM ** -0.5
    for r in range(d):
        for b in range(nb):
            bias = _block_bias(bias_ref, step, b)
            rows = slice(b * BLOCK, (b + 1) * BLOCK)
            lse_tile = jnp.zeros((BLOCK, LANES), F32)
            for h in range(n_heads):
                cols = slice(h * B_HEAD_DIM, (h + 1) * B_HEAD_DIM)
                kk = ksub[r, b * BLOCK:(b + 2) * BLOCK, cols]
                v2 = jnp.concatenate([vsub[r, b * BLOCK:(b + 2) * BLOCK, cols], ones], axis=1)
                s = lax.dot_general(qsub[r, rows, cols], kk, (((1,), (1,)), ((), ())),
                                    preferred_element_type=F32) * scale + bias
                m = jnp.max(s, axis=1, keepdims=True)
                mb = jnp.broadcast_to(m, (BLOCK, LANES))
                prob = jnp.concatenate([jnp.exp(s[:, :LANES] - mb), jnp.exp(s[:, LANES:] - mb)],
                                       axis=1).astype(BF16)
                o2 = jnp.dot(prob, v2, preferred_element_type=F32)
                denom = o2[:, LANES:]
                o = o2[:, :LANES] / denom
                lse_h = mb + jnp.log(denom)
                in_head = jnp.logical_and(lane >= h * B_LSE_LANES, lane < (h + 1) * B_LSE_LANES)
                lse_tile = jnp.where(in_head, lse_h, lse_tile)
                if d == 1:
                    o_ref[rows, cols] = o.astype(o_ref.dtype)
                else:
                    ostage[h, pl.ds(b * span + r, BLOCK, stride=d), :] = o
            if d == 1:
                lse_ref[rows, :] = lse_tile
            else:
                lstage[pl.ds(b * span + r, BLOCK, stride=d), :] = lse_tile

    if d > 1:
        for h in range(n_heads):
            o_ref[:, h * B_HEAD_DIM:(h + 1) * B_HEAD_DIM] = ostage[h].astype(o_ref.dtype)
        lse_ref[...] = lstage[...]
    ksub[:, 0:BLOCK, :] = ksub[:, nb * BLOCK:(nb + 1) * BLOCK, :]
    vsub[:, 0:BLOCK, :] = vsub[:, nb * BLOCK:(nb + 1) * BLOCK, :]


def _mixer_b_group(h, gi):
    s_len = h.shape[0]
    window, d = B_PATTERNS[gi]
    assert window // d == BLOCK
    nb = max(1, MIXER_B_TOKENS // (BLOCK * d))
    t_rows = nb * BLOCK * d
    f1 = DEINTERLEAVE_MAX_STRIDE if d > DEINTERLEAVE_MAX_STRIDE else 1
    assert d % f1 == 0 and d // f1 <= DEINTERLEAVE_MAX_STRIDE
    gw = B_GROUP_WIDTH
    col0 = (PK_B + gi * 3 * gw) // gw
    bias = _band_bias(BLOCK, 1)
    blk = lambda c: pl.BlockSpec((t_rows, gw), lambda i: (i, c))
    scratch = [pltpu.VMEM((d, nb * BLOCK, gw), BF16), pltpu.VMEM((d, (nb + 1) * BLOCK, gw), BF16),
               pltpu.VMEM((d, (nb + 1) * BLOCK, gw), BF16)]
    if d > 1:
        mid_rows = t_rows // f1 if f1 > 1 else 8
        scratch += [pltpu.VMEM((B_HEADS_PER_GROUP, t_rows, LANES), F32),
                    pltpu.VMEM((B_HEADS_PER_GROUP, f1, mid_rows, LANES), F32),
                    pltpu.VMEM((B_HEADS_PER_GROUP, t_rows, LANES), F32), pltpu.VMEM((t_rows, LANES), F32)]
    return pl.pallas_call(
        functools.partial(_mixer_b_kernel, d, nb),
        grid=(s_len // t_rows,),
        in_specs=[blk(col0), blk(col0 + 1), blk(col0 + 2), pl.BlockSpec(bias.shape, lambda i: (0, 0, 0))],
        out_specs=[pl.BlockSpec((t_rows, gw), lambda i: (i, 0)), pl.BlockSpec((t_rows, LANES), lambda i: (i, 0))],
        out_shape=[jax.ShapeDtypeStruct((s_len, gw), BF16), jax.ShapeDtypeStruct((s_len, LANES), F32)],
        scratch_shapes=scratch,
        compiler_params=_compiler_params(("arbitrary",)),
        name=f"mixer_b_d{d}",
    )(h, h, h, bias)


TAIL_TM = 512
TAIL_SUB = 512


def _tail_kernel(x_ref, ya_ref, o0_ref, o1_ref, o2_ref, l0_ref, l1_ref, l2_ref, gate_b_ref, sig_a_ref, sig_b_ref,
                 w_pa_ref, w_pb_ref, w_out_ref, ln_g_ref, ln_b_ref, out_ref):
    outs = [o0_ref, o1_ref, o2_ref]
    for si in range(x_ref.shape[0] // TAIL_SUB):
        rows = slice(si * TAIL_SUB, (si + 1) * TAIL_SUB)
        lses = [l0_ref[rows, :], l1_ref[rows, :], l2_ref[rows, :]]
        m = jnp.maximum(jnp.maximum(lses[0], lses[1]), lses[2])
        es = [jnp.exp(l - m) for l in lses]
        inv = 1.0 / (es[0] + es[1] + es[2])
        wts = [e * inv for e in es]
        yb_cols = []
        for h in range(B_HEADS_PER_GROUP):
            cols = slice(h * B_HEAD_DIM, (h + 1) * B_HEAD_DIM)
            acc = None
            for gi in range(B_N_GROUPS):
                w_h = jnp.broadcast_to(wts[gi][:, h * B_LSE_LANES:h * B_LSE_LANES + 1], (TAIL_SUB, B_HEAD_DIM))
                term = w_h * outs[gi][rows, cols].astype(F32)
                acc = term if acc is None else acc + term
            yb_cols.append((acc * gate_b_ref[rows, cols].astype(F32)).astype(BF16))
        yb = jnp.concatenate(yb_cols, axis=1)
        y_a = jnp.dot(ya_ref[rows, :], w_pa_ref[...], preferred_element_type=F32)
        y_b = jnp.dot(yb, w_pb_ref[...], preferred_element_type=F32)
        merged = sig_a_ref[rows, :].astype(F32) * y_a + sig_b_ref[rows, :].astype(F32) * y_b
        sub = jnp.dot(merged.astype(BF16), w_out_ref[...], preferred_element_type=F32)
        z = DN_ALPHA * x_ref[rows, :] + sub
        inv_d = 1.0 / z.shape[-1]
        mu = jnp.sum(z, axis=-1, keepdims=True) * inv_d
        var = jnp.sum(z * z, axis=-1, keepdims=True) * inv_d - mu * mu
        out_ref[rows, :] = ((z - mu) * lax.rsqrt(var + LN_EPS) * ln_g_ref[...] + ln_b_ref[...]).astype(out_ref.dtype)


def _tail(x2d, ya, o_groups, lse_groups, h, w_pa, w_pb, w_out, ln_g, ln_b):
    s_len = x2d.shape[0]
    tm = TAIL_TM
    row = lambda width, cblk=0: pl.BlockSpec((tm, width), lambda i: (i, cblk))
    full = lambda a: pl.BlockSpec(a.shape, lambda i: (0,) * a.ndim, pipeline_mode=pl.Buffered(1))
    gw = B_GROUP_WIDTH
    return pl.pallas_call(
        _tail_kernel,
        grid=(s_len // tm,),
        in_specs=[row(D_MODEL), row(A_WIDTH), row(gw), row(gw), row(gw), row(LANES), row(LANES), row(LANES),
                  row(gw, PK_GATE_B // gw), row(D_MODEL, PK_MG_A // D_MODEL), row(D_MODEL, PK_MG_B // D_MODEL),
                  full(w_pa), full(w_pb), full(w_out), full(ln_g), full(ln_b)],
        out_specs=row(D_MODEL),
        out_shape=jax.ShapeDtypeStruct((s_len, D_MODEL), x2d.dtype),
        compiler_params=_compiler_params(("arbitrary",)),
        name="tail",
    )(x2d, ya, *o_groups, *lse_groups, h, h, h, w_pa, w_pb, w_out, ln_g, ln_b)


def _hybrid_layer(x, positions, w_in, b_gate, sinks, w_pa, w_pb, w_out, ln_g, ln_b):
    bn, s_len, d_model = x.shape
    assert bn == 1 and d_model == D_MODEL
    assert s_len % (BLOCK * B_PATTERNS[-1][1]) == 0 and s_len % PROJ_TM == 0
    x2d = x.reshape(s_len, d_model)
    pos, freq = _rope_inputs(positions)
    w, kinds, bias = _pack_weights(w_in, b_gate)
    h = _in_projection(x2d, w, bias, pos, freq, kinds)
    o_groups, lse_groups = [], []
    for gi in range(B_N_GROUPS):
        o, lse = _mixer_b_group(h, gi)
        o_groups.append(o)
        lse_groups.append(lse)
    ya = _mixer_a(h, sinks)
    out = _tail(x2d, ya, o_groups, lse_groups, h, w_pa.astype(BF16), w_pb.astype(BF16), w_out.astype(BF16),
                ln_g.reshape(1, d_model).astype(F32), ln_b.reshape(1, d_model).astype(F32))
    return out.reshape(bn, s_len, d_model)


def kernel(x, positions, w_in, b_gate, sinks, w_pa, w_pb, w_out, ln_g, ln_b):
    for layer in range(w_in.shape[0]):
        x = _hybrid_layer(x, positions, w_in[layer], b_gate[layer], sinks[layer], w_pa[layer], w_pb[layer],
                          w_out[layer], ln_g[layer], ln_b[layer])
    return x
```

```python
import functools

import jax
import jax.numpy as jnp
import numpy as np
from jax import lax
from jax.experimental import pallas as pl
from jax.experimental.pallas import tpu as pltpu

F32 = jnp.float32
BF16 = jnp.bfloat16

D_MODEL = 2048
ROPE_THETA = 10000.0
LN_EPS = 1e-5
BLOCK = 128
LANES = 128
A_HEADS = 16
A_KV_HEADS = 2
A_HEAD_DIM = 64
A_WINDOW = 128
A_WIDTH = A_HEADS * A_HEAD_DIM
A_KV_WIDTH = A_KV_HEADS * A_HEAD_DIM
B_PATTERNS = ((128, 1), (512, 4), (2048, 16))
B_HEADS_PER_GROUP = 4
B_HEAD_DIM = 128
B_GROUP_WIDTH = B_HEADS_PER_GROUP * B_HEAD_DIM
B_N_GROUPS = len(B_PATTERNS)
B_QKV_WIDTH = B_N_GROUPS * B_GROUP_WIDTH
B_LSE_LANES = LANES // B_HEADS_PER_GROUP
DEPTH = 1
DN_ALPHA = float((2 * DEPTH) ** 0.25)

IN_SIZES = (A_WIDTH, A_KV_WIDTH, A_KV_WIDTH, A_WIDTH, B_QKV_WIDTH, B_QKV_WIDTH, B_QKV_WIDTH,
            B_GROUP_WIDTH, D_MODEL, D_MODEL)
IN_OFFSETS = tuple(int(o) for o in np.cumsum((0,) + IN_SIZES[:-1]))
(OFF_QA, OFF_KA, OFF_VA, OFF_GATE_A, OFF_QB, OFF_KB, OFF_VB, OFF_GATE_B, OFF_MG_A, OFF_MG_B) = IN_OFFSETS
D_IN = sum(IN_SIZES)

VMEM_LIMIT_BYTES = 56 * 1024 * 1024

EP_NONE, EP_ROPE64, EP_ROPE128, EP_SILU, EP_SIGMOID = range(5)

PROJ_TM = 1024
PROJ_TN = 1280

PK_MG_A = 0
PK_MG_B = PK_MG_A + D_MODEL
PK_B = PK_MG_B + D_MODEL
PK_GATE_B = PK_B + 3 * B_QKV_WIDTH
PK_A = PK_GATE_B + B_GROUP_WIDTH
PK_A_WIDTH = 2 * A_WIDTH + 2 * A_KV_WIDTH
A_COL_K = A_WIDTH
A_COL_V = A_COL_K + A_KV_WIDTH
A_COL_GATE = A_COL_V + A_KV_WIDTH
assert PK_A + PK_A_WIDTH == D_IN and PK_A % PK_A_WIDTH == 0 and PK_B % B_GROUP_WIDTH == 0
assert PK_GATE_B % B_GROUP_WIDTH == 0 and D_IN % PROJ_TN == 0


def _compiler_params(semantics):
    return pltpu.CompilerParams(dimension_semantics=semantics, vmem_limit_bytes=VMEM_LIMIT_BYTES)


TAB_COS_A, TAB_SIN_A_LOW, TAB_SIN_A_HIGH, TAB_COS_B, TAB_SIN_B = range(5)


def _rope_tables(pos, freq):
    ang = pos * freq
    c = jnp.cos(ang)
    s = jnp.sin(ang)
    lane = lax.broadcasted_iota(jnp.int32, c.shape, 1)

    def expand(t):
        r32 = pltpu.roll(t, 32, axis=1)
        r64 = pltpu.roll(t, 64, axis=1)
        r96 = pltpu.roll(t, 96, axis=1)
        t_a = jnp.where(lane < 32, t, jnp.where(lane < 64, r32, jnp.where(lane < 96, r64, r96)))
        t_b = jnp.where(lane < 64, r96, r32)
        return t_a, t_b

    c_a, c_b = expand(c)
    s_a, s_b = expand(s)
    first_half_a = jnp.bitwise_and(lane, A_HEAD_DIM - 1) < A_HEAD_DIM // 2
    return (c_a, jnp.where(first_half_a, -s_a, 0.0), jnp.where(first_half_a, 0.0, s_a),
            c_b, jnp.where(lane < 64, -s_b, s_b))


def _rope_inputs(positions):
    s_len = positions.shape[1]
    half_a, half_b = A_HEAD_DIM // 2, B_HEAD_DIM // 2
    inv_a = ROPE_THETA ** (-jnp.arange(half_a, dtype=F32) / half_a)
    inv_b = ROPE_THETA ** (-jnp.arange(half_b, dtype=F32) / half_b)
    freq = jnp.concatenate([inv_a, inv_b, jnp.zeros((LANES - half_a - half_b,), F32)])[None, :]
    pos = jnp.broadcast_to(positions.reshape(s_len, 1).astype(F32), (s_len, LANES))
    return pos, freq


def _proj_kernel(tile_kinds, table_rows, x_ref, w_ref, bias_ref, pos_ref, freq_ref, out_ref, xb_ref, tab_ref):
    n = pl.program_id(1)
    tm = x_ref.shape[0]
    has_rope = [any(k in (EP_ROPE64, EP_ROPE128) for k in kinds) for kinds in tile_kinds]
    n_table_tiles = has_rope.index(True)
    assert n_table_tiles * table_rows >= tm and len(set(tile_kinds[:n_table_tiles])) == 1

    @pl.when(n == 0)
    def _():
        xb_ref[...] = x_ref[...].astype(BF16)

    def body(kinds, fill_tables):
        acc = jnp.dot(xb_ref[...], w_ref[...], preferred_element_type=F32)
        if fill_tables:
            start = pl.multiple_of(jnp.minimum(n * table_rows, tm - table_rows), 8)
            tables = _rope_tables(pos_ref[pl.ds(start, table_rows), :], freq_ref[...])
            for ti, t in enumerate(tables):
                tab_ref[ti, pl.ds(start, table_rows), :] = t
        for ci, kind in enumerate(kinds):
            cols = slice(ci * LANES, (ci + 1) * LANES)
            t = acc[:, cols]
            if kind == EP_ROPE64:
                t = (t * tab_ref[TAB_COS_A] + pltpu.roll(t, 96, axis=1) * tab_ref[TAB_SIN_A_LOW]
                     + pltpu.roll(t, 32, axis=1) * tab_ref[TAB_SIN_A_HIGH])
            elif kind == EP_ROPE128:
                t = t * tab_ref[TAB_COS_B] + pltpu.roll(t, 64, axis=1) * tab_ref[TAB_SIN_B]
            elif kind == EP_SILU:
                t = t * jax.nn.sigmoid(t)
            elif kind == EP_SIGMOID:
                t = jax.nn.sigmoid(t + bias_ref[:, cols])
            out_ref[:, cols] = t.astype(out_ref.dtype)

    branches = [(tile_kinds[0], list(range(n_table_tiles)), True)]
    for i in range(n_table_tiles, len(tile_kinds)):
        same = [br for br in branches if br[0] == tile_kinds[i] and not br[2]]
        if same:
            same[0][1].append(i)
        else:
            branches.append((tile_kinds[i], [i], False))
    for kinds, tiles, fill_tables in branches:
        cond = n == tiles[0]
        for i in tiles[1:]:
            cond = cond | (n == i)
        pl.when(cond)(functools.partial(body, kinds, fill_tables))


def _in_projection(x2d, w, bias, pos, freq, chunk_kinds):
    s_len, k_dim = x2d.shape
    n_cols = w.shape[1]
    n_tiles = n_cols // PROJ_TN
    chunks_per_tile = PROJ_TN // LANES
    tile_kinds = tuple(tuple(chunk_kinds[t * chunks_per_tile:(t + 1) * chunks_per_tile]) for t in range(n_tiles))
    tm = min(PROJ_TM, s_len)
    n_table_tiles = [any(k in (EP_ROPE64, EP_ROPE128) for k in kinds) for kinds in tile_kinds].index(True)
    table_rows = -(-tm // (8 * n_table_tiles)) * 8
    row_spec = lambda width: pl.BlockSpec((tm, width), lambda m, n: (m, 0))
    return pl.pallas_call(
        functools.partial(_proj_kernel, tile_kinds, table_rows),
        grid=(s_len // tm, n_tiles),
        in_specs=[row_spec(k_dim),
                  pl.BlockSpec((k_dim, PROJ_TN), lambda m, n: (0, n)),
                  pl.BlockSpec((1, PROJ_TN), lambda m, n: (0, n)),
                  row_spec(LANES),
                  pl.BlockSpec((1, LANES), lambda m, n: (0, 0))],
        out_specs=pl.BlockSpec((tm, PROJ_TN), lambda m, n: (m, n)),
        out_shape=jax.ShapeDtypeStruct((s_len, n_cols), BF16),
        scratch_shapes=[pltpu.VMEM((tm, k_dim), BF16), pltpu.VMEM((5, tm, LANES), F32)],
        compiler_params=_compiler_params(("arbitrary", "arbitrary")),
        name="in_projection",
    )(x2d, w, bias, pos, freq)


PACK_COLS = 256
PACK_BLOCKS_PER_STEP = PROJ_TN // PACK_COLS


def _packed_layout():
    g = B_GROUP_WIDTH
    assert OFF_VA == OFF_KA + A_KV_WIDTH
    segments = [(OFF_MG_A, D_MODEL, [EP_SIGMOID]), (OFF_MG_B, D_MODEL, [EP_SIGMOID])]
    for gi in range(B_N_GROUPS):
        segments += [(OFF_QB + gi * g, g, [EP_ROPE128]), (OFF_KB + gi * g, g, [EP_ROPE128]),
                     (OFF_VB + gi * g, g, [EP_NONE])]
    segments += [(OFF_GATE_B, g, [EP_SILU]), (OFF_QA, A_WIDTH, [EP_ROPE64]),
                 (OFF_KA, 2 * A_KV_WIDTH, [EP_ROPE64] * (A_KV_WIDTH // LANES) + [EP_NONE] * (A_KV_WIDTH // LANES)),
                 (OFF_GATE_A, A_WIDTH, [EP_SILU])]
    perm, kinds = [], []
    for off, width, seg_kinds in segments:
        assert off % PACK_COLS == 0 and width % PACK_COLS == 0
        perm += [off // PACK_COLS + j for j in range(width // PACK_COLS)]
        kinds += seg_kinds * (width // LANES // len(seg_kinds))
    assert sorted(perm) == list(range(D_IN // PACK_COLS)) and len(kinds) == D_IN // LANES
    return np.asarray(perm, np.int32), kinds


def _pack_kernel(perm_ref, *refs):
    del perm_ref
    out_ref = refs[-1]
    for j, w_ref in enumerate(refs[:-1]):
        out_ref[:, j * PACK_COLS:(j + 1) * PACK_COLS] = w_ref[...].astype(out_ref.dtype)


def _pack_weights(w_in, b_gate):
    perm, kinds = _packed_layout()
    k_dim = w_in.shape[0]
    n_steps = D_IN // PROJ_TN
    src = lambda j: pl.BlockSpec((k_dim, PACK_COLS), lambda i, perm_ref: (0, perm_ref[i * PACK_BLOCKS_PER_STEP + j]))
    w = pl.pallas_call(
        _pack_kernel,
        grid_spec=pltpu.PrefetchScalarGridSpec(
            num_scalar_prefetch=1, grid=(n_steps,),
            in_specs=[src(j) for j in range(PACK_BLOCKS_PER_STEP)],
            out_specs=pl.BlockSpec((k_dim, PROJ_TN), lambda i, perm_ref: (0, i))),
        out_shape=jax.ShapeDtypeStruct((k_dim, D_IN), BF16),
        compiler_params=_compiler_params(("arbitrary",)),
        name="pack_weights",
    )(jnp.asarray(perm), *([w_in] * PACK_BLOCKS_PER_STEP))
    bias = jnp.concatenate([b_gate[0], b_gate[1], jnp.zeros((D_IN - 2 * D_MODEL,), F32)])[None, :]
    return w, kinds, bias


ATT_TOKENS = 512
MIXER_B_TOKENS = 1024
DEINTERLEAVE_MAX_STRIDE = 4


def _band_bias(max_dist, reps):
    q_idx = np.arange(BLOCK)[:, None] + BLOCK
    k_idx = np.arange(2 * BLOCK)[None, :]
    dist = q_idx - k_idx
    band = (dist >= 0) & (dist <= max_dist)
    first = band & (k_idx >= BLOCK)
    both = np.stack([first, band]).astype(bool)
    bias = np.where(both, 0.0, -np.inf).astype(np.float32)
    return jnp.asarray(np.tile(bias, (1, 1, reps)))


def _block_bias(bias_ref, step, b):
    if b == 0:
        return bias_ref[jnp.where(step == 0, 0, 1)]
    return bias_ref[1]


def _swap_lane_halves(t):
    return pltpu.roll(t, LANES // 2, axis=1)


def _mixer_a_kernel(a_ref, bias_ref, out_ref, kbuf_ref, vbuf_ref):
    step = pl.program_id(0)
    tq = a_ref.shape[0]
    n_blocks = tq // BLOCK
    n_pairs = A_HEADS // 2
    pairs_per_group = n_pairs // A_KV_HEADS

    @pl.when(step == 0)
    def _():
        kbuf_ref[0:BLOCK, :] = jnp.zeros((BLOCK, A_KV_WIDTH), BF16)
        vbuf_ref[0:BLOCK, :] = jnp.zeros((BLOCK, A_KV_WIDTH), BF16)

    kbuf_ref[BLOCK:, :] = a_ref[:, A_COL_K:A_COL_K + A_KV_WIDTH]
    vbuf_ref[BLOCK:, :] = a_ref[:, A_COL_V:A_COL_V + A_KV_WIDTH]
    lane2 = lax.broadcasted_iota(jnp.int32, (2 * BLOCK, LANES), 1)
    low2 = lane2 < LANES // 2
    sink_slot = lax.broadcasted_iota(jnp.int32, (2 * BLOCK, LANES), 0) == 0
    denom_cols = jnp.concatenate([jnp.where(low2, 1.0, 0.0), jnp.where(low2, 0.0, 1.0)], axis=0).astype(BF16)

    for b in range(n_blocks):
        bias_kind = jnp.where(step == 0, 0, 1) if b == 0 else 1
        rows = slice(b * BLOCK, (b + 1) * BLOCK)
        kk = jnp.where(sink_slot, 0.0, kbuf_ref[b * BLOCK:(b + 2) * BLOCK, :].astype(F32))
        vv = jnp.where(sink_slot, 0.0, vbuf_ref[b * BLOCK:(b + 2) * BLOCK, :].astype(F32))
        kk_sw = _swap_lane_halves(kk)
        vv_sw = _swap_lane_halves(vv)
        k2, v2 = [], []
        for g in range(A_KV_HEADS):
            if g == 0:
                k_top, k_bot = jnp.where(low2, kk, 0.0), jnp.where(low2, 0.0, kk_sw)
                v_top, v_bot = jnp.where(low2, vv, 0.0), jnp.where(low2, 0.0, vv_sw)
            else:
                k_top, k_bot = jnp.where(low2, kk_sw, 0.0), jnp.where(low2, 0.0, kk)
                v_top, v_bot = jnp.where(low2, vv_sw, 0.0), jnp.where(low2, 0.0, vv)
            k2.append(jnp.concatenate([k_top, k_bot], axis=0).astype(BF16))
            v2.append(jnp.concatenate([jnp.concatenate([v_top, v_bot], axis=0).astype(BF16), denom_cols], axis=1))
        for p in range(n_pairs):
            g = p // pairs_per_group
            cols = slice(p * LANES, (p + 1) * LANES)
            qp = a_ref[rows, cols] * jnp.asarray(A_HEAD_DIM ** -0.5, BF16)
            s = (lax.dot_general(qp, k2[g], (((1,), (1,)), ((), ())), preferred_element_type=F32)
                 + bias_ref[bias_kind, p])
            m0b = jnp.broadcast_to(jnp.max(s[:, :2 * BLOCK], axis=1, keepdims=True), (BLOCK, LANES))
            m1b = jnp.broadcast_to(jnp.max(s[:, 2 * BLOCK:], axis=1, keepdims=True), (BLOCK, LANES))
            shifts = (m0b, m0b, m1b, m1b)
            prob = jnp.concatenate([jnp.exp(s[:, j * LANES:(j + 1) * LANES] - shifts[j]) for j in range(4)],
                                   axis=1).astype(BF16)
            o2 = jnp.dot(prob, v2[g], preferred_element_type=F32)
            o = o2[:, :LANES] / o2[:, LANES:]
            gate = a_ref[rows, A_COL_GATE + p * LANES:A_COL_GATE + (p + 1) * LANES].astype(F32)
            out_ref[rows, cols] = (o * gate).astype(out_ref.dtype)

    kbuf_ref[0:BLOCK, :] = kbuf_ref[tq:tq + BLOCK, :]
    vbuf_ref[0:BLOCK, :] = vbuf_ref[tq:tq + BLOCK, :]


def _mixer_a(h, sinks):
    s_len = h.shape[0]
    tq = ATT_TOKENS
    n_pairs = A_HEADS // 2
    band = _band_bias(A_WINDOW - 1, 2)
    sink_pairs = sinks.astype(F32).reshape(n_pairs, 2)
    col = jnp.arange(4 * BLOCK)
    bias = jnp.broadcast_to(band[:, None], (2, n_pairs, BLOCK, 4 * BLOCK))
    bias = jnp.where(col == 0, sink_pairs[None, :, 0, None, None], bias)
    bias = jnp.where(col == 2 * BLOCK, sink_pairs[None, :, 1, None, None], bias)
    return pl.pallas_call(
        _mixer_a_kernel,
        grid=(s_len // tq,),
        in_specs=[
            pl.BlockSpec((tq, PK_A_WIDTH), lambda i: (i, PK_A // PK_A_WIDTH)),
            pl.BlockSpec(bias.shape, lambda i: (0, 0, 0, 0), pipeline_mode=pl.Buffered(1)),
        ],
        out_specs=pl.BlockSpec((tq, A_WIDTH), lambda i: (i, 0)),
        out_shape=jax.ShapeDtypeStruct((s_len, A_WIDTH), BF16),
        scratch_shapes=[pltpu.VMEM((tq + BLOCK, A_KV_WIDTH), BF16), pltpu.VMEM((tq + BLOCK, A_KV_WIDTH), BF16)],
        compiler_params=_compiler_params(("arbitrary",)),
        name="mixer_a",
    )(h, bias)


def _mixer_b_kernel(d, nb, q_ref, k_ref, v_ref, bias_ref, o_ref, lse_ref, qsub, ksub, vsub, *stage):
    step = pl.program_id(0)
    n_heads = B_HEADS_PER_GROUP
    span = BLOCK * d

    @pl.when(step == 0)
    def _():
        ksub[:, 0:BLOCK, :] = jnp.zeros((d, BLOCK, B_GROUP_WIDTH), BF16)
        vsub[:, 0:BLOCK, :] = jnp.zeros((d, BLOCK, B_GROUP_WIDTH), BF16)

    if d == 1:
        qsub[0] = q_ref[...]
        ksub[0, BLOCK:, :] = k_ref[...]
        vsub[0, BLOCK:, :] = v_ref[...]
    else:
        slab, mid, ostage, lstage = stage
        f1 = DEINTERLEAVE_MAX_STRIDE if d > DEINTERLEAVE_MAX_STRIDE else 1
        f2 = d // f1
        t_rows = nb * span
        for src, dst, row0 in ((q_ref, qsub, 0), (k_ref, ksub, BLOCK), (v_ref, vsub, BLOCK)):
            for c in range(n_heads):
                cols = slice(c * LANES, (c + 1) * LANES)
                sl = slab.at[c]
                sl[...] = src[:, cols].astype(F32)
                for r1 in range(f1):
                    if f1 > 1:
                        md = mid.at[c, r1]
                        md[...] = sl[pl.ds(r1, t_rows // f1, stride=f1), :]
                    else:
                        md = sl
                    for r2 in range(f2):
                        for b in range(nb):
                            piece = md[pl.ds(b * BLOCK * f2 + r2, BLOCK, stride=f2), :]
                            dst[r1 + f1 * r2, row0 + b * BLOCK:row0 + (b + 1) * BLOCK, cols] = piece.astype(BF16)

    ones = jnp.ones((2 * BLOCK, LANES), BF16)
    lane = lax.broadcasted_iota(jnp.int32, (BLOCK, LANES), 1)
    scale = B_HEAD_DIM ** -0.5
    exp2_scale = scale * float(np.log2(np.e))
    for r in range(d):
        for b in range(nb):
            bias = _block_bias(bias_ref, step, b)
            rows = slice(b * BLOCK, (b + 1) * BLOCK)
            lse_tile = jnp.zeros((BLOCK, LANES), F32)
            for h in range(n_heads):
                cols = slice(h * B_HEAD_DIM, (h + 1) * B_HEAD_DIM)
                kk = ksub[r, b * BLOCK:(b + 2) * BLOCK, cols]
                v2 = jnp.concatenate([vsub[r, b * BLOCK:(b + 2) * BLOCK, cols], ones], axis=1)
                s = lax.dot_general(qsub[r, rows, cols], kk, (((1,), (1,)), ((), ())),
                                    preferred_element_type=F32) + bias
                m = jnp.max(s, axis=1, keepdims=True)
                mb = jnp.broadcast_to(m, (BLOCK, LANES))
                prob = jnp.concatenate([jnp.exp2((s[:, :LANES] - mb) * exp2_scale),
                                        jnp.exp2((s[:, LANES:] - mb) * exp2_scale)], axis=1).astype(BF16)
                o2 = jnp.dot(prob, v2, preferred_element_type=F32)
                denom = o2[:, LANES:]
                o = o2[:, :LANES] / denom
                lse_h = mb * scale + jnp.log(denom)
                in_head = jnp.logical_and(lane >= h * B_LSE_LANES, lane < (h + 1) * B_LSE_LANES)
                lse_tile = jnp.where(in_head, lse_h, lse_tile)
                if d == 1:
                    o_ref[rows, cols] = o.astype(o_ref.dtype)
                else:
                    ostage[h, pl.ds(b * span + r, BLOCK, stride=d), :] = o
            if d == 1:
                lse_ref[rows, :] = lse_tile
            else:
                lstage[pl.ds(b * span + r, BLOCK, stride=d), :] = lse_tile

    if d > 1:
        for h in range(n_heads):
            o_ref[:, h * B_HEAD_DIM:(h + 1) * B_HEAD_DIM] = ostage[h].astype(o_ref.dtype)
        lse_ref[...] = lstage[...]
    ksub[:, 0:BLOCK, :] = ksub[:, nb * BLOCK:(nb + 1) * BLOCK, :]
    vsub[:, 0:BLOCK, :] = vsub[:, nb * BLOCK:(nb + 1) * BLOCK, :]


def _mixer_b_group(h, gi):
    s_len = h.shape[0]
    window, d = B_PATTERNS[gi]
    assert window // d == BLOCK
    nb = max(1, MIXER_B_TOKENS // (BLOCK * d))
    t_rows = nb * BLOCK * d
    f1 = DEINTERLEAVE_MAX_STRIDE if d > DEINTERLEAVE_MAX_STRIDE else 1
    assert d % f1 == 0 and d // f1 <= DEINTERLEAVE_MAX_STRIDE
    gw = B_GROUP_WIDTH
    col0 = (PK_B + gi * 3 * gw) // gw
    bias = _band_bias(BLOCK, 1)
    blk = lambda c: pl.BlockSpec((t_rows, gw), lambda i: (i, c))
    scratch = [pltpu.VMEM((d, nb * BLOCK, gw), BF16), pltpu.VMEM((d, (nb + 1) * BLOCK, gw), BF16),
               pltpu.VMEM((d, (nb + 1) * BLOCK, gw), BF16)]
    if d > 1:
        mid_rows = t_rows // f1 if f1 > 1 else 8
        scratch += [pltpu.VMEM((B_HEADS_PER_GROUP, t_rows, LANES), F32),
                    pltpu.VMEM((B_HEADS_PER_GROUP, f1, mid_rows, LANES), F32),
                    pltpu.VMEM((B_HEADS_PER_GROUP, t_rows, LANES), F32), pltpu.VMEM((t_rows, LANES), F32)]
    return pl.pallas_call(
        functools.partial(_mixer_b_kernel, d, nb),
        grid=(s_len // t_rows,),
        in_specs=[blk(col0), blk(col0 + 1), blk(col0 + 2), pl.BlockSpec(bias.shape, lambda i: (0, 0, 0))],
        out_specs=[pl.BlockSpec((t_rows, gw), lambda i: (i, 0)), pl.BlockSpec((t_rows, LANES), lambda i: (i, 0))],
        out_shape=[jax.ShapeDtypeStruct((s_len, gw), BF16), jax.ShapeDtypeStruct((s_len, LANES), F32)],
        scratch_shapes=scratch,
        compiler_params=_compiler_params(("arbitrary",)),
        name=f"mixer_b_d{d}",
    )(h, h, h, bias)


TAIL_TM = 512


def _tail_kernel(x_ref, ya_ref, o0_ref, o1_ref, o2_ref, l0_ref, l1_ref, l2_ref, gate_b_ref, sig_a_ref, sig_b_ref,
                 w_pa_ref, w_pb_ref, w_out_ref, ln_g_ref, ln_b_ref, out_ref):
    tm = x_ref.shape[0]
    outs = [o0_ref, o1_ref, o2_ref]
    ya = ya_ref[...]
    if True:
        rows = slice(0, tm)
        lses = [l0_ref[rows, :], l1_ref[rows, :], l2_ref[rows, :]]
        m = jnp.maximum(jnp.maximum(lses[0], lses[1]), lses[2])
        es = [jnp.exp(l - m) for l in lses]
        inv = 1.0 / (es[0] + es[1] + es[2])
        wts = [e * inv for e in es]
        yb_cols = []
        for h in range(B_HEADS_PER_GROUP):
            cols = slice(h * B_HEAD_DIM, (h + 1) * B_HEAD_DIM)
            acc = None
            for gi in range(B_N_GROUPS):
                w_h = jnp.broadcast_to(wts[gi][:, h * B_LSE_LANES:h * B_LSE_LANES + 1], (tm, B_HEAD_DIM))
                term = w_h * outs[gi][rows, cols].astype(F32)
                acc = term if acc is None else acc + term
            yb_cols.append((acc * gate_b_ref[rows, cols].astype(F32)).astype(BF16))
        yb = jnp.concatenate(yb_cols, axis=1)
        y_a = jnp.dot(ya, w_pa_ref[...], preferred_element_type=F32)
        y_b = jnp.dot(yb, w_pb_ref[...], preferred_element_type=F32)
        merged = sig_a_ref[rows, :].astype(F32) * y_a + sig_b_ref[rows, :].astype(F32) * y_b
        sub = jnp.dot(merged.astype(BF16), w_out_ref[...], preferred_element_type=F32)
        z = DN_ALPHA * x_ref[rows, :] + sub
        inv_d = 1.0 / z.shape[-1]
        mu = jnp.sum(z, axis=-1, keepdims=True) * inv_d
        var = jnp.sum(z * z, axis=-1, keepdims=True) * inv_d - mu * mu
        out_ref[rows, :] = ((z - mu) * lax.rsqrt(var + LN_EPS) * ln_g_ref[...] + ln_b_ref[...]).astype(out_ref.dtype)


def _tail(x2d, ya, o_groups, lse_groups, h, w_pa, w_pb, w_out, ln_g, ln_b):
    s_len = x2d.shape[0]
    tm = TAIL_TM
    row = lambda width, cblk=0: pl.BlockSpec((tm, width), lambda i: (i, cblk))
    full = lambda a: pl.BlockSpec(a.shape, lambda i: (0,) * a.ndim, pipeline_mode=pl.Buffered(1))
    gw = B_GROUP_WIDTH
    return pl.pallas_call(
        _tail_kernel,
        grid=(s_len // tm,),
        in_specs=[row(D_MODEL), row(A_WIDTH), row(gw), row(gw), row(gw), row(LANES), row(LANES), row(LANES),
                  row(gw, PK_GATE_B // gw), row(D_MODEL, PK_MG_A // D_MODEL), row(D_MODEL, PK_MG_B // D_MODEL),
                  full(w_pa), full(w_pb), full(w_out), full(ln_g), full(ln_b)],
        out_specs=row(D_MODEL),
        out_shape=jax.ShapeDtypeStruct((s_len, D_MODEL), x2d.dtype),
        compiler_params=_compiler_params(("arbitrary",)),
        name="tail",
    )(x2d, ya, *o_groups, *lse_groups, h, h, h, w_pa, w_pb, w_out, ln_g, ln_b)


def _hybrid_layer(x, positions, w_in, b_gate, sinks, w_pa, w_pb, w_out, ln_g, ln_b):
    bn, s_len, d_model = x.shape
    assert bn == 1 and d_model == D_MODEL
    assert s_len % (BLOCK * B_PATTERNS[-1][1]) == 0 and s_len % PROJ_TM == 0
    x2d = x.reshape(s_len, d_model)
    pos, freq = _rope_inputs(positions)
    w, kinds, bias = _pack_weights(w_in, b_gate)
    h = _in_projection(x2d, w, bias, pos, freq, kinds)
    o_groups, lse_groups = [], []
    for gi in range(B_N_GROUPS):
        o, lse = _mixer_b_group(h, gi)
        o_groups.append(o)
        lse_groups.append(lse)
    ya = _mixer_a(h, sinks)
    out = _tail(x2d, ya, o_groups, lse_groups, h, w_pa.astype(BF16), w_pb.astype(BF16), w_out.astype(BF16),
                ln_g.reshape(1, d_model).astype(F32), ln_b.reshape(1, d_model).astype(F32))
    return out.reshape(bn, s_len, d_model)


def kernel(x, positions, w_in, b_gate, sinks, w_pa, w_pb, w_out, ln_g, ln_b):
    for layer in range(w_in.shape[0]):
        x = _hybrid_layer(x, positions, w_in[layer], b_gate[layer], sinks[layer], w_pa[layer], w_pb[layer],
                          w_out[layer], ln_g[layer], ln_b[layer])
    return x
```

```python
import functools

import jax
import jax.numpy as jnp
import numpy as np
from jax import lax
from jax.experimental import pallas as pl
from jax.experimental.pallas import tpu as pltpu

F32 = jnp.float32
BF16 = jnp.bfloat16

D_MODEL = 2048
ROPE_THETA = 10000.0
LN_EPS = 1e-5
BLOCK = 128
LANES = 128
A_HEADS = 16
A_KV_HEADS = 2
A_HEAD_DIM = 64
A_WINDOW = 128
A_WIDTH = A_HEADS * A_HEAD_DIM
A_KV_WIDTH = A_KV_HEADS * A_HEAD_DIM
B_PATTERNS = ((128, 1), (512, 4), (2048, 16))
B_HEADS_PER_GROUP = 4
B_HEAD_DIM = 128
B_GROUP_WIDTH = B_HEADS_PER_GROUP * B_HEAD_DIM
B_N_GROUPS = len(B_PATTERNS)
B_QKV_WIDTH = B_N_GROUPS * B_GROUP_WIDTH
B_LSE_LANES = LANES // B_HEADS_PER_GROUP
DEPTH = 1
DN_ALPHA = float((2 * DEPTH) ** 0.25)

IN_SIZES = (A_WIDTH, A_KV_WIDTH, A_KV_WIDTH, A_WIDTH, B_QKV_WIDTH, B_QKV_WIDTH, B_QKV_WIDTH,
            B_GROUP_WIDTH, D_MODEL, D_MODEL)
IN_OFFSETS = tuple(int(o) for o in np.cumsum((0,) + IN_SIZES[:-1]))
(OFF_QA, OFF_KA, OFF_VA, OFF_GATE_A, OFF_QB, OFF_KB, OFF_VB, OFF_GATE_B, OFF_MG_A, OFF_MG_B) = IN_OFFSETS
D_IN = sum(IN_SIZES)

VMEM_LIMIT_BYTES = 56 * 1024 * 1024

EP_NONE, EP_ROPE64, EP_ROPE128, EP_SILU, EP_SIGMOID = range(5)

PROJ_TM = 1024
PROJ_TN = 2304

PK_MG_A = 0
PK_MG_B = PK_MG_A + D_MODEL
PK_B = PK_MG_B + D_MODEL
PK_GATE_B = PK_B + 3 * B_QKV_WIDTH
PK_A = PK_GATE_B + B_GROUP_WIDTH
PK_A_WIDTH = 2 * A_WIDTH + 2 * A_KV_WIDTH
A_COL_K = A_WIDTH
A_COL_V = A_COL_K + A_KV_WIDTH
A_COL_GATE = A_COL_V + A_KV_WIDTH
assert PK_A + PK_A_WIDTH == D_IN and PK_A % PK_A_WIDTH == 0 and PK_B % B_GROUP_WIDTH == 0
assert PK_GATE_B % B_GROUP_WIDTH == 0 and D_IN % PROJ_TN == 0


def _compiler_params(semantics):
    return pltpu.CompilerParams(dimension_semantics=semantics, vmem_limit_bytes=VMEM_LIMIT_BYTES)


TAB_COS_A, TAB_SIN_A_LOW, TAB_SIN_A_HIGH, TAB_COS_B, TAB_SIN_B = range(5)


def _rope_tables(pos, freq):
    ang = pos * freq
    c = jnp.cos(ang)
    s = jnp.sin(ang)
    lane = lax.broadcasted_iota(jnp.int32, c.shape, 1)

    def expand(t):
        r32 = pltpu.roll(t, 32, axis=1)
        r64 = pltpu.roll(t, 64, axis=1)
        r96 = pltpu.roll(t, 96, axis=1)
        t_a = jnp.where(lane < 32, t, jnp.where(lane < 64, r32, jnp.where(lane < 96, r64, r96)))
        t_b = jnp.where(lane < 64, r96, r32)
        return t_a, t_b

    c_a, c_b = expand(c)
    s_a, s_b = expand(s)
    first_half_a = jnp.bitwise_and(lane, A_HEAD_DIM - 1) < A_HEAD_DIM // 2
    return (c_a, jnp.where(first_half_a, -s_a, 0.0), jnp.where(first_half_a, 0.0, s_a),
            c_b, jnp.where(lane < 64, -s_b, s_b))


def _rope_inputs(positions):
    s_len = positions.shape[1]
    half_a, half_b = A_HEAD_DIM // 2, B_HEAD_DIM // 2
    inv_a = ROPE_THETA ** (-jnp.arange(half_a, dtype=F32) / half_a)
    inv_b = ROPE_THETA ** (-jnp.arange(half_b, dtype=F32) / half_b)
    freq = jnp.concatenate([inv_a, inv_b, jnp.zeros((LANES - half_a - half_b,), F32)])[None, :]
    pos = jnp.broadcast_to(positions.reshape(s_len, 1).astype(F32), (s_len, LANES))
    return pos, freq


def _proj_kernel(tile_kinds, table_rows, x_ref, w_ref, bias_ref, pos_ref, freq_ref, out_ref, xb_ref, tab_ref):
    n = pl.program_id(1)
    tm = x_ref.shape[0]
    has_rope = [any(k in (EP_ROPE64, EP_ROPE128) for k in kinds) for kinds in tile_kinds]
    n_table_tiles = has_rope.index(True)
    assert n_table_tiles * table_rows >= tm and len(set(tile_kinds[:n_table_tiles])) == 1

    @pl.when(n == 0)
    def _():
        xb_ref[...] = x_ref[...].astype(BF16)

    def body(kinds, fill_tables):
        acc = jnp.dot(xb_ref[...], w_ref[...], preferred_element_type=F32)
        if fill_tables:
            start = pl.multiple_of(jnp.minimum(n * table_rows, tm - table_rows), 8)
            tables = _rope_tables(pos_ref[pl.ds(start, table_rows), :], freq_ref[...])
            for ti, t in enumerate(tables):
                tab_ref[ti, pl.ds(start, table_rows), :] = t
        for ci, kind in enumerate(kinds):
            cols = slice(ci * LANES, (ci + 1) * LANES)
            t = acc[:, cols]
            if kind == EP_ROPE64:
                t = (t * tab_ref[TAB_COS_A] + pltpu.roll(t, 96, axis=1) * tab_ref[TAB_SIN_A_LOW]
                     + pltpu.roll(t, 32, axis=1) * tab_ref[TAB_SIN_A_HIGH])
            elif kind == EP_ROPE128:
                t = t * tab_ref[TAB_COS_B] + pltpu.roll(t, 64, axis=1) * tab_ref[TAB_SIN_B]
            elif kind == EP_SILU:
                t = t * jax.nn.sigmoid(t)
            elif kind == EP_SIGMOID:
                t = jax.nn.sigmoid(t + bias_ref[:, cols])
            out_ref[:, cols] = t.astype(out_ref.dtype)

    branches = [(tile_kinds[0], list(range(n_table_tiles)), True)]
    for i in range(n_table_tiles, len(tile_kinds)):
        same = [br for br in branches if br[0] == tile_kinds[i] and not br[2]]
        if same:
            same[0][1].append(i)
        else:
            branches.append((tile_kinds[i], [i], False))
    for kinds, tiles, fill_tables in branches:
        cond = n == tiles[0]
        for i in tiles[1:]:
            cond = cond | (n == i)
        pl.when(cond)(functools.partial(body, kinds, fill_tables))


def _in_projection(x2d, w, bias, pos, freq, chunk_kinds):
    s_len, k_dim = x2d.shape
    n_cols = w.shape[1]
    n_tiles = n_cols // PROJ_TN
    chunks_per_tile = PROJ_TN // LANES
    tile_kinds = tuple(tuple(chunk_kinds[t * chunks_per_tile:(t + 1) * chunks_per_tile]) for t in range(n_tiles))
    tm = min(PROJ_TM, s_len)
    n_table_tiles = [any(k in (EP_ROPE64, EP_ROPE128) for k in kinds) for kinds in tile_kinds].index(True)
    table_rows = -(-tm // (8 * n_table_tiles)) * 8
    row_spec = lambda width: pl.BlockSpec((tm, width), lambda m, n: (m, 0))
    return pl.pallas_call(
        functools.partial(_proj_kernel, tile_kinds, table_rows),
        grid=(s_len // tm, n_tiles),
        in_specs=[row_spec(k_dim),
                  pl.BlockSpec((k_dim, PROJ_TN), lambda m, n: (0, n)),
                  pl.BlockSpec((1, PROJ_TN), lambda m, n: (0, n)),
                  row_spec(LANES),
                  pl.BlockSpec((1, LANES), lambda m, n: (0, 0))],
        out_specs=pl.BlockSpec((tm, PROJ_TN), lambda m, n: (m, n)),
        out_shape=jax.ShapeDtypeStruct((s_len, n_cols), BF16),
        scratch_shapes=[pltpu.VMEM((tm, k_dim), BF16), pltpu.VMEM((5, tm, LANES), F32)],
        compiler_params=_compiler_params(("arbitrary", "arbitrary")),
        name="in_projection",
    )(x2d, w, bias, pos, freq)


PACK_COLS = 256
PACK_BLOCKS_PER_STEP = 5
PACK_TN = PACK_BLOCKS_PER_STEP * PACK_COLS
assert D_IN % PACK_TN == 0


def _packed_layout():
    g = B_GROUP_WIDTH
    assert OFF_VA == OFF_KA + A_KV_WIDTH
    segments = [(OFF_MG_A, D_MODEL, [EP_SIGMOID]), (OFF_MG_B, D_MODEL, [EP_SIGMOID])]
    for gi in range(B_N_GROUPS):
        segments += [(OFF_QB + gi * g, g, [EP_ROPE128]), (OFF_KB + gi * g, g, [EP_ROPE128]),
                     (OFF_VB + gi * g, g, [EP_NONE])]
    segments += [(OFF_GATE_B, g, [EP_SILU]), (OFF_QA, A_WIDTH, [EP_ROPE64]),
                 (OFF_KA, 2 * A_KV_WIDTH, [EP_ROPE64] * (A_KV_WIDTH // LANES) + [EP_NONE] * (A_KV_WIDTH // LANES)),
                 (OFF_GATE_A, A_WIDTH, [EP_SILU])]
    perm, kinds = [], []
    for off, width, seg_kinds in segments:
        assert off % PACK_COLS == 0 and width % PACK_COLS == 0
        perm += [off // PACK_COLS + j for j in range(width // PACK_COLS)]
        kinds += seg_kinds * (width // LANES // len(seg_kinds))
    assert sorted(perm) == list(range(D_IN // PACK_COLS)) and len(kinds) == D_IN // LANES
    return np.asarray(perm, np.int32), kinds


def _pack_kernel(perm_ref, *refs):
    del perm_ref
    out_ref = refs[-1]
    for j, w_ref in enumerate(refs[:-1]):
        out_ref[:, j * PACK_COLS:(j + 1) * PACK_COLS] = w_ref[...].astype(out_ref.dtype)


def _pack_weights(w_in, b_gate):
    perm, kinds = _packed_layout()
    k_dim = w_in.shape[0]
    n_steps = D_IN // PACK_TN
    src = lambda j: pl.BlockSpec((k_dim, PACK_COLS), lambda i, perm_ref: (0, perm_ref[i * PACK_BLOCKS_PER_STEP + j]))
    w = pl.pallas_call(
        _pack_kernel,
        grid_spec=pltpu.PrefetchScalarGridSpec(
            num_scalar_prefetch=1, grid=(n_steps,),
            in_specs=[src(j) for j in range(PACK_BLOCKS_PER_STEP)],
            out_specs=pl.BlockSpec((k_dim, PACK_TN), lambda i, perm_ref: (0, i))),
        out_shape=jax.ShapeDtypeStruct((k_dim, D_IN), BF16),
        compiler_params=_compiler_params(("arbitrary",)),
        name="pack_weights",
    )(jnp.asarray(perm), *([w_in] * PACK_BLOCKS_PER_STEP))
    bias = jnp.concatenate([b_gate[0], b_gate[1], jnp.zeros((D_IN - 2 * D_MODEL,), F32)])[None, :]
    return w, kinds, bias


ATT_TOKENS = 512
MIXER_B_TOKENS = 1024
DEINTERLEAVE_MAX_STRIDE = 4


def _band_bias(max_dist, reps):
    q_idx = np.arange(BLOCK)[:, None] + BLOCK
    k_idx = np.arange(2 * BLOCK)[None, :]
    dist = q_idx - k_idx
    band = (dist >= 0) & (dist <= max_dist)
    first = band & (k_idx >= BLOCK)
    both = np.stack([first, band]).astype(bool)
    bias = np.where(both, 0.0, -np.inf).astype(np.float32)
    return jnp.asarray(np.tile(bias, (1, 1, reps)))


def _block_bias(bias_ref, step, b):
    if b == 0:
        return bias_ref[jnp.where(step == 0, 0, 1)]
    return bias_ref[1]


def _swap_lane_halves(t):
    return pltpu.roll(t, LANES // 2, axis=1)


def _mixer_a_kernel(a_ref, bias_ref, out_ref, kbuf_ref, vbuf_ref):
    step = pl.program_id(0)
    tq = a_ref.shape[0]
    n_blocks = tq // BLOCK
    n_pairs = A_HEADS // 2
    pairs_per_group = n_pairs // A_KV_HEADS

    @pl.when(step == 0)
    def _():
        kbuf_ref[0:BLOCK, :] = jnp.zeros((BLOCK, A_KV_WIDTH), BF16)
        vbuf_ref[0:BLOCK, :] = jnp.zeros((BLOCK, A_KV_WIDTH), BF16)

    kbuf_ref[BLOCK:, :] = a_ref[:, A_COL_K:A_COL_K + A_KV_WIDTH]
    vbuf_ref[BLOCK:, :] = a_ref[:, A_COL_V:A_COL_V + A_KV_WIDTH]
    lane2 = lax.broadcasted_iota(jnp.int32, (2 * BLOCK, LANES), 1)
    low2 = lane2 < LANES // 2
    sink_slot = lax.broadcasted_iota(jnp.int32, (2 * BLOCK, LANES), 0) == 0
    denom_cols = jnp.concatenate([jnp.where(low2, 1.0, 0.0), jnp.where(low2, 0.0, 1.0)], axis=0).astype(BF16)

    for b in range(n_blocks):
        bias_kind = jnp.where(step == 0, 0, 1) if b == 0 else 1
        rows = slice(b * BLOCK, (b + 1) * BLOCK)
        kk = jnp.where(sink_slot, 0.0, kbuf_ref[b * BLOCK:(b + 2) * BLOCK, :].astype(F32))
        vv = jnp.where(sink_slot, 0.0, vbuf_ref[b * BLOCK:(b + 2) * BLOCK, :].astype(F32))
        kk_sw = _swap_lane_halves(kk)
        vv_sw = _swap_lane_halves(vv)
        k2, v2 = [], []
        for g in range(A_KV_HEADS):
            if g == 0:
                k_top, k_bot = jnp.where(low2, kk, 0.0), jnp.where(low2, 0.0, kk_sw)
                v_top, v_bot = jnp.where(low2, vv, 0.0), jnp.where(low2, 0.0, vv_sw)
            else:
                k_top, k_bot = jnp.where(low2, kk_sw, 0.0), jnp.where(low2, 0.0, kk)
                v_top, v_bot = jnp.where(low2, vv_sw, 0.0), jnp.where(low2, 0.0, vv)
            k2.append(jnp.concatenate([k_top, k_bot], axis=0).astype(BF16))
            v2.append(jnp.concatenate([jnp.concatenate([v_top, v_bot], axis=0).astype(BF16), denom_cols], axis=1))
        for p in range(n_pairs):
            g = p // pairs_per_group
            cols = slice(p * LANES, (p + 1) * LANES)
            qp = a_ref[rows, cols] * jnp.asarray(A_HEAD_DIM ** -0.5, BF16)
            s = (lax.dot_general(qp, k2[g], (((1,), (1,)), ((), ())), preferred_element_type=F32)
                 + bias_ref[bias_kind, p])
            m0b = jnp.broadcast_to(jnp.max(s[:, :2 * BLOCK], axis=1, keepdims=True), (BLOCK, LANES))
            m1b = jnp.broadcast_to(jnp.max(s[:, 2 * BLOCK:], axis=1, keepdims=True), (BLOCK, LANES))
            shifts = (m0b, m0b, m1b, m1b)
            prob = jnp.concatenate([jnp.exp(s[:, j * LANES:(j + 1) * LANES] - shifts[j]) for j in range(4)],
                                   axis=1).astype(BF16)
            o2 = jnp.dot(prob, v2[g], preferred_element_type=F32)
            o = o2[:, :LANES] / o2[:, LANES:]
            gate = a_ref[rows, A_COL_GATE + p * LANES:A_COL_GATE + (p + 1) * LANES].astype(F32)
            out_ref[rows, cols] = (o * gate).astype(out_ref.dtype)

    kbuf_ref[0:BLOCK, :] = kbuf_ref[tq:tq + BLOCK, :]
    vbuf_ref[0:BLOCK, :] = vbuf_ref[tq:tq + BLOCK, :]


def _mixer_a(h, sinks):
    s_len = h.shape[0]
    tq = ATT_TOKENS
    n_pairs = A_HEADS // 2
    band = _band_bias(A_WINDOW - 1, 2)
    sink_pairs = sinks.astype(F32).reshape(n_pairs, 2)
    col = jnp.arange(4 * BLOCK)
    bias = jnp.broadcast_to(band[:, None], (2, n_pairs, BLOCK, 4 * BLOCK))
    bias = jnp.where(col == 0, sink_pairs[None, :, 0, None, None], bias)
    bias = jnp.where(col == 2 * BLOCK, sink_pairs[None, :, 1, None, None], bias)
    return pl.pallas_call(
        _mixer_a_kernel,
        grid=(s_len // tq,),
        in_specs=[
            pl.BlockSpec((tq, PK_A_WIDTH), lambda i: (i, PK_A // PK_A_WIDTH)),
            pl.BlockSpec(bias.shape, lambda i: (0, 0, 0, 0), pipeline_mode=pl.Buffered(1)),
        ],
        out_specs=pl.BlockSpec((tq, A_WIDTH), lambda i: (i, 0)),
        out_shape=jax.ShapeDtypeStruct((s_len, A_WIDTH), BF16),
        scratch_shapes=[pltpu.VMEM((tq + BLOCK, A_KV_WIDTH), BF16), pltpu.VMEM((tq + BLOCK, A_KV_WIDTH), BF16)],
        compiler_params=_compiler_params(("arbitrary",)),
        name="mixer_a",
    )(h, bias)


def _mixer_b_kernel(d, nb, q_ref, k_ref, v_ref, bias_ref, o_ref, lse_ref, qsub, ksub, vsub, *stage):
    step = pl.program_id(0)
    n_heads = B_HEADS_PER_GROUP
    span = BLOCK * d

    @pl.when(step == 0)
    def _():
        ksub[:, 0:BLOCK, :] = jnp.zeros((d, BLOCK, B_GROUP_WIDTH), BF16)
        vsub[:, 0:BLOCK, :] = jnp.zeros((d, BLOCK, B_GROUP_WIDTH), BF16)

    if d == 1:
        qsub[0] = q_ref[...]
        ksub[0, BLOCK:, :] = k_ref[...]
        vsub[0, BLOCK:, :] = v_ref[...]
    else:
        slab, mid, ostage, lstage = stage
        f1 = DEINTERLEAVE_MAX_STRIDE if d > DEINTERLEAVE_MAX_STRIDE else 1
        f2 = d // f1
        t_rows = nb * span
        for src, dst, row0 in ((q_ref, qsub, 0), (k_ref, ksub, BLOCK), (v_ref, vsub, BLOCK)):
            for c in range(n_heads):
                cols = slice(c * LANES, (c + 1) * LANES)
                sl = slab.at[c]
                sl[...] = src[:, cols].astype(F32)
                for r1 in range(f1):
                    if f1 > 1:
                        md = mid.at[c, r1]
                        md[...] = sl[pl.ds(r1, t_rows // f1, stride=f1), :]
                    else:
                        md = sl
                    for r2 in range(f2):
                        for b in range(nb):
                            piece = md[pl.ds(b * BLOCK * f2 + r2, BLOCK, stride=f2), :]
                            dst[r1 + f1 * r2, row0 + b * BLOCK:row0 + (b + 1) * BLOCK, cols] = piece.astype(BF16)

    ones = jnp.ones((2 * BLOCK, LANES), BF16)
    lane = lax.broadcasted_iota(jnp.int32, (BLOCK, LANES), 1)
    scale = B_HEAD_DIM ** -0.5
    exp2_scale = scale * float(np.log2(np.e))
    for r in range(d):
        for b in range(nb):
            bias = _block_bias(bias_ref, step, b)
            rows = slice(b * BLOCK, (b + 1) * BLOCK)
            lse_tile = jnp.zeros((BLOCK, LANES), F32)
            for h in range(n_heads):
                cols = slice(h * B_HEAD_DIM, (h + 1) * B_HEAD_DIM)
                kk = ksub[r, b * BLOCK:(b + 2) * BLOCK, cols]
                v2 = jnp.concatenate([vsub[r, b * BLOCK:(b + 2) * BLOCK, cols], ones], axis=1)
                s = lax.dot_general(qsub[r, rows, cols], kk, (((1,), (1,)), ((), ())),
                                    preferred_element_type=F32) + bias
                m = jnp.max(s, axis=1, keepdims=True)
                mb = jnp.broadcast_to(m, (BLOCK, LANES))
                prob = jnp.concatenate([jnp.exp2((s[:, :LANES] - mb) * exp2_scale),
                                        jnp.exp2((s[:, LANES:] - mb) * exp2_scale)], axis=1).astype(BF16)
                o2 = jnp.dot(prob, v2, preferred_element_type=F32)
                denom = o2[:, LANES:]
                o = o2[:, :LANES] / denom
                lse_h = mb * scale + jnp.log(denom)
                in_head = jnp.logical_and(lane >= h * B_LSE_LANES, lane < (h + 1) * B_LSE_LANES)
                lse_tile = jnp.where(in_head, lse_h, lse_tile)
                if d == 1:
                    o_ref[rows, cols] = o.astype(o_ref.dtype)
                else:
                    ostage[h, pl.ds(b * span + r, BLOCK, stride=d), :] = o
            if d == 1:
                lse_ref[rows, :] = lse_tile
            else:
                lstage[pl.ds(b * span + r, BLOCK, stride=d), :] = lse_tile

    if d > 1:
        for h in range(n_heads):
            o_ref[:, h * B_HEAD_DIM:(h + 1) * B_HEAD_DIM] = ostage[h].astype(o_ref.dtype)
        lse_ref[...] = lstage[...]
    ksub[:, 0:BLOCK, :] = ksub[:, nb * BLOCK:(nb + 1) * BLOCK, :]
    vsub[:, 0:BLOCK, :] = vsub[:, nb * BLOCK:(nb + 1) * BLOCK, :]


def _mixer_b_group(h, gi):
    s_len = h.shape[0]
    window, d = B_PATTERNS[gi]
    assert window // d == BLOCK
    nb = max(1, MIXER_B_TOKENS // (BLOCK * d))
    t_rows = nb * BLOCK * d
    f1 = DEINTERLEAVE_MAX_STRIDE if d > DEINTERLEAVE_MAX_STRIDE else 1
    assert d % f1 == 0 and d // f1 <= DEINTERLEAVE_MAX_STRIDE
    gw = B_GROUP_WIDTH
    col0 = (PK_B + gi * 3 * gw) // gw
    bias = _band_bias(BLOCK, 1)
    blk = lambda c: pl.BlockSpec((t_rows, gw), lambda i: (i, c))
    scratch = [pltpu.VMEM((d, nb * BLOCK, gw), BF16), pltpu.VMEM((d, (nb + 1) * BLOCK, gw), BF16),
               pltpu.VMEM((d, (nb + 1) * BLOCK, gw), BF16)]
    if d > 1:
        mid_rows = t_rows // f1 if f1 > 1 else 8
        scratch += [pltpu.VMEM((B_HEADS_PER_GROUP, t_rows, LANES), F32),
                    pltpu.VMEM((B_HEADS_PER_GROUP, f1, mid_rows, LANES), F32),
                    pltpu.VMEM((B_HEADS_PER_GROUP, t_rows, LANES), F32), pltpu.VMEM((t_rows, LANES), F32)]
    return pl.pallas_call(
        functools.partial(_mixer_b_kernel, d, nb),
        grid=(s_len // t_rows,),
        in_specs=[blk(col0), blk(col0 + 1), blk(col0 + 2), pl.BlockSpec(bias.shape, lambda i: (0, 0, 0))],
        out_specs=[pl.BlockSpec((t_rows, gw), lambda i: (i, 0)), pl.BlockSpec((t_rows, LANES), lambda i: (i, 0))],
        out_shape=[jax.ShapeDtypeStruct((s_len, gw), BF16), jax.ShapeDtypeStruct((s_len, LANES), F32)],
        scratch_shapes=scratch,
        compiler_params=_compiler_params(("arbitrary",)),
        name=f"mixer_b_d{d}",
    )(h, h, h, bias)


TAIL_TM = 512


def _tail_kernel(x_ref, ya_ref, o0_ref, o1_ref, o2_ref, l0_ref, l1_ref, l2_ref, gate_b_ref, sig_a_ref, sig_b_ref,
                 w_pa_ref, w_pb_ref, w_out_ref, ln_g_ref, ln_b_ref, out_ref):
    tm = x_ref.shape[0]
    outs = [o0_ref, o1_ref, o2_ref]
    ya = ya_ref[...]
    if True:
        rows = slice(0, tm)
        lses = [l0_ref[rows, :], l1_ref[rows, :], l2_ref[rows, :]]
        m = jnp.maximum(jnp.maximum(lses[0], lses[1]), lses[2])
        es = [jnp.exp(l - m) for l in lses]
        inv = 1.0 / (es[0] + es[1] + es[2])
        wts = [e * inv for e in es]
        yb_cols = []
        for h in range(B_HEADS_PER_GROUP):
            cols = slice(h * B_HEAD_DIM, (h + 1) * B_HEAD_DIM)
            acc = None
            for gi in range(B_N_GROUPS):
                w_h = jnp.broadcast_to(wts[gi][:, h * B_LSE_LANES:h * B_LSE_LANES + 1], (tm, B_HEAD_DIM))
                term = w_h * outs[gi][rows, cols].astype(F32)
                acc = term if acc is None else acc + term
            yb_cols.append((acc * gate_b_ref[rows, cols].astype(F32)).astype(BF16))
        yb = jnp.concatenate(yb_cols, axis=1)
        y_a = jnp.dot(ya, w_pa_ref[...], preferred_element_type=F32)
        y_b = jnp.dot(yb, w_pb_ref[...], preferred_element_type=F32)
        merged = sig_a_ref[rows, :].astype(F32) * y_a + sig_b_ref[rows, :].astype(F32) * y_b
        sub = jnp.dot(merged.astype(BF16), w_out_ref[...], preferred_element_type=F32)
        z = DN_ALPHA * x_ref[rows, :] + sub
        inv_d = 1.0 / z.shape[-1]
        mu = jnp.sum(z, axis=-1, keepdims=True) * inv_d
        var = jnp.sum(z * z, axis=-1, keepdims=True) * inv_d - mu * mu
        out_ref[rows, :] = ((z - mu) * lax.rsqrt(var + LN_EPS) * ln_g_ref[...] + ln_b_ref[...]).astype(out_ref.dtype)


def _tail(x2d, ya, o_groups, lse_groups, h, w_pa, w_pb, w_out, ln_g, ln_b):
    s_len = x2d.shape[0]
    tm = TAIL_TM
    row = lambda width, cblk=0: pl.BlockSpec((tm, width), lambda i: (i, cblk))
    full = lambda a: pl.BlockSpec(a.shape, lambda i: (0,) * a.ndim, pipeline_mode=pl.Buffered(1))
    gw = B_GROUP_WIDTH
    return pl.pallas_call(
        _tail_kernel,
        grid=(s_len // tm,),
        in_specs=[row(D_MODEL), row(A_WIDTH), row(gw), row(gw), row(gw), row(LANES), row(LANES), row(LANES),
                  row(gw, PK_GATE_B // gw), row(D_MODEL, PK_MG_A // D_MODEL), row(D_MODEL, PK_MG_B // D_MODEL),
                  full(w_pa), full(w_pb), full(w_out), full(ln_g), full(ln_b)],
        out_specs=row(D_MODEL),
        out_shape=jax.ShapeDtypeStruct((s_len, D_MODEL), x2d.dtype),
        compiler_params=_compiler_params(("arbitrary",)),
        name="tail",
    )(x2d, ya, *o_groups, *lse_groups, h, h, h, w_pa, w_pb, w_out, ln_g, ln_b)


def _hybrid_layer(x, positions, w_in, b_gate, sinks, w_pa, w_pb, w_out, ln_g, ln_b):
    bn, s_len, d_model = x.shape
    assert bn == 1 and d_model == D_MODEL
    assert s_len % (BLOCK * B_PATTERNS[-1][1]) == 0 and s_len % PROJ_TM == 0
    x2d = x.reshape(s_len, d_model)
    pos, freq = _rope_inputs(positions)
    w, kinds, bias = _pack_weights(w_in, b_gate)
    h = _in_projection(x2d, w, bias, pos, freq, kinds)
    o_groups, lse_groups = [], []
    for gi in range(B_N_GROUPS):
        o, lse = _mixer_b_group(h, gi)
        o_groups.append(o)
        lse_groups.append(lse)
    ya = _mixer_a(h, sinks)
    out = _tail(x2d, ya, o_groups, lse_groups, h, w_pa.astype(BF16), w_pb.astype(BF16), w_out.astype(BF16),
                ln_g.reshape(1, d_model).astype(F32), ln_b.reshape(1, d_model).astype(F32))
    return out.reshape(bn, s_len, d_model)


def kernel(x, positions, w_in, b_gate, sinks, w_pa, w_pb, w_out, ln_g, ln_b):
    for layer in range(w_in.shape[0]):
        x = _hybrid_layer(x, positions, w_in[layer], b_gate[layer], sinks[layer], w_pa[layer], w_pb[layer],
                          w_out[layer], ln_g[layer], ln_b[layer])
    return x
```

```python
import functools

import jax
import jax.numpy as jnp
import numpy as np
from jax import lax
from jax.experimental import pallas as pl
from jax.experimental.pallas import tpu as pltpu

F32 = jnp.float32
BF16 = jnp.bfloat16

D_MODEL = 2048
ROPE_THETA = 10000.0
LN_EPS = 1e-5
BLOCK = 128
LANES = 128
A_HEADS = 16
A_KV_HEADS = 2
A_HEAD_DIM = 64
A_WINDOW = 128
A_WIDTH = A_HEADS * A_HEAD_DIM
A_KV_WIDTH = A_KV_HEADS * A_HEAD_DIM
B_PATTERNS = ((128, 1), (512, 4), (2048, 16))
B_HEADS_PER_GROUP = 4
B_HEAD_DIM = 128
B_GROUP_WIDTH = B_HEADS_PER_GROUP * B_HEAD_DIM
B_N_GROUPS = len(B_PATTERNS)
B_QKV_WIDTH = B_N_GROUPS * B_GROUP_WIDTH
B_LSE_LANES = LANES // B_HEADS_PER_GROUP
DEPTH = 1
DN_ALPHA = float((2 * DEPTH) ** 0.25)

IN_SIZES = (A_WIDTH, A_KV_WIDTH, A_KV_WIDTH, A_WIDTH, B_QKV_WIDTH, B_QKV_WIDTH, B_QKV_WIDTH,
            B_GROUP_WIDTH, D_MODEL, D_MODEL)
IN_OFFSETS = tuple(int(o) for o in np.cumsum((0,) + IN_SIZES[:-1]))
(OFF_QA, OFF_KA, OFF_VA, OFF_GATE_A, OFF_QB, OFF_KB, OFF_VB, OFF_GATE_B, OFF_MG_A, OFF_MG_B) = IN_OFFSETS
D_IN = sum(IN_SIZES)

VMEM_LIMIT_BYTES = 56 * 1024 * 1024

EP_NONE, EP_ROPE64, EP_ROPE128, EP_SILU, EP_SIGMOID = range(5)

PROJ_TM = 1024
PROJ_TN = 2304

PK_MG_A = 0
PK_MG_B = PK_MG_A + D_MODEL
PK_VB = PK_MG_B + D_MODEL
PK_GATE_B = PK_VB + B_QKV_WIDTH
PK_QKB = PK_GATE_B + B_GROUP_WIDTH
PK_A = PK_QKB + 2 * B_QKV_WIDTH
PK_A_WIDTH = 2 * A_WIDTH + 2 * A_KV_WIDTH
A_COL_K = A_WIDTH
A_COL_V = A_COL_K + A_KV_WIDTH
A_COL_GATE = A_COL_V + A_KV_WIDTH
assert PK_A + PK_A_WIDTH == D_IN and PK_A % PK_A_WIDTH == 0 and PK_VB % B_GROUP_WIDTH == 0
assert PK_GATE_B % B_GROUP_WIDTH == 0 and PK_QKB % B_GROUP_WIDTH == 0 and D_IN % PROJ_TN == 0


def _compiler_params(semantics):
    return pltpu.CompilerParams(dimension_semantics=semantics, vmem_limit_bytes=VMEM_LIMIT_BYTES)


TAB_COS_A, TAB_SIN_A_LOW, TAB_SIN_A_HIGH, TAB_COS_B, TAB_SIN_B = range(5)


def _rope_tables(pos, freq):
    ang = pos * freq
    c = jnp.cos(ang)
    s = jnp.sin(ang)
    lane = lax.broadcasted_iota(jnp.int32, c.shape, 1)

    def expand(t):
        r32 = pltpu.roll(t, 32, axis=1)
        r64 = pltpu.roll(t, 64, axis=1)
        r96 = pltpu.roll(t, 96, axis=1)
        t_a = jnp.where(lane < 32, t, jnp.where(lane < 64, r32, jnp.where(lane < 96, r64, r96)))
        t_b = jnp.where(lane < 64, r96, r32)
        return t_a, t_b

    c_a, c_b = expand(c)
    s_a, s_b = expand(s)
    first_half_a = jnp.bitwise_and(lane, A_HEAD_DIM - 1) < A_HEAD_DIM // 2
    return (c_a, jnp.where(first_half_a, -s_a, 0.0), jnp.where(first_half_a, 0.0, s_a),
            c_b, jnp.where(lane < 64, -s_b, s_b))


def _rope_inputs(positions):
    s_len = positions.shape[1]
    half_a, half_b = A_HEAD_DIM // 2, B_HEAD_DIM // 2
    inv_a = ROPE_THETA ** (-jnp.arange(half_a, dtype=F32) / half_a)
    inv_b = ROPE_THETA ** (-jnp.arange(half_b, dtype=F32) / half_b)
    freq = jnp.concatenate([inv_a, inv_b, jnp.zeros((LANES - half_a - half_b,), F32)])[None, :]
    pos = jnp.broadcast_to(positions.reshape(s_len, 1).astype(F32), (s_len, LANES))
    return pos, freq


def _proj_kernel(tile_kinds, table_rows, x_ref, w_ref, bias_ref, pos_ref, freq_ref, out_ref, xb_ref, tab_ref):
    n = pl.program_id(1)
    tm = x_ref.shape[0]
    has_rope = [any(k in (EP_ROPE64, EP_ROPE128) for k in kinds) for kinds in tile_kinds]
    n_table_tiles = has_rope.index(True)
    assert n_table_tiles * table_rows >= tm

    @pl.when(n == 0)
    def _():
        xb_ref[...] = x_ref[...].astype(BF16)

    def body(kinds, tiles, fill_tables):
        acc = jnp.dot(xb_ref[...], w_ref[...], preferred_element_type=F32)
        if fill_tables:
            if len(tiles) == 1:
                start = min(tiles[0] * table_rows, tm - table_rows)
            else:
                start = pl.multiple_of(jnp.minimum(n * table_rows, tm - table_rows), 8)
            tables = _rope_tables(pos_ref[pl.ds(start, table_rows), :], freq_ref[...])
            for ti, t in enumerate(tables):
                tab_ref[ti, pl.ds(start, table_rows), :] = t
        for ci, kind in enumerate(kinds):
            cols = slice(ci * LANES, (ci + 1) * LANES)
            t = acc[:, cols]
            if kind == EP_ROPE64:
                t = (t * tab_ref[TAB_COS_A] + pltpu.roll(t, 96, axis=1) * tab_ref[TAB_SIN_A_LOW]
                     + pltpu.roll(t, 32, axis=1) * tab_ref[TAB_SIN_A_HIGH])
            elif kind == EP_ROPE128:
                t = t * tab_ref[TAB_COS_B] + pltpu.roll(t, 64, axis=1) * tab_ref[TAB_SIN_B]
            elif kind == EP_SILU:
                t = t * jax.nn.sigmoid(t)
            elif kind == EP_SIGMOID:
                t = jax.nn.sigmoid(t + bias_ref[:, cols])
            out_ref[:, cols] = t.astype(out_ref.dtype)

    branches = []
    for i, kinds in enumerate(tile_kinds):
        same = [br for br in branches if br[0] == kinds and br[2] == (i < n_table_tiles)]
        if same:
            same[0][1].append(i)
        else:
            branches.append((kinds, [i], i < n_table_tiles))
    for kinds, tiles, fill_tables in branches:
        cond = n == tiles[0]
        for i in tiles[1:]:
            cond = cond | (n == i)
        pl.when(cond)(functools.partial(body, kinds, tiles, fill_tables))


def _in_projection(x2d, w, bias, pos, freq, chunk_kinds):
    s_len, k_dim = x2d.shape
    n_cols = w.shape[1]
    n_tiles = n_cols // PROJ_TN
    chunks_per_tile = PROJ_TN // LANES
    tile_kinds = tuple(tuple(chunk_kinds[t * chunks_per_tile:(t + 1) * chunks_per_tile]) for t in range(n_tiles))
    tm = min(PROJ_TM, s_len)
    n_table_tiles = [any(k in (EP_ROPE64, EP_ROPE128) for k in kinds) for kinds in tile_kinds].index(True)
    table_rows = -(-tm // (8 * n_table_tiles)) * 8
    row_spec = lambda width: pl.BlockSpec((tm, width), lambda m, n: (m, 0))
    return pl.pallas_call(
        functools.partial(_proj_kernel, tile_kinds, table_rows),
        grid=(s_len // tm, n_tiles),
        in_specs=[row_spec(k_dim),
                  pl.BlockSpec((k_dim, PROJ_TN), lambda m, n: (0, n)),
                  pl.BlockSpec((1, PROJ_TN), lambda m, n: (0, n)),
                  row_spec(LANES),
                  pl.BlockSpec((1, LANES), lambda m, n: (0, 0))],
        out_specs=pl.BlockSpec((tm, PROJ_TN), lambda m, n: (m, n)),
        out_shape=jax.ShapeDtypeStruct((s_len, n_cols), BF16),
        scratch_shapes=[pltpu.VMEM((tm, k_dim), BF16), pltpu.VMEM((5, tm, LANES), F32)],
        compiler_params=_compiler_params(("arbitrary", "arbitrary")),
        name="in_projection",
    )(x2d, w, bias, pos, freq)


PACK_COLS = 256
PACK_BLOCKS_PER_STEP = 5
PACK_TN = PACK_BLOCKS_PER_STEP * PACK_COLS
assert D_IN % PACK_TN == 0


def _packed_layout():
    g = B_GROUP_WIDTH
    assert OFF_VA == OFF_KA + A_KV_WIDTH
    segments = [(OFF_MG_A, D_MODEL, [EP_SIGMOID]), (OFF_MG_B, D_MODEL, [EP_SIGMOID])]
    segments += [(OFF_VB, B_QKV_WIDTH, [EP_NONE]), (OFF_GATE_B, g, [EP_SILU])]
    for gi in range(B_N_GROUPS):
        segments += [(OFF_QB + gi * g, g, [EP_ROPE128]), (OFF_KB + gi * g, g, [EP_ROPE128])]
    segments += [(OFF_QA, A_WIDTH, [EP_ROPE64]),
                 (OFF_KA, 2 * A_KV_WIDTH, [EP_ROPE64] * (A_KV_WIDTH // LANES) + [EP_NONE] * (A_KV_WIDTH // LANES)),
                 (OFF_GATE_A, A_WIDTH, [EP_SILU])]
    perm, kinds = [], []
    for off, width, seg_kinds in segments:
        assert off % PACK_COLS == 0 and width % PACK_COLS == 0
        perm += [off // PACK_COLS + j for j in range(width // PACK_COLS)]
        kinds += seg_kinds * (width // LANES // len(seg_kinds))
    assert sorted(perm) == list(range(D_IN // PACK_COLS)) and len(kinds) == D_IN // LANES
    return np.asarray(perm, np.int32), kinds


def _pack_kernel(perm_ref, *refs):
    del perm_ref
    out_ref = refs[-1]
    for j, w_ref in enumerate(refs[:-1]):
        out_ref[:, j * PACK_COLS:(j + 1) * PACK_COLS] = w_ref[...].astype(out_ref.dtype)


def _pack_weights(w_in, b_gate):
    perm, kinds = _packed_layout()
    k_dim = w_in.shape[0]
    n_steps = D_IN // PACK_TN
    src = lambda j: pl.BlockSpec((k_dim, PACK_COLS), lambda i, perm_ref: (0, perm_ref[i * PACK_BLOCKS_PER_STEP + j]))
    w = pl.pallas_call(
        _pack_kernel,
        grid_spec=pltpu.PrefetchScalarGridSpec(
            num_scalar_prefetch=1, grid=(n_steps,),
            in_specs=[src(j) for j in range(PACK_BLOCKS_PER_STEP)],
            out_specs=pl.BlockSpec((k_dim, PACK_TN), lambda i, perm_ref: (0, i))),
        out_shape=jax.ShapeDtypeStruct((k_dim, D_IN), BF16),
        compiler_params=_compiler_params(("arbitrary",)),
        name="pack_weights",
    )(jnp.asarray(perm), *([w_in] * PACK_BLOCKS_PER_STEP))
    bias = jnp.concatenate([b_gate[0], b_gate[1], jnp.zeros((D_IN - 2 * D_MODEL,), F32)])[None, :]
    return w, kinds, bias


ATT_TOKENS = 512
MIXER_B_TOKENS = 1024
DEINTERLEAVE_MAX_STRIDE = 4


def _band_bias(max_dist, reps):
    q_idx = np.arange(BLOCK)[:, None] + BLOCK
    k_idx = np.arange(2 * BLOCK)[None, :]
    dist = q_idx - k_idx
    band = (dist >= 0) & (dist <= max_dist)
    first = band & (k_idx >= BLOCK)
    both = np.stack([first, band]).astype(bool)
    bias = np.where(both, 0.0, -np.inf).astype(np.float32)
    return jnp.asarray(np.tile(bias, (1, 1, reps)))


def _block_bias(bias_ref, step, b):
    if b == 0:
        return bias_ref[jnp.where(step == 0, 0, 1)]
    return bias_ref[1]


def _swap_lane_halves(t):
    return pltpu.roll(t, LANES // 2, axis=1)


def _mixer_a_kernel(a_ref, bias_ref, out_ref, kbuf_ref, vbuf_ref):
    step = pl.program_id(0)
    tq = a_ref.shape[0]
    n_blocks = tq // BLOCK
    n_pairs = A_HEADS // 2
    pairs_per_group = n_pairs // A_KV_HEADS

    @pl.when(step == 0)
    def _():
        kbuf_ref[0:BLOCK, :] = jnp.zeros((BLOCK, A_KV_WIDTH), BF16)
        vbuf_ref[0:BLOCK, :] = jnp.zeros((BLOCK, A_KV_WIDTH), BF16)

    kbuf_ref[BLOCK:, :] = a_ref[:, A_COL_K:A_COL_K + A_KV_WIDTH]
    vbuf_ref[BLOCK:, :] = a_ref[:, A_COL_V:A_COL_V + A_KV_WIDTH]
    lane2 = lax.broadcasted_iota(jnp.int32, (2 * BLOCK, LANES), 1)
    low2 = lane2 < LANES // 2
    sink_slot = lax.broadcasted_iota(jnp.int32, (2 * BLOCK, LANES), 0) == 0
    denom_cols = jnp.concatenate([jnp.where(low2, 1.0, 0.0), jnp.where(low2, 0.0, 1.0)], axis=0).astype(BF16)

    for b in range(n_blocks):
        bias_kind = jnp.where(step == 0, 0, 1) if b == 0 else 1
        rows = slice(b * BLOCK, (b + 1) * BLOCK)
        kk = jnp.where(sink_slot, 0.0, kbuf_ref[b * BLOCK:(b + 2) * BLOCK, :].astype(F32))
        vv = jnp.where(sink_slot, 0.0, vbuf_ref[b * BLOCK:(b + 2) * BLOCK, :].astype(F32))
        kk_sw = _swap_lane_halves(kk)
        vv_sw = _swap_lane_halves(vv)
        k2, v2 = [], []
        for g in range(A_KV_HEADS):
            if g == 0:
                k_top, k_bot = jnp.where(low2, kk, 0.0), jnp.where(low2, 0.0, kk_sw)
                v_top, v_bot = jnp.where(low2, vv, 0.0), jnp.where(low2, 0.0, vv_sw)
            else:
                k_top, k_bot = jnp.where(low2, kk_sw, 0.0), jnp.where(low2, 0.0, kk)
                v_top, v_bot = jnp.where(low2, vv_sw, 0.0), jnp.where(low2, 0.0, vv)
            k2.append(jnp.concatenate([k_top, k_bot], axis=0).astype(BF16))
            v2.append(jnp.concatenate([jnp.concatenate([v_top, v_bot], axis=0).astype(BF16), denom_cols], axis=1))
        for p in range(n_pairs):
            g = p // pairs_per_group
            cols = slice(p * LANES, (p + 1) * LANES)
            qp = a_ref[rows, cols] * jnp.asarray(A_HEAD_DIM ** -0.5, BF16)
            s = (lax.dot_general(qp, k2[g], (((1,), (1,)), ((), ())), preferred_element_type=F32)
                 + bias_ref[bias_kind, p])
            m0b = jnp.broadcast_to(jnp.max(s[:, :2 * BLOCK], axis=1, keepdims=True), (BLOCK, LANES))
            m1b = jnp.broadcast_to(jnp.max(s[:, 2 * BLOCK:], axis=1, keepdims=True), (BLOCK, LANES))
            shifts = (m0b, m0b, m1b, m1b)
            prob = jnp.concatenate([jnp.exp(s[:, j * LANES:(j + 1) * LANES] - shifts[j]) for j in range(4)],
                                   axis=1).astype(BF16)
            o2 = jnp.dot(prob, v2[g], preferred_element_type=F32)
            o = o2[:, :LANES] / o2[:, LANES:]
            gate = a_ref[rows, A_COL_GATE + p * LANES:A_COL_GATE + (p + 1) * LANES].astype(F32)
            out_ref[rows, cols] = (o * gate).astype(out_ref.dtype)

    kbuf_ref[0:BLOCK, :] = kbuf_ref[tq:tq + BLOCK, :]
    vbuf_ref[0:BLOCK, :] = vbuf_ref[tq:tq + BLOCK, :]


def _mixer_a(h, sinks):
    s_len = h.shape[0]
    tq = ATT_TOKENS
    n_pairs = A_HEADS // 2
    band = _band_bias(A_WINDOW - 1, 2)
    sink_pairs = sinks.astype(F32).reshape(n_pairs, 2)
    col = jnp.arange(4 * BLOCK)
    bias = jnp.broadcast_to(band[:, None], (2, n_pairs, BLOCK, 4 * BLOCK))
    bias = jnp.where(col == 0, sink_pairs[None, :, 0, None, None], bias)
    bias = jnp.where(col == 2 * BLOCK, sink_pairs[None, :, 1, None, None], bias)
    return pl.pallas_call(
        _mixer_a_kernel,
        grid=(s_len // tq,),
        in_specs=[
            pl.BlockSpec((tq, PK_A_WIDTH), lambda i: (i, PK_A // PK_A_WIDTH)),
            pl.BlockSpec(bias.shape, lambda i: (0, 0, 0, 0), pipeline_mode=pl.Buffered(1)),
        ],
        out_specs=pl.BlockSpec((tq, A_WIDTH), lambda i: (i, 0)),
        out_shape=jax.ShapeDtypeStruct((s_len, A_WIDTH), BF16),
        scratch_shapes=[pltpu.VMEM((tq + BLOCK, A_KV_WIDTH), BF16), pltpu.VMEM((tq + BLOCK, A_KV_WIDTH), BF16)],
        compiler_params=_compiler_params(("arbitrary",)),
        name="mixer_a",
    )(h, bias)


def _mixer_b_kernel(d, nb, q_ref, k_ref, v_ref, bias_ref, o_ref, lse_ref, qsub, ksub, vsub, *stage):
    step = pl.program_id(0)
    n_heads = B_HEADS_PER_GROUP
    span = BLOCK * d

    @pl.when(step == 0)
    def _():
        ksub[:, 0:BLOCK, :] = jnp.zeros((d, BLOCK, B_GROUP_WIDTH), BF16)
        vsub[:, 0:BLOCK, :] = jnp.zeros((d, BLOCK, B_GROUP_WIDTH), BF16)

    if d == 1:
        qsub[0] = q_ref[...]
        ksub[0, BLOCK:, :] = k_ref[...]
        vsub[0, BLOCK:, :] = v_ref[...]
    else:
        slab, mid, ostage, lstage = stage
        f1 = DEINTERLEAVE_MAX_STRIDE if d > DEINTERLEAVE_MAX_STRIDE else 1
        f2 = d // f1
        t_rows = nb * span
        for src, dst, row0 in ((q_ref, qsub, 0), (k_ref, ksub, BLOCK), (v_ref, vsub, BLOCK)):
            for c in range(n_heads):
                cols = slice(c * LANES, (c + 1) * LANES)
                sl = slab.at[c]
                sl[...] = src[:, cols].astype(F32)
                for r1 in range(f1):
                    if f1 > 1:
                        md = mid.at[c, r1]
                        md[...] = sl[pl.ds(r1, t_rows // f1, stride=f1), :]
                    else:
                        md = sl
                    for r2 in range(f2):
                        for b in range(nb):
                            piece = md[pl.ds(b * BLOCK * f2 + r2, BLOCK, stride=f2), :]
                            dst[r1 + f1 * r2, row0 + b * BLOCK:row0 + (b + 1) * BLOCK, cols] = piece.astype(BF16)

    ones = jnp.ones((2 * BLOCK, LANES), BF16)
    lane = lax.broadcasted_iota(jnp.int32, (BLOCK, LANES), 1)
    scale = B_HEAD_DIM ** -0.5
    exp2_scale = scale * float(np.log2(np.e))
    for r in range(d):
        for b in range(nb):
            bias = _block_bias(bias_ref, step, b)
            rows = slice(b * BLOCK, (b + 1) * BLOCK)
            lse_tile = jnp.zeros((BLOCK, LANES), F32)
            for h in range(n_heads):
                cols = slice(h * B_HEAD_DIM, (h + 1) * B_HEAD_DIM)
                kk = ksub[r, b * BLOCK:(b + 2) * BLOCK, cols]
                v2 = jnp.concatenate([vsub[r, b * BLOCK:(b + 2) * BLOCK, cols], ones], axis=1)
                s = lax.dot_general(qsub[r, rows, cols], kk, (((1,), (1,)), ((), ())),
                                    preferred_element_type=F32) + bias
                m = jnp.max(s, axis=1, keepdims=True)
                mb = jnp.broadcast_to(m, (BLOCK, LANES))
                prob = jnp.concatenate([jnp.exp2((s[:, :LANES] - mb) * exp2_scale),
                                        jnp.exp2((s[:, LANES:] - mb) * exp2_scale)], axis=1).astype(BF16)
                o2 = jnp.dot(prob, v2, preferred_element_type=F32)
                denom = o2[:, LANES:]
                o = o2[:, :LANES] / denom
                lse_h = mb * scale + jnp.log(denom)
                in_head = jnp.logical_and(lane >= h * B_LSE_LANES, lane < (h + 1) * B_LSE_LANES)
                lse_tile = jnp.where(in_head, lse_h, lse_tile)
                if d == 1:
                    o_ref[rows, cols] = o.astype(o_ref.dtype)
                else:
                    ostage[h, pl.ds(b * span + r, BLOCK, stride=d), :] = o
            if d == 1:
                lse_ref[rows, :] = lse_tile
            else:
                lstage[pl.ds(b * span + r, BLOCK, stride=d), :] = lse_tile

    if d > 1:
        for h in range(n_heads):
            o_ref[:, h * B_HEAD_DIM:(h + 1) * B_HEAD_DIM] = ostage[h].astype(o_ref.dtype)
        lse_ref[...] = lstage[...]
    ksub[:, 0:BLOCK, :] = ksub[:, nb * BLOCK:(nb + 1) * BLOCK, :]
    vsub[:, 0:BLOCK, :] = vsub[:, nb * BLOCK:(nb + 1) * BLOCK, :]


def _mixer_b_group(h, gi):
    s_len = h.shape[0]
    window, d = B_PATTERNS[gi]
    assert window // d == BLOCK
    nb = max(1, MIXER_B_TOKENS // (BLOCK * d))
    t_rows = nb * BLOCK * d
    f1 = DEINTERLEAVE_MAX_STRIDE if d > DEINTERLEAVE_MAX_STRIDE else 1
    assert d % f1 == 0 and d // f1 <= DEINTERLEAVE_MAX_STRIDE
    gw = B_GROUP_WIDTH
    col_q, col_v = (PK_QKB + gi * 2 * gw) // gw, (PK_VB + gi * gw) // gw
    bias = _band_bias(BLOCK, 1)
    blk = lambda c: pl.BlockSpec((t_rows, gw), lambda i: (i, c))
    scratch = [pltpu.VMEM((d, nb * BLOCK, gw), BF16), pltpu.VMEM((d, (nb + 1) * BLOCK, gw), BF16),
               pltpu.VMEM((d, (nb + 1) * BLOCK, gw), BF16)]
    if d > 1:
        mid_rows = t_rows // f1 if f1 > 1 else 8
        scratch += [pltpu.VMEM((B_HEADS_PER_GROUP, t_rows, LANES), F32),
                    pltpu.VMEM((B_HEADS_PER_GROUP, f1, mid_rows, LANES), F32),
                    pltpu.VMEM((B_HEADS_PER_GROUP, t_rows, LANES), F32), pltpu.VMEM((t_rows, LANES), F32)]
    return pl.pallas_call(
        functools.partial(_mixer_b_kernel, d, nb),
        grid=(s_len // t_rows,),
        in_specs=[blk(col_q), blk(col_q + 1), blk(col_v), pl.BlockSpec(bias.shape, lambda i: (0, 0, 0))],
        out_specs=[pl.BlockSpec((t_rows, gw), lambda i: (i, 0)), pl.BlockSpec((t_rows, LANES), lambda i: (i, 0))],
        out_shape=[jax.ShapeDtypeStruct((s_len, gw), BF16), jax.ShapeDtypeStruct((s_len, LANES), F32)],
        scratch_shapes=scratch,
        compiler_params=_compiler_params(("arbitrary",)),
        name=f"mixer_b_d{d}",
    )(h, h, h, bias)


TAIL_TM = 512


def _tail_kernel(x_ref, ya_ref, o0_ref, o1_ref, o2_ref, l0_ref, l1_ref, l2_ref, gate_b_ref, sig_a_ref, sig_b_ref,
                 w_pa_ref, w_pb_ref, w_out_ref, ln_g_ref, ln_b_ref, out_ref):
    tm = x_ref.shape[0]
    outs = [o0_ref, o1_ref, o2_ref]
    lses = [l0_ref[...], l1_ref[...], l2_ref[...]]
    m = jnp.maximum(jnp.maximum(lses[0], lses[1]), lses[2])
    es = [jnp.exp(l - m) for l in lses]
    inv = 1.0 / (es[0] + es[1] + es[2])
    wts = [e * inv for e in es]
    yb_cols = []
    for h in range(B_HEADS_PER_GROUP):
        cols = slice(h * B_HEAD_DIM, (h + 1) * B_HEAD_DIM)
        acc = None
        for gi in range(B_N_GROUPS):
            w_h = jnp.broadcast_to(wts[gi][:, h * B_LSE_LANES:h * B_LSE_LANES + 1], (tm, B_HEAD_DIM))
            term = w_h * outs[gi][:, cols].astype(F32)
            acc = term if acc is None else acc + term
        yb_cols.append((acc * gate_b_ref[:, cols].astype(F32)).astype(BF16))
    yb = jnp.concatenate(yb_cols, axis=1)
    y_a = jnp.dot(ya_ref[...], w_pa_ref[...], preferred_element_type=F32)
    y_b = jnp.dot(yb, w_pb_ref[...], preferred_element_type=F32)
    merged = sig_a_ref[...].astype(F32) * y_a + sig_b_ref[...].astype(F32) * y_b
    sub = jnp.dot(merged.astype(BF16), w_out_ref[...], preferred_element_type=F32)
    z = DN_ALPHA * x_ref[...] + sub
    inv_d = 1.0 / z.shape[-1]
    mu = jnp.sum(z, axis=-1, keepdims=True) * inv_d
    var = jnp.sum(z * z, axis=-1, keepdims=True) * inv_d - mu * mu
    out_ref[...] = ((z - mu) * lax.rsqrt(var + LN_EPS) * ln_g_ref[...] + ln_b_ref[...]).astype(out_ref.dtype)


def _tail(x2d, ya, o_groups, lse_groups, h, w_pa, w_pb, w_out, ln_g, ln_b):
    s_len = x2d.shape[0]
    tm = TAIL_TM
    row = lambda width, cblk=0: pl.BlockSpec((tm, width), lambda i: (i, cblk))
    full = lambda a: pl.BlockSpec(a.shape, lambda i: (0,) * a.ndim, pipeline_mode=pl.Buffered(1))
    gw = B_GROUP_WIDTH
    return pl.pallas_call(
        _tail_kernel,
        grid=(s_len // tm,),
        in_specs=[row(D_MODEL), row(A_WIDTH), row(gw), row(gw), row(gw), row(LANES), row(LANES), row(LANES),
                  row(gw, PK_GATE_B // gw), row(D_MODEL, PK_MG_A // D_MODEL), row(D_MODEL, PK_MG_B // D_MODEL),
                  full(w_pa), full(w_pb), full(w_out), full(ln_g), full(ln_b)],
        out_specs=row(D_MODEL),
        out_shape=jax.ShapeDtypeStruct((s_len, D_MODEL), x2d.dtype),
        compiler_params=_compiler_params(("arbitrary",)),
        name="tail",
    )(x2d, ya, *o_groups, *lse_groups, h, h, h, w_pa, w_pb, w_out, ln_g, ln_b)


def _hybrid_layer(x, positions, w_in, b_gate, sinks, w_pa, w_pb, w_out, ln_g, ln_b):
    bn, s_len, d_model = x.shape
    assert bn == 1 and d_model == D_MODEL
    assert s_len % (BLOCK * B_PATTERNS[-1][1]) == 0 and s_len % PROJ_TM == 0
    x2d = x.reshape(s_len, d_model)
    pos, freq = _rope_inputs(positions)
    w, kinds, bias = _pack_weights(w_in, b_gate)
    h = _in_projection(x2d, w, bias, pos, freq, kinds)
    o_groups, lse_groups = [], []
    for gi in range(B_N_GROUPS):
        o, lse = _mixer_b_group(h, gi)
        o_groups.append(o)
        lse_groups.append(lse)
    ya = _mixer_a(h, sinks)
    out = _tail(x2d, ya, o_groups, lse_groups, h, w_pa.astype(BF16), w_pb.astype(BF16), w_out.astype(BF16),
                ln_g.reshape(1, d_model).astype(F32), ln_b.reshape(1, d_model).astype(F32))
    return out.reshape(bn, s_len, d_model)


def kernel(x, positions, w_in, b_gate, sinks, w_pa, w_pb, w_out, ln_g, ln_b):
    for layer in range(w_in.shape[0]):
        x = _hybrid_layer(x, positions, w_in[layer], b_gate[layer], sinks[layer], w_pa[layer], w_pb[layer],
                          w_out[layer], ln_g[layer], ln_b[layer])
    return x
```

```python
import functools

import jax
import jax.numpy as jnp
import numpy as np
from jax import lax
from jax.experimental import pallas as pl
from jax.experimental.pallas import tpu as pltpu

F32 = jnp.float32
BF16 = jnp.bfloat16

D_MODEL = 2048
ROPE_THETA = 10000.0
LN_EPS = 1e-5
BLOCK = 128
LANES = 128
A_HEADS = 16
A_KV_HEADS = 2
A_HEAD_DIM = 64
A_WINDOW = 128
A_WIDTH = A_HEADS * A_HEAD_DIM
A_KV_WIDTH = A_KV_HEADS * A_HEAD_DIM
B_PATTERNS = ((128, 1), (512, 4), (2048, 16))
B_HEADS_PER_GROUP = 4
B_HEAD_DIM = 128
B_GROUP_WIDTH = B_HEADS_PER_GROUP * B_HEAD_DIM
B_N_GROUPS = len(B_PATTERNS)
B_QKV_WIDTH = B_N_GROUPS * B_GROUP_WIDTH
B_LSE_LANES = LANES // B_HEADS_PER_GROUP
DEPTH = 1
DN_ALPHA = float((2 * DEPTH) ** 0.25)

IN_SIZES = (A_WIDTH, A_KV_WIDTH, A_KV_WIDTH, A_WIDTH, B_QKV_WIDTH, B_QKV_WIDTH, B_QKV_WIDTH,
            B_GROUP_WIDTH, D_MODEL, D_MODEL)
IN_OFFSETS = tuple(int(o) for o in np.cumsum((0,) + IN_SIZES[:-1]))
(OFF_QA, OFF_KA, OFF_VA, OFF_GATE_A, OFF_QB, OFF_KB, OFF_VB, OFF_GATE_B, OFF_MG_A, OFF_MG_B) = IN_OFFSETS
D_IN = sum(IN_SIZES)

VMEM_LIMIT_BYTES = 60 * 1024 * 1024

EP_NONE, EP_ROPE64, EP_ROPE128, EP_SILU, EP_SIGMOID = range(5)

PROJ_TM = 1024
PROJ_TN = 2304

PK_MG_A = 0
PK_MG_B = PK_MG_A + D_MODEL
PK_VB = PK_MG_B + D_MODEL
PK_GATE_B = PK_VB + B_QKV_WIDTH
PK_QKB = PK_GATE_B + B_GROUP_WIDTH
PK_A = PK_QKB + 2 * B_QKV_WIDTH
PK_A_WIDTH = 2 * A_WIDTH + 2 * A_KV_WIDTH
A_COL_K = A_WIDTH
A_COL_V = A_COL_K + A_KV_WIDTH
A_COL_GATE = A_COL_V + A_KV_WIDTH
assert PK_A + PK_A_WIDTH == D_IN and PK_A % PK_A_WIDTH == 0 and PK_VB % B_GROUP_WIDTH == 0
assert PK_GATE_B % B_GROUP_WIDTH == 0 and PK_QKB % B_GROUP_WIDTH == 0 and D_IN % PROJ_TN == 0


def _compiler_params(semantics):
    return pltpu.CompilerParams(dimension_semantics=semantics, vmem_limit_bytes=VMEM_LIMIT_BYTES)


N_ROPE_TABLES = 4


def _rope_tables(pos, freq):
    ang = pos * freq
    c = jnp.cos(ang)
    s = jnp.sin(ang)
    lane = lax.broadcasted_iota(jnp.int32, c.shape, 1)

    def expand(t):
        r32 = pltpu.roll(t, 32, axis=1)
        r64 = pltpu.roll(t, 64, axis=1)
        r96 = pltpu.roll(t, 96, axis=1)
        t_a = jnp.where(lane < 32, t, jnp.where(lane < 64, r32, jnp.where(lane < 96, r64, r96)))
        t_b = jnp.where(lane < 64, r96, r32)
        return t_a, t_b

    c_a, c_b = expand(c)
    s_a, s_b = expand(s)
    return (c_a, jnp.where(_first_half_of_head64(lane), -s_a, s_a), c_b, jnp.where(lane < 64, -s_b, s_b))


def _first_half_of_head64(lane):
    return jnp.bitwise_and(lane, A_HEAD_DIM - 1) < A_HEAD_DIM // 2


def _rope_inputs(positions):
    s_len = positions.shape[1]
    half_a, half_b = A_HEAD_DIM // 2, B_HEAD_DIM // 2
    inv_a = ROPE_THETA ** (-jnp.arange(half_a, dtype=F32) / half_a)
    inv_b = ROPE_THETA ** (-jnp.arange(half_b, dtype=F32) / half_b)
    freq = jnp.concatenate([inv_a, inv_b, jnp.zeros((LANES - half_a - half_b,), F32)])[None, :]
    pos = jnp.broadcast_to(positions.reshape(s_len, 1).astype(F32), (s_len, LANES))
    return pos, freq


def _proj_kernel(tile_kinds, x_ref, w_ref, bias_ref, cos_a_ref, sin_a_ref, cos_b_ref, sin_b_ref, out_ref, xb_ref):
    n = pl.program_id(1)

    @pl.when(n == 0)
    def _():
        xb_ref[...] = x_ref[...].astype(BF16)

    def body(kinds):
        acc = jnp.dot(xb_ref[...], w_ref[...], preferred_element_type=F32)
        first_half = _first_half_of_head64(lax.broadcasted_iota(jnp.int32, (x_ref.shape[0], LANES), 1))
        for ci, kind in enumerate(kinds):
            cols = slice(ci * LANES, (ci + 1) * LANES)
            t = acc[:, cols]
            if kind == EP_ROPE64:
                rot = jnp.where(first_half, pltpu.roll(t, 96, axis=1), pltpu.roll(t, 32, axis=1))
                t = t * cos_a_ref[...] + rot * sin_a_ref[...]
            elif kind == EP_ROPE128:
                t = t * cos_b_ref[...] + pltpu.roll(t, 64, axis=1) * sin_b_ref[...]
            elif kind == EP_SILU:
                t = t * jax.nn.sigmoid(t)
            elif kind == EP_SIGMOID:
                t = jax.nn.sigmoid(t + bias_ref[:, cols])
            out_ref[:, cols] = t.astype(out_ref.dtype)

    branches = []
    for i, kinds in enumerate(tile_kinds):
        same = [br for br in branches if br[0] == kinds]
        if same:
            same[0][1].append(i)
        else:
            branches.append((kinds, [i]))
    for kinds, tiles in branches:
        cond = n == tiles[0]
        for i in tiles[1:]:
            cond = cond | (n == i)
        pl.when(cond)(functools.partial(body, kinds))


def _in_projection(x2d, w, bias, tables, chunk_kinds):
    s_len, k_dim = x2d.shape
    n_cols = w.shape[1]
    n_tiles = n_cols // PROJ_TN
    chunks_per_tile = PROJ_TN // LANES
    tile_kinds = tuple(tuple(chunk_kinds[t * chunks_per_tile:(t + 1) * chunks_per_tile]) for t in range(n_tiles))
    tm = min(PROJ_TM, s_len)
    row_spec = lambda width: pl.BlockSpec((tm, width), lambda m, n: (m, 0))
    return pl.pallas_call(
        functools.partial(_proj_kernel, tile_kinds),
        grid=(s_len // tm, n_tiles),
        in_specs=[row_spec(k_dim),
                  pl.BlockSpec((k_dim, PROJ_TN), lambda m, n: (0, n)),
                  pl.BlockSpec((1, PROJ_TN), lambda m, n: (0, n))] + [row_spec(LANES)] * N_ROPE_TABLES,
        out_specs=pl.BlockSpec((tm, PROJ_TN), lambda m, n: (m, n)),
        out_shape=jax.ShapeDtypeStruct((s_len, n_cols), BF16),
        scratch_shapes=[pltpu.VMEM((tm, k_dim), BF16)],
        compiler_params=_compiler_params(("arbitrary", "arbitrary")),
        name="in_projection",
    )(x2d, w, bias, *tables)


PACK_COLS = 256
PACK_BLOCKS_PER_STEP = 5
PACK_TN = PACK_BLOCKS_PER_STEP * PACK_COLS
assert D_IN % PACK_TN == 0


def _packed_layout():
    g = B_GROUP_WIDTH
    assert OFF_VA == OFF_KA + A_KV_WIDTH
    segments = [(OFF_MG_A, D_MODEL, [EP_SIGMOID]), (OFF_MG_B, D_MODEL, [EP_SIGMOID])]
    segments += [(OFF_VB, B_QKV_WIDTH, [EP_NONE]), (OFF_GATE_B, g, [EP_SILU])]
    for gi in range(B_N_GROUPS):
        segments += [(OFF_QB + gi * g, g, [EP_ROPE128]), (OFF_KB + gi * g, g, [EP_ROPE128])]
    segments += [(OFF_QA, A_WIDTH, [EP_ROPE64]),
                 (OFF_KA, 2 * A_KV_WIDTH, [EP_ROPE64] * (A_KV_WIDTH // LANES) + [EP_NONE] * (A_KV_WIDTH // LANES)),
                 (OFF_GATE_A, A_WIDTH, [EP_SILU])]
    perm, kinds = [], []
    for off, width, seg_kinds in segments:
        assert off % PACK_COLS == 0 and width % PACK_COLS == 0
        perm += [off // PACK_COLS + j for j in range(width // PACK_COLS)]
        kinds += seg_kinds * (width // LANES // len(seg_kinds))
    assert sorted(perm) == list(range(D_IN // PACK_COLS)) and len(kinds) == D_IN // LANES
    return np.asarray(perm, np.int32), kinds


def _pack_kernel(n_table_steps, perm_ref, pos_ref, freq_ref, *refs):
    del perm_ref
    w_refs = refs[:PACK_BLOCKS_PER_STEP]
    out_ref = refs[PACK_BLOCKS_PER_STEP]
    table_refs = refs[PACK_BLOCKS_PER_STEP + 1:]
    for j, w_ref in enumerate(w_refs):
        out_ref[:, j * PACK_COLS:(j + 1) * PACK_COLS] = w_ref[...].astype(out_ref.dtype)

    @pl.when(pl.program_id(0) < n_table_steps)
    def _():
        for t_ref, t in zip(table_refs, _rope_tables(pos_ref[...], freq_ref[...])):
            t_ref[...] = t


def _pack_weights_and_tables(w_in, b_gate, positions):
    perm, kinds = _packed_layout()
    k_dim = w_in.shape[0]
    n_steps = D_IN // PACK_TN
    pos, freq = _rope_inputs(positions)
    s_len = pos.shape[0]
    n_table_steps = n_steps - 1
    assert s_len % (8 * n_table_steps) == 0
    table_rows = s_len // n_table_steps
    tab_spec = pl.BlockSpec((table_rows, LANES), lambda i, perm_ref: (jnp.minimum(i, n_table_steps - 1), 0))
    src = lambda j: pl.BlockSpec((k_dim, PACK_COLS), lambda i, perm_ref: (0, perm_ref[i * PACK_BLOCKS_PER_STEP + j]))
    tab = jax.ShapeDtypeStruct((s_len, LANES), F32)
    w, *tables = pl.pallas_call(
        functools.partial(_pack_kernel, n_table_steps),
        grid_spec=pltpu.PrefetchScalarGridSpec(
            num_scalar_prefetch=1, grid=(n_steps,),
            in_specs=[tab_spec, pl.BlockSpec((1, LANES), lambda i, perm_ref: (0, 0))]
            + [src(j) for j in range(PACK_BLOCKS_PER_STEP)],
            out_specs=[pl.BlockSpec((k_dim, PACK_TN), lambda i, perm_ref: (0, i))] + [tab_spec] * N_ROPE_TABLES),
        out_shape=[jax.ShapeDtypeStruct((k_dim, D_IN), BF16)] + [tab] * N_ROPE_TABLES,
        compiler_params=_compiler_params(("arbitrary",)),
        name="pack_weights",
    )(jnp.asarray(perm), pos, freq, *([w_in] * PACK_BLOCKS_PER_STEP))
    bias = jnp.concatenate([b_gate[0], b_gate[1], jnp.zeros((D_IN - 2 * D_MODEL,), F32)])[None, :]
    return w, kinds, bias, tables


ATT_TOKENS = 512
MIXER_B_TOKENS = 1024
DEINTERLEAVE_MAX_STRIDE = 4


def _band_bias(max_dist, reps):
    q_idx = np.arange(BLOCK)[:, None] + BLOCK
    k_idx = np.arange(2 * BLOCK)[None, :]
    dist = q_idx - k_idx
    band = (dist >= 0) & (dist <= max_dist)
    first = band & (k_idx >= BLOCK)
    both = np.stack([first, band]).astype(bool)
    bias = np.where(both, 0.0, -np.inf).astype(np.float32)
    return jnp.asarray(np.tile(bias, (1, 1, reps)))


def _block_bias(bias_ref, step, b):
    if b == 0:
        return bias_ref[jnp.where(step == 0, 0, 1)]
    return bias_ref[1]


def _swap_lane_halves(t):
    return pltpu.roll(t, LANES // 2, axis=1)


def _mixer_a_kernel(a_ref, bias_ref, out_ref, kbuf_ref, vbuf_ref):
    step = pl.program_id(0)
    tq = a_ref.shape[0]
    n_blocks = tq // BLOCK
    n_pairs = A_HEADS // 2
    pairs_per_group = n_pairs // A_KV_HEADS

    @pl.when(step == 0)
    def _():
        kbuf_ref[0:BLOCK, :] = jnp.zeros((BLOCK, A_KV_WIDTH), BF16)
        vbuf_ref[0:BLOCK, :] = jnp.zeros((BLOCK, A_KV_WIDTH), BF16)

    kbuf_ref[BLOCK:, :] = a_ref[:, A_COL_K:A_COL_K + A_KV_WIDTH]
    vbuf_ref[BLOCK:, :] = a_ref[:, A_COL_V:A_COL_V + A_KV_WIDTH]
    lane2 = lax.broadcasted_iota(jnp.int32, (2 * BLOCK, LANES), 1)
    low2 = lane2 < LANES // 2
    sink_slot = lax.broadcasted_iota(jnp.int32, (2 * BLOCK, LANES), 0) == 0
    denom_cols = jnp.concatenate([jnp.where(low2, 1.0, 0.0), jnp.where(low2, 0.0, 1.0)], axis=0).astype(BF16)

    for b in range(n_blocks):
        bias_kind = jnp.where(step == 0, 0, 1) if b == 0 else 1
        rows = slice(b * BLOCK, (b + 1) * BLOCK)
        kk = jnp.where(sink_slot, 0.0, kbuf_ref[b * BLOCK:(b + 2) * BLOCK, :].astype(F32))
        vv = jnp.where(sink_slot, 0.0, vbuf_ref[b * BLOCK:(b + 2) * BLOCK, :].astype(F32))
        kk_sw = _swap_lane_halves(kk)
        vv_sw = _swap_lane_halves(vv)
        k2, v2 = [], []
        for g in range(A_KV_HEADS):
            if g == 0:
                k_top, k_bot = jnp.where(low2, kk, 0.0), jnp.where(low2, 0.0, kk_sw)
                v_top, v_bot = jnp.where(low2, vv, 0.0), jnp.where(low2, 0.0, vv_sw)
            else:
                k_top, k_bot = jnp.where(low2, kk_sw, 0.0), jnp.where(low2, 0.0, kk)
                v_top, v_bot = jnp.where(low2, vv_sw, 0.0), jnp.where(low2, 0.0, vv)
            k2.append(jnp.concatenate([k_top, k_bot], axis=0).astype(BF16))
            v2.append(jnp.concatenate([jnp.concatenate([v_top, v_bot], axis=0).astype(BF16), denom_cols], axis=1))
        for p in range(n_pairs):
            g = p // pairs_per_group
            cols = slice(p * LANES, (p + 1) * LANES)
            qp = a_ref[rows, cols] * jnp.asarray(A_HEAD_DIM ** -0.5, BF16)
            s = (lax.dot_general(qp, k2[g], (((1,), (1,)), ((), ())), preferred_element_type=F32)
                 + bias_ref[bias_kind, p])
            m0b = jnp.broadcast_to(jnp.max(s[:, :2 * BLOCK], axis=1, keepdims=True), (BLOCK, LANES))
            m1b = jnp.broadcast_to(jnp.max(s[:, 2 * BLOCK:], axis=1, keepdims=True), (BLOCK, LANES))
            shifts = (m0b, m0b, m1b, m1b)
            prob = jnp.concatenate([jnp.exp(s[:, j * LANES:(j + 1) * LANES] - shifts[j]) for j in range(4)],
                                   axis=1).astype(BF16)
            o2 = jnp.dot(prob, v2[g], preferred_element_type=F32)
            o = o2[:, :LANES] / o2[:, LANES:]
            gate = a_ref[rows, A_COL_GATE + p * LANES:A_COL_GATE + (p + 1) * LANES].astype(F32)
            out_ref[rows, cols] = (o * gate).astype(out_ref.dtype)

    kbuf_ref[0:BLOCK, :] = kbuf_ref[tq:tq + BLOCK, :]
    vbuf_ref[0:BLOCK, :] = vbuf_ref[tq:tq + BLOCK, :]


def _mixer_a(h, sinks):
    s_len = h.shape[0]
    tq = ATT_TOKENS
    n_pairs = A_HEADS // 2
    band = _band_bias(A_WINDOW - 1, 2)
    sink_pairs = sinks.astype(F32).reshape(n_pairs, 2)
    col = jnp.arange(4 * BLOCK)
    bias = jnp.broadcast_to(band[:, None], (2, n_pairs, BLOCK, 4 * BLOCK))
    bias = jnp.where(col == 0, sink_pairs[None, :, 0, None, None], bias)
    bias = jnp.where(col == 2 * BLOCK, sink_pairs[None, :, 1, None, None], bias)
    return pl.pallas_call(
        _mixer_a_kernel,
        grid=(s_len // tq,),
        in_specs=[
            pl.BlockSpec((tq, PK_A_WIDTH), lambda i: (i, PK_A // PK_A_WIDTH)),
            pl.BlockSpec(bias.shape, lambda i: (0, 0, 0, 0), pipeline_mode=pl.Buffered(1)),
        ],
        out_specs=pl.BlockSpec((tq, A_WIDTH), lambda i: (i, 0)),
        out_shape=jax.ShapeDtypeStruct((s_len, A_WIDTH), BF16),
        scratch_shapes=[pltpu.VMEM((tq + BLOCK, A_KV_WIDTH), BF16), pltpu.VMEM((tq + BLOCK, A_KV_WIDTH), BF16)],
        compiler_params=_compiler_params(("arbitrary",)),
        name="mixer_a",
    )(h, bias)


def _mixer_b_kernel(d, nb, q_ref, k_ref, v_ref, bias_ref, o_ref, lse_ref, qsub, ksub, vsub, *stage):
    step = pl.program_id(0)
    n_heads = B_HEADS_PER_GROUP
    span = BLOCK * d

    @pl.when(step == 0)
    def _():
        ksub[:, 0:BLOCK, :] = jnp.zeros((d, BLOCK, B_GROUP_WIDTH), BF16)
        vsub[:, 0:BLOCK, :] = jnp.zeros((d, BLOCK, B_GROUP_WIDTH), BF16)

    if d == 1:
        qsub[0] = q_ref[...]
        ksub[0, BLOCK:, :] = k_ref[...]
        vsub[0, BLOCK:, :] = v_ref[...]
    else:
        slab, mid, ostage, lstage = stage
        f1 = DEINTERLEAVE_MAX_STRIDE if d > DEINTERLEAVE_MAX_STRIDE else 1
        f2 = d // f1
        t_rows = nb * span
        for src, dst, row0 in ((q_ref, qsub, 0), (k_ref, ksub, BLOCK), (v_ref, vsub, BLOCK)):
            for c in range(n_heads):
                cols = slice(c * LANES, (c + 1) * LANES)
                sl = slab.at[c]
                sl[...] = src[:, cols].astype(F32)
                for r1 in range(f1):
                    if f1 > 1:
                        md = mid.at[c, r1]
                        md[...] = sl[pl.ds(r1, t_rows // f1, stride=f1), :]
                    else:
                        md = sl
                    for r2 in range(f2):
                        for b in range(nb):
                            piece = md[pl.ds(b * BLOCK * f2 + r2, BLOCK, stride=f2), :]
                            dst[r1 + f1 * r2, row0 + b * BLOCK:row0 + (b + 1) * BLOCK, cols] = piece.astype(BF16)

    ones = jnp.ones((2 * BLOCK, LANES), BF16)
    lane = lax.broadcasted_iota(jnp.int32, (BLOCK, LANES), 1)
    scale = B_HEAD_DIM ** -0.5
    exp2_scale = scale * float(np.log2(np.e))
    for r in range(d):
        for b in range(nb):
            bias = _block_bias(bias_ref, step, b)
            rows = slice(b * BLOCK, (b + 1) * BLOCK)
            lse_tile = jnp.zeros((BLOCK, LANES), F32)
            for h in range(n_heads):
                cols = slice(h * B_HEAD_DIM, (h + 1) * B_HEAD_DIM)
                kk = ksub[r, b * BLOCK:(b + 2) * BLOCK, cols]
                v2 = jnp.concatenate([vsub[r, b * BLOCK:(b + 2) * BLOCK, cols], ones], axis=1)
                s = lax.dot_general(qsub[r, rows, cols], kk, (((1,), (1,)), ((), ())),
                                    preferred_element_type=F32) + bias
                m = jnp.max(s, axis=1, keepdims=True)
                mb = jnp.broadcast_to(m, (BLOCK, LANES))
                prob = jnp.concatenate([jnp.exp2((s[:, :LANES] - mb) * exp2_scale),
                                        jnp.exp2((s[:, LANES:] - mb) * exp2_scale)], axis=1).astype(BF16)
                o2 = jnp.dot(prob, v2, preferred_element_type=F32)
                denom = o2[:, LANES:]
                o = o2[:, :LANES] / denom
                lse_h = mb * scale + jnp.log(denom)
                in_head = jnp.logical_and(lane >= h * B_LSE_LANES, lane < (h + 1) * B_LSE_LANES)
                lse_tile = jnp.where(in_head, lse_h, lse_tile)
                if d == 1:
                    o_ref[rows, cols] = o.astype(o_ref.dtype)
                else:
                    ostage[h, pl.ds(b * span + r, BLOCK, stride=d), :] = o
            if d == 1:
                lse_ref[rows, :] = lse_tile
            else:
                lstage[pl.ds(b * span + r, BLOCK, stride=d), :] = lse_tile

    if d > 1:
        for h in range(n_heads):
            o_ref[:, h * B_HEAD_DIM:(h + 1) * B_HEAD_DIM] = ostage[h].astype(o_ref.dtype)
        lse_ref[...] = lstage[...]
    ksub[:, 0:BLOCK, :] = ksub[:, nb * BLOCK:(nb + 1) * BLOCK, :]
    vsub[:, 0:BLOCK, :] = vsub[:, nb * BLOCK:(nb + 1) * BLOCK, :]


def _mixer_b_group(h, gi):
    s_len = h.shape[0]
    window, d = B_PATTERNS[gi]
    assert window // d == BLOCK
    nb = max(1, MIXER_B_TOKENS // (BLOCK * d))
    t_rows = nb * BLOCK * d
    f1 = DEINTERLEAVE_MAX_STRIDE if d > DEINTERLEAVE_MAX_STRIDE else 1
    assert d % f1 == 0 and d // f1 <= DEINTERLEAVE_MAX_STRIDE
    gw = B_GROUP_WIDTH
    col_q, col_v = (PK_QKB + gi * 2 * gw) // gw, (PK_VB + gi * gw) // gw
    bias = _band_bias(BLOCK, 1)
    blk = lambda c: pl.BlockSpec((t_rows, gw), lambda i: (i, c))
    scratch = [pltpu.VMEM((d, nb * BLOCK, gw), BF16), pltpu.VMEM((d, (nb + 1) * BLOCK, gw), BF16),
               pltpu.VMEM((d, (nb + 1) * BLOCK, gw), BF16)]
    if d > 1:
        mid_rows = t_rows // f1 if f1 > 1 else 8
        scratch += [pltpu.VMEM((B_HEADS_PER_GROUP, t_rows, LANES), F32),
                    pltpu.VMEM((B_HEADS_PER_GROUP, f1, mid_rows, LANES), F32),
                    pltpu.VMEM((B_HEADS_PER_GROUP, t_rows, LANES), F32), pltpu.VMEM((t_rows, LANES), F32)]
    return pl.pallas_call(
        functools.partial(_mixer_b_kernel, d, nb),
        grid=(s_len // t_rows,),
        in_specs=[blk(col_q), blk(col_q + 1), blk(col_v), pl.BlockSpec(bias.shape, lambda i: (0, 0, 0))],
        out_specs=[pl.BlockSpec((t_rows, gw), lambda i: (i, 0)), pl.BlockSpec((t_rows, LANES), lambda i: (i, 0))],
        out_shape=[jax.ShapeDtypeStruct((s_len, gw), BF16), jax.ShapeDtypeStruct((s_len, LANES), F32)],
        scratch_shapes=scratch,
        compiler_params=_compiler_params(("arbitrary",)),
        name=f"mixer_b_d{d}",
    )(h, h, h, bias)


TAIL_TM = 512


def _tail_kernel(x_ref, ya_ref, o0_ref, o1_ref, o2_ref, l0_ref, l1_ref, l2_ref, gate_b_ref, sig_a_ref, sig_b_ref,
                 w_pa_ref, w_pb_ref, w_out_ref, ln_g_ref, ln_b_ref, out_ref):
    tm = x_ref.shape[0]
    outs = [o0_ref, o1_ref, o2_ref]
    lses = [l0_ref[...], l1_ref[...], l2_ref[...]]
    m = jnp.maximum(jnp.maximum(lses[0], lses[1]), lses[2])
    es = [jnp.exp(l - m) for l in lses]
    inv = 1.0 / (es[0] + es[1] + es[2])
    wts = [e * inv for e in es]
    yb_cols = []
    for h in range(B_HEADS_PER_GROUP):
        cols = slice(h * B_HEAD_DIM, (h + 1) * B_HEAD_DIM)
        acc = None
        for gi in range(B_N_GROUPS):
            w_h = jnp.broadcast_to(wts[gi][:, h * B_LSE_LANES:h * B_LSE_LANES + 1], (tm, B_HEAD_DIM))
            term = w_h * outs[gi][:, cols].astype(F32)
            acc = term if acc is None else acc + term
        yb_cols.append((acc * gate_b_ref[:, cols].astype(F32)).astype(BF16))
    yb = jnp.concatenate(yb_cols, axis=1)
    y_a = jnp.dot(ya_ref[...], w_pa_ref[...], preferred_element_type=F32)
    y_b = jnp.dot(yb, w_pb_ref[...], preferred_element_type=F32)
    merged = sig_a_ref[...].astype(F32) * y_a + sig_b_ref[...].astype(F32) * y_b
    sub = jnp.dot(merged.astype(BF16), w_out_ref[...], preferred_element_type=F32)
    z = DN_ALPHA * x_ref[...] + sub
    inv_d = 1.0 / z.shape[-1]
    mu = jnp.sum(z, axis=-1, keepdims=True) * inv_d
    var = jnp.sum(z * z, axis=-1, keepdims=True) * inv_d - mu * mu
    out_ref[...] = ((z - mu) * lax.rsqrt(var + LN_EPS) * ln_g_ref[...] + ln_b_ref[...]).astype(out_ref.dtype)


def _tail(x2d, ya, o_groups, lse_groups, h, w_pa, w_pb, w_out, ln_g, ln_b):
    s_len = x2d.shape[0]
    tm = TAIL_TM
    row = lambda width, cblk=0: pl.BlockSpec((tm, width), lambda i: (i, cblk))
    full = lambda a: pl.BlockSpec(a.shape, lambda i: (0,) * a.ndim, pipeline_mode=pl.Buffered(1))
    gw = B_GROUP_WIDTH
    return pl.pallas_call(
        _tail_kernel,
        grid=(s_len // tm,),
        in_specs=[row(D_MODEL), row(A_WIDTH), row(gw), row(gw), row(gw), row(LANES), row(LANES), row(LANES),
                  row(gw, PK_GATE_B // gw), row(D_MODEL, PK_MG_A // D_MODEL), row(D_MODEL, PK_MG_B // D_MODEL),
                  full(w_pa), full(w_pb), full(w_out), full(ln_g), full(ln_b)],
        out_specs=row(D_MODEL),
        out_shape=jax.ShapeDtypeStruct((s_len, D_MODEL), x2d.dtype),
        compiler_params=_compiler_params(("arbitrary",)),
        name="tail",
    )(x2d, ya, *o_groups, *lse_groups, h, h, h, w_pa, w_pb, w_out, ln_g, ln_b)


def _hybrid_layer(x, positions, w_in, b_gate, sinks, w_pa, w_pb, w_out, ln_g, ln_b):
    bn, s_len, d_model = x.shape
    assert bn == 1 and d_model == D_MODEL
    assert s_len % (BLOCK * B_PATTERNS[-1][1]) == 0 and s_len % PROJ_TM == 0
    x2d = x.reshape(s_len, d_model)
    w, kinds, bias, tables = _pack_weights_and_tables(w_in, b_gate, positions)
    h = _in_projection(x2d, w, bias, tables, kinds)
    o_groups, lse_groups = [], []
    for gi in range(B_N_GROUPS):
        o, lse = _mixer_b_group(h, gi)
        o_groups.append(o)
        lse_groups.append(lse)
    ya = _mixer_a(h, sinks)
    out = _tail(x2d, ya, o_groups, lse_groups, h, w_pa.astype(BF16), w_pb.astype(BF16), w_out.astype(BF16),
                ln_g.reshape(1, d_model).astype(F32), ln_b.reshape(1, d_model).astype(F32))
    return out.reshape(bn, s_len, d_model)


def kernel(x, positions, w_in, b_gate, sinks, w_pa, w_pb, w_out, ln_g, ln_b):
    for layer in range(w_in.shape[0]):
        x = _hybrid_layer(x, positions, w_in[layer], b_gate[layer], sinks[layer], w_pa[layer], w_pb[layer],
                          w_out[layer], ln_g[layer], ln_b[layer])
    return x
```

```python
import functools

import jax
import jax.numpy as jnp
import numpy as np
from jax import lax
from jax.experimental import pallas as pl
from jax.experimental.pallas import tpu as pltpu

F32 = jnp.float32
BF16 = jnp.bfloat16

D_MODEL = 2048
ROPE_THETA = 10000.0
LN_EPS = 1e-5
BLOCK = 128
LANES = 128
A_HEADS = 16
A_KV_HEADS = 2
A_HEAD_DIM = 64
A_WINDOW = 128
A_WIDTH = A_HEADS * A_HEAD_DIM
A_KV_WIDTH = A_KV_HEADS * A_HEAD_DIM
B_PATTERNS = ((128, 1), (512, 4), (2048, 16))
B_HEADS_PER_GROUP = 4
B_HEAD_DIM = 128
B_GROUP_WIDTH = B_HEADS_PER_GROUP * B_HEAD_DIM
B_N_GROUPS = len(B_PATTERNS)
B_QKV_WIDTH = B_N_GROUPS * B_GROUP_WIDTH
B_LSE_LANES = LANES // B_HEADS_PER_GROUP
DEPTH = 1
DN_ALPHA = float((2 * DEPTH) ** 0.25)

IN_SIZES = (A_WIDTH, A_KV_WIDTH, A_KV_WIDTH, A_WIDTH, B_QKV_WIDTH, B_QKV_WIDTH, B_QKV_WIDTH,
            B_GROUP_WIDTH, D_MODEL, D_MODEL)
IN_OFFSETS = tuple(int(o) for o in np.cumsum((0,) + IN_SIZES[:-1]))
(OFF_QA, OFF_KA, OFF_VA, OFF_GATE_A, OFF_QB, OFF_KB, OFF_VB, OFF_GATE_B, OFF_MG_A, OFF_MG_B) = IN_OFFSETS
D_IN = sum(IN_SIZES)

VMEM_LIMIT_BYTES = 60 * 1024 * 1024

EP_NONE, EP_ROPE64, EP_ROPE128, EP_SILU, EP_SIGMOID = range(5)

PROJ_TM = 1024
PROJ_TN = 2304

PK_MG_A = 0
PK_MG_B = PK_MG_A + D_MODEL
PK_VB = PK_MG_B + D_MODEL
PK_GATE_B = PK_VB + B_QKV_WIDTH
PK_QKB = PK_GATE_B + B_GROUP_WIDTH
PK_A = PK_QKB + 2 * B_QKV_WIDTH
PK_A_WIDTH = 2 * A_WIDTH + 2 * A_KV_WIDTH
A_COL_K = A_WIDTH
A_COL_V = A_COL_K + A_KV_WIDTH
A_COL_GATE = A_COL_V + A_KV_WIDTH
assert PK_A + PK_A_WIDTH == D_IN and PK_A % PK_A_WIDTH == 0 and PK_VB % B_GROUP_WIDTH == 0
assert PK_GATE_B % B_GROUP_WIDTH == 0 and PK_QKB % B_GROUP_WIDTH == 0 and D_IN % PROJ_TN == 0


def _compiler_params(semantics):
    return pltpu.CompilerParams(dimension_semantics=semantics, vmem_limit_bytes=VMEM_LIMIT_BYTES)


N_ROPE_TABLES = 4


def _rope_tables(pos, freq, select_ref, sign_ref):
    half = pos.shape[0] // 2
    lane = lax.broadcasted_iota(jnp.int32, (half, LANES), 1)
    ang = jnp.where(lane < LANES // 2, pos[:half], pos[half:]) * freq

    def to_head_layout(t):
        hi = t.astype(BF16)
        rest = t - hi.astype(F32)
        mid = rest.astype(BF16)
        low = (rest - mid.astype(F32)).astype(BF16)
        r = jnp.dot(jnp.concatenate([hi, mid, low], axis=1), select_ref[...], preferred_element_type=F32)
        return [jnp.concatenate([r[:, (2 * layout) * LANES:(2 * layout + 1) * LANES],
                                 r[:, (2 * layout + 1) * LANES:(2 * layout + 2) * LANES]], axis=0)
                for layout in range(2)]

    cos_a, cos_b = to_head_layout(jnp.cos(ang))
    sin_a, sin_b = to_head_layout(jnp.sin(ang))
    return [t * sign_ref[ti:ti + 1, :] for ti, t in enumerate((cos_a, sin_a, cos_b, sin_b))]


def _first_half_of_head64(lane):
    return jnp.bitwise_and(lane, A_HEAD_DIM - 1) < A_HEAD_DIM // 2


def _rope_inputs(positions):
    s_len = positions.shape[1]
    half_b = B_HEAD_DIM // 2
    assert A_HEAD_DIM * 2 == B_HEAD_DIM and 2 * half_b == LANES
    inv_b = ROPE_THETA ** (-jnp.arange(half_b, dtype=F32) / half_b)
    freq = jnp.concatenate([inv_b, inv_b])[None, :]
    pos = jnp.broadcast_to(positions.reshape(s_len, 1).astype(F32), (s_len, LANES))
    lane = np.arange(LANES)
    src_lane = np.stack([2 * (lane % (A_HEAD_DIM // 2)), lane % half_b])
    select = np.zeros((LANES, 4 * LANES), np.float32)
    for layout in range(2):
        for hf in range(2):
            select[hf * half_b + src_lane[layout], (2 * layout + hf) * LANES + lane] = 1.0
    select = np.concatenate([select] * 3, axis=0)
    sign = np.ones((N_ROPE_TABLES, LANES), np.float32)
    sign[1, lane % A_HEAD_DIM < A_HEAD_DIM // 2] = -1.0
    sign[3, lane < half_b] = -1.0
    return pos, freq, jnp.asarray(select, BF16), jnp.asarray(sign)


def _proj_kernel(tile_kinds, x_ref, w_ref, bias_ref, cos_a_ref, sin_a_ref, cos_b_ref, sin_b_ref, out_ref, xb_ref):
    n = pl.program_id(1)

    @pl.when(n == 0)
    def _():
        xb_ref[...] = x_ref[...].astype(BF16)

    def body(kinds):
        acc = jnp.dot(xb_ref[...], w_ref[...], preferred_element_type=F32)
        first_half = _first_half_of_head64(lax.broadcasted_iota(jnp.int32, (x_ref.shape[0], LANES), 1))
        for ci, kind in enumerate(kinds):
            cols = slice(ci * LANES, (ci + 1) * LANES)
            t = acc[:, cols]
            if kind == EP_ROPE64:
                rot = jnp.where(first_half, pltpu.roll(t, 96, axis=1), pltpu.roll(t, 32, axis=1))
                t = t * cos_a_ref[...] + rot * sin_a_ref[...]
            elif kind == EP_ROPE128:
                t = t * cos_b_ref[...] + pltpu.roll(t, 64, axis=1) * sin_b_ref[...]
            elif kind == EP_SILU:
                t = t * jax.nn.sigmoid(t)
            elif kind == EP_SIGMOID:
                t = jax.nn.sigmoid(t + bias_ref[:, cols])
            out_ref[:, cols] = t.astype(out_ref.dtype)

    branches = []
    for i, kinds in enumerate(tile_kinds):
        same = [br for br in branches if br[0] == kinds]
        if same:
            same[0][1].append(i)
        else:
            branches.append((kinds, [i]))
    for kinds, tiles in branches:
        cond = n == tiles[0]
        for i in tiles[1:]:
            cond = cond | (n == i)
        pl.when(cond)(functools.partial(body, kinds))


def _in_projection(x2d, w, bias, tables, chunk_kinds):
    s_len, k_dim = x2d.shape
    n_cols = w.shape[1]
    n_tiles = n_cols // PROJ_TN
    chunks_per_tile = PROJ_TN // LANES
    tile_kinds = tuple(tuple(chunk_kinds[t * chunks_per_tile:(t + 1) * chunks_per_tile]) for t in range(n_tiles))
    tm = min(PROJ_TM, s_len)
    row_spec = lambda width: pl.BlockSpec((tm, width), lambda m, n: (m, 0))
    return pl.pallas_call(
        functools.partial(_proj_kernel, tile_kinds),
        grid=(s_len // tm, n_tiles),
        in_specs=[row_spec(k_dim),
                  pl.BlockSpec((k_dim, PROJ_TN), lambda m, n: (0, n)),
                  pl.BlockSpec((1, PROJ_TN), lambda m, n: (0, n))] + [row_spec(LANES)] * N_ROPE_TABLES,
        out_specs=pl.BlockSpec((tm, PROJ_TN), lambda m, n: (m, n)),
        out_shape=jax.ShapeDtypeStruct((s_len, n_cols), BF16),
        scratch_shapes=[pltpu.VMEM((tm, k_dim), BF16)],
        compiler_params=_compiler_params(("arbitrary", "arbitrary")),
        name="in_projection",
    )(x2d, w, bias, *tables)


PACK_COLS = 256
PACK_BLOCKS_PER_STEP = 5
PACK_TN = PACK_BLOCKS_PER_STEP * PACK_COLS
assert D_IN % PACK_TN == 0


def _packed_layout():
    g = B_GROUP_WIDTH
    assert OFF_VA == OFF_KA + A_KV_WIDTH
    segments = [(OFF_MG_A, D_MODEL, [EP_SIGMOID]), (OFF_MG_B, D_MODEL, [EP_SIGMOID])]
    segments += [(OFF_VB, B_QKV_WIDTH, [EP_NONE]), (OFF_GATE_B, g, [EP_SILU])]
    for gi in range(B_N_GROUPS):
        segments += [(OFF_QB + gi * g, g, [EP_ROPE128]), (OFF_KB + gi * g, g, [EP_ROPE128])]
    segments += [(OFF_QA, A_WIDTH, [EP_ROPE64]),
                 (OFF_KA, 2 * A_KV_WIDTH, [EP_ROPE64] * (A_KV_WIDTH // LANES) + [EP_NONE] * (A_KV_WIDTH // LANES)),
                 (OFF_GATE_A, A_WIDTH, [EP_SILU])]
    perm, kinds = [], []
    for off, width, seg_kinds in segments:
        assert off % PACK_COLS == 0 and width % PACK_COLS == 0
        perm += [off // PACK_COLS + j for j in range(width // PACK_COLS)]
        kinds += seg_kinds * (width // LANES // len(seg_kinds))
    assert sorted(perm) == list(range(D_IN // PACK_COLS)) and len(kinds) == D_IN // LANES
    return np.asarray(perm, np.int32), kinds


def _pack_kernel(n_table_steps, perm_ref, pos_ref, freq_ref, select_ref, sign_ref, *refs):
    del perm_ref
    w_refs = refs[:PACK_BLOCKS_PER_STEP]
    out_ref = refs[PACK_BLOCKS_PER_STEP]
    table_refs = refs[PACK_BLOCKS_PER_STEP + 1:]
    for j, w_ref in enumerate(w_refs):
        out_ref[:, j * PACK_COLS:(j + 1) * PACK_COLS] = w_ref[...].astype(out_ref.dtype)

    @pl.when(pl.program_id(0) < n_table_steps)
    def _():
        for t_ref, t in zip(table_refs, _rope_tables(pos_ref[...], freq_ref[...], select_ref, sign_ref)):
            t_ref[...] = t


def _pack_weights_and_tables(w_in, b_gate, positions):
    perm, kinds = _packed_layout()
    k_dim = w_in.shape[0]
    n_steps = D_IN // PACK_TN
    pos, freq, select, sign = _rope_inputs(positions)
    s_len = pos.shape[0]
    n_table_steps = n_steps - 1
    assert s_len % (16 * n_table_steps) == 0
    const = lambda a: pl.BlockSpec(a.shape, lambda i, perm_ref: (0,) * a.ndim)
    table_rows = s_len // n_table_steps
    tab_spec = pl.BlockSpec((table_rows, LANES), lambda i, perm_ref: (jnp.minimum(i, n_table_steps - 1), 0))
    src = lambda j: pl.BlockSpec((k_dim, PACK_COLS), lambda i, perm_ref: (0, perm_ref[i * PACK_BLOCKS_PER_STEP + j]))
    tab = jax.ShapeDtypeStruct((s_len, LANES), F32)
    w, *tables = pl.pallas_call(
        functools.partial(_pack_kernel, n_table_steps),
        grid_spec=pltpu.PrefetchScalarGridSpec(
            num_scalar_prefetch=1, grid=(n_steps,),
            in_specs=[tab_spec, const(freq), const(select), const(sign)]
            + [src(j) for j in range(PACK_BLOCKS_PER_STEP)],
            out_specs=[pl.BlockSpec((k_dim, PACK_TN), lambda i, perm_ref: (0, i))] + [tab_spec] * N_ROPE_TABLES),
        out_shape=[jax.ShapeDtypeStruct((k_dim, D_IN), BF16)] + [tab] * N_ROPE_TABLES,
        compiler_params=_compiler_params(("arbitrary",)),
        name="pack_weights",
    )(jnp.asarray(perm), pos, freq, select, sign, *([w_in] * PACK_BLOCKS_PER_STEP))
    bias = jnp.concatenate([b_gate[0], b_gate[1], jnp.zeros((D_IN - 2 * D_MODEL,), F32)])[None, :]
    return w, kinds, bias, tables


ATT_TOKENS = 512
MIXER_B_TOKENS = 1024
DEINTERLEAVE_MAX_STRIDE = 4


def _band_bias(max_dist, reps):
    q_idx = np.arange(BLOCK)[:, None] + BLOCK
    k_idx = np.arange(2 * BLOCK)[None, :]
    dist = q_idx - k_idx
    band = (dist >= 0) & (dist <= max_dist)
    first = band & (k_idx >= BLOCK)
    both = np.stack([first, band]).astype(bool)
    bias = np.where(both, 0.0, -np.inf).astype(np.float32)
    return jnp.asarray(np.tile(bias, (1, 1, reps)))


def _block_bias(bias_ref, step, b):
    if b == 0:
        return bias_ref[jnp.where(step == 0, 0, 1)]
    return bias_ref[1]


def _swap_lane_halves(t):
    return pltpu.roll(t, LANES // 2, axis=1)


def _mixer_a_kernel(a_ref, bias_ref, out_ref, kbuf_ref, vbuf_ref):
    step = pl.program_id(0)
    tq = a_ref.shape[0]
    n_blocks = tq // BLOCK
    n_pairs = A_HEADS // 2
    pairs_per_group = n_pairs // A_KV_HEADS

    @pl.when(step == 0)
    def _():
        kbuf_ref[0:BLOCK, :] = jnp.zeros((BLOCK, A_KV_WIDTH), BF16)
        vbuf_ref[0:BLOCK, :] = jnp.zeros((BLOCK, A_KV_WIDTH), BF16)

    kbuf_ref[BLOCK:, :] = a_ref[:, A_COL_K:A_COL_K + A_KV_WIDTH]
    vbuf_ref[BLOCK:, :] = a_ref[:, A_COL_V:A_COL_V + A_KV_WIDTH]
    lane2 = lax.broadcasted_iota(jnp.int32, (2 * BLOCK, LANES), 1)
    low2 = lane2 < LANES // 2
    sink_slot = lax.broadcasted_iota(jnp.int32, (2 * BLOCK, LANES), 0) == 0
    denom_cols = jnp.concatenate([jnp.where(low2, 1.0, 0.0), jnp.where(low2, 0.0, 1.0)], axis=0).astype(BF16)

    for b in range(n_blocks):
        bias_kind = jnp.where(step == 0, 0, 1) if b == 0 else 1
        rows = slice(b * BLOCK, (b + 1) * BLOCK)
        kk = jnp.where(sink_slot, 0.0, kbuf_ref[b * BLOCK:(b + 2) * BLOCK, :].astype(F32))
        vv = jnp.where(sink_slot, 0.0, vbuf_ref[b * BLOCK:(b + 2) * BLOCK, :].astype(F32))
        kk_sw = _swap_lane_halves(kk)
        vv_sw = _swap_lane_halves(vv)
        k2, v2 = [], []
        for g in range(A_KV_HEADS):
            if g == 0:
                k_top, k_bot = jnp.where(low2, kk, 0.0), jnp.where(low2, 0.0, kk_sw)
                v_top, v_bot = jnp.where(low2, vv, 0.0), jnp.where(low2, 0.0, vv_sw)
            else:
                k_top, k_bot = jnp.where(low2, kk_sw, 0.0), jnp.where(low2, 0.0, kk)
                v_top, v_bot = jnp.where(low2, vv_sw, 0.0), jnp.where(low2, 0.0, vv)
            k2.append(jnp.concatenate([k_top, k_bot], axis=0).astype(BF16))
            v2.append(jnp.concatenate([jnp.concatenate([v_top, v_bot], axis=0).astype(BF16), denom_cols], axis=1))
        for p in range(n_pairs):
            g = p // pairs_per_group
            cols = slice(p * LANES, (p + 1) * LANES)
            qp = a_ref[rows, cols] * jnp.asarray(A_HEAD_DIM ** -0.5, BF16)
            s = (lax.dot_general(qp, k2[g], (((1,), (1,)), ((), ())), preferred_element_type=F32)
                 + bias_ref[bias_kind, p])
            m0b = jnp.broadcast_to(jnp.max(s[:, :2 * BLOCK], axis=1, keepdims=True), (BLOCK, LANES))
            m1b = jnp.broadcast_to(jnp.max(s[:, 2 * BLOCK:], axis=1, keepdims=True), (BLOCK, LANES))
            shifts = (m0b, m0b, m1b, m1b)
            prob = jnp.concatenate([jnp.exp(s[:, j * LANES:(j + 1) * LANES] - shifts[j]) for j in range(4)],
                                   axis=1).astype(BF16)
            o2 = jnp.dot(prob, v2[g], preferred_element_type=F32)
            o = o2[:, :LANES] / o2[:, LANES:]
            gate = a_ref[rows, A_COL_GATE + p * LANES:A_COL_GATE + (p + 1) * LANES].astype(F32)
            out_ref[rows, cols] = (o * gate).astype(out_ref.dtype)

    kbuf_ref[0:BLOCK, :] = kbuf_ref[tq:tq + BLOCK, :]
    vbuf_ref[0:BLOCK, :] = vbuf_ref[tq:tq + BLOCK, :]


def _mixer_a(h, sinks):
    s_len = h.shape[0]
    tq = ATT_TOKENS
    n_pairs = A_HEADS // 2
    band = _band_bias(A_WINDOW - 1, 2)
    sink_pairs = sinks.astype(F32).reshape(n_pairs, 2)
    col = jnp.arange(4 * BLOCK)
    bias = jnp.broadcast_to(band[:, None], (2, n_pairs, BLOCK, 4 * BLOCK))
    bias = jnp.where(col == 0, sink_pairs[None, :, 0, None, None], bias)
    bias = jnp.where(col == 2 * BLOCK, sink_pairs[None, :, 1, None, None], bias)
    return pl.pallas_call(
        _mixer_a_kernel,
        grid=(s_len // tq,),
        in_specs=[
            pl.BlockSpec((tq, PK_A_WIDTH), lambda i: (i, PK_A // PK_A_WIDTH)),
            pl.BlockSpec(bias.shape, lambda i: (0, 0, 0, 0), pipeline_mode=pl.Buffered(1)),
        ],
        out_specs=pl.BlockSpec((tq, A_WIDTH), lambda i: (i, 0)),
        out_shape=jax.ShapeDtypeStruct((s_len, A_WIDTH), BF16),
        scratch_shapes=[pltpu.VMEM((tq + BLOCK, A_KV_WIDTH), BF16), pltpu.VMEM((tq + BLOCK, A_KV_WIDTH), BF16)],
        compiler_params=_compiler_params(("arbitrary",)),
        name="mixer_a",
    )(h, bias)


def _mixer_b_kernel(d, nb, q_ref, k_ref, v_ref, bias_ref, o_ref, lse_ref, qsub, ksub, vsub, *stage):
    step = pl.program_id(0)
    n_heads = B_HEADS_PER_GROUP
    span = BLOCK * d

    @pl.when(step == 0)
    def _():
        ksub[:, 0:BLOCK, :] = jnp.zeros((d, BLOCK, B_GROUP_WIDTH), BF16)
        vsub[:, 0:BLOCK, :] = jnp.zeros((d, BLOCK, B_GROUP_WIDTH), BF16)

    if d == 1:
        qsub[0] = q_ref[...]
        ksub[0, BLOCK:, :] = k_ref[...]
        vsub[0, BLOCK:, :] = v_ref[...]
    else:
        slab, mid, ostage, lstage = stage
        f1 = DEINTERLEAVE_MAX_STRIDE if d > DEINTERLEAVE_MAX_STRIDE else 1
        f2 = d // f1
        t_rows = nb * span
        for src, dst, row0 in ((q_ref, qsub, 0), (k_ref, ksub, BLOCK), (v_ref, vsub, BLOCK)):
            for c in range(n_heads):
                cols = slice(c * LANES, (c + 1) * LANES)
                sl = slab.at[c]
                sl[...] = src[:, cols].astype(F32)
                for r1 in range(f1):
                    if f1 > 1:
                        md = mid.at[c, r1]
                        md[...] = sl[pl.ds(r1, t_rows // f1, stride=f1), :]
                    else:
                        md = sl
                    for r2 in range(f2):
                        for b in range(nb):
                            piece = md[pl.ds(b * BLOCK * f2 + r2, BLOCK, stride=f2), :]
                            dst[r1 + f1 * r2, row0 + b * BLOCK:row0 + (b + 1) * BLOCK, cols] = piece.astype(BF16)

    ones = jnp.ones((2 * BLOCK, LANES), BF16)
    lane = lax.broadcasted_iota(jnp.int32, (BLOCK, LANES), 1)
    scale = B_HEAD_DIM ** -0.5
    exp2_scale = scale * float(np.log2(np.e))
    for r in range(d):
        for b in range(nb):
            bias = _block_bias(bias_ref, step, b)
            rows = slice(b * BLOCK, (b + 1) * BLOCK)
            lse_tile = jnp.zeros((BLOCK, LANES), F32)
            for h in range(n_heads):
                cols = slice(h * B_HEAD_DIM, (h + 1) * B_HEAD_DIM)
                kk = ksub[r, b * BLOCK:(b + 2) * BLOCK, cols]
                v2 = jnp.concatenate([vsub[r, b * BLOCK:(b + 2) * BLOCK, cols], ones], axis=1)
                s = lax.dot_general(qsub[r, rows, cols], kk, (((1,), (1,)), ((), ())),
                                    preferred_element_type=F32) + bias
                m = jnp.max(s, axis=1, keepdims=True)
                mb = jnp.broadcast_to(m, (BLOCK, LANES))
                prob = jnp.concatenate([jnp.exp2((s[:, :LANES] - mb) * exp2_scale),
                                        jnp.exp2((s[:, LANES:] - mb) * exp2_scale)], axis=1).astype(BF16)
                o2 = jnp.dot(prob, v2, preferred_element_type=F32)
                denom = o2[:, LANES:]
                o = o2[:, :LANES] / denom
                lse_h = mb * scale + jnp.log(denom)
                in_head = jnp.logical_and(lane >= h * B_LSE_LANES, lane < (h + 1) * B_LSE_LANES)
                lse_tile = jnp.where(in_head, lse_h, lse_tile)
                if d == 1:
                    o_ref[rows, cols] = o.astype(o_ref.dtype)
                else:
                    ostage[h, pl.ds(b * span + r, BLOCK, stride=d), :] = o
            if d == 1:
                lse_ref[rows, :] = lse_tile
            else:
                lstage[pl.ds(b * span + r, BLOCK, stride=d), :] = lse_tile

    if d > 1:
        for h in range(n_heads):
            o_ref[:, h * B_HEAD_DIM:(h + 1) * B_HEAD_DIM] = ostage[h].astype(o_ref.dtype)
        lse_ref[...] = lstage[...]
    ksub[:, 0:BLOCK, :] = ksub[:, nb * BLOCK:(nb + 1) * BLOCK, :]
    vsub[:, 0:BLOCK, :] = vsub[:, nb * BLOCK:(nb + 1) * BLOCK, :]


def _mixer_b_group(h, gi):
    s_len = h.shape[0]
    window, d = B_PATTERNS[gi]
    assert window // d == BLOCK
    nb = max(1, MIXER_B_TOKENS // (BLOCK * d))
    t_rows = nb * BLOCK * d
    f1 = DEINTERLEAVE_MAX_STRIDE if d > DEINTERLEAVE_MAX_STRIDE else 1
    assert d % f1 == 0 and d // f1 <= DEINTERLEAVE_MAX_STRIDE
    gw = B_GROUP_WIDTH
    col_q, col_v = (PK_QKB + gi * 2 * gw) // gw, (PK_VB + gi * gw) // gw
    bias = _band_bias(BLOCK, 1)
    blk = lambda c: pl.BlockSpec((t_rows, gw), lambda i: (i, c))
    scratch = [pltpu.VMEM((d, nb * BLOCK, gw), BF16), pltpu.VMEM((d, (nb + 1) * BLOCK, gw), BF16),
               pltpu.VMEM((d, (nb + 1) * BLOCK, gw), BF16)]
    if d > 1:
        mid_rows = t_rows // f1 if f1 > 1 else 8
        scratch += [pltpu.VMEM((B_HEADS_PER_GROUP, t_rows, LANES), F32),
                    pltpu.VMEM((B_HEADS_PER_GROUP, f1, mid_rows, LANES), F32),
                    pltpu.VMEM((B_HEADS_PER_GROUP, t_rows, LANES), F32), pltpu.VMEM((t_rows, LANES), F32)]
    return pl.pallas_call(
        functools.partial(_mixer_b_kernel, d, nb),
        grid=(s_len // t_rows,),
        in_specs=[blk(col_q), blk(col_q + 1), blk(col_v), pl.BlockSpec(bias.shape, lambda i: (0, 0, 0))],
        out_specs=[pl.BlockSpec((t_rows, gw), lambda i: (i, 0)), pl.BlockSpec((t_rows, LANES), lambda i: (i, 0))],
        out_shape=[jax.ShapeDtypeStruct((s_len, gw), BF16), jax.ShapeDtypeStruct((s_len, LANES), F32)],
        scratch_shapes=scratch,
        compiler_params=_compiler_params(("arbitrary",)),
        name=f"mixer_b_d{d}",
    )(h, h, h, bias)


TAIL_TM = 512


def _tail_kernel(x_ref, ya_ref, o0_ref, o1_ref, o2_ref, l0_ref, l1_ref, l2_ref, gate_b_ref, sig_a_ref, sig_b_ref,
                 w_pa_ref, w_pb_ref, w_out_ref, ln_g_ref, ln_b_ref, out_ref):
    tm = x_ref.shape[0]
    outs = [o0_ref, o1_ref, o2_ref]
    lses = [l0_ref[...], l1_ref[...], l2_ref[...]]
    m = jnp.maximum(jnp.maximum(lses[0], lses[1]), lses[2])
    es = [jnp.exp(l - m) for l in lses]
    inv = 1.0 / (es[0] + es[1] + es[2])
    wts = [e * inv for e in es]
    yb_cols = []
    for h in range(B_HEADS_PER_GROUP):
        cols = slice(h * B_HEAD_DIM, (h + 1) * B_HEAD_DIM)
        acc = None
        for gi in range(B_N_GROUPS):
            w_h = jnp.broadcast_to(wts[gi][:, h * B_LSE_LANES:h * B_LSE_LANES + 1], (tm, B_HEAD_DIM))
            term = w_h * outs[gi][:, cols].astype(F32)
            acc = term if acc is None else acc + term
        yb_cols.append((acc * gate_b_ref[:, cols].astype(F32)).astype(BF16))
    yb = jnp.concatenate(yb_cols, axis=1)
    y_a = jnp.dot(ya_ref[...], w_pa_ref[...], preferred_element_type=F32)
    y_b = jnp.dot(yb, w_pb_ref[...], preferred_element_type=F32)
    merged = sig_a_ref[...].astype(F32) * y_a + sig_b_ref[...].astype(F32) * y_b
    sub = jnp.dot(merged.astype(BF16), w_out_ref[...], preferred_element_type=F32)
    z = DN_ALPHA * x_ref[...] + sub
    inv_d = 1.0 / z.shape[-1]
    mu = jnp.sum(z, axis=-1, keepdims=True) * inv_d
    var = jnp.sum(z * z, axis=-1, keepdims=True) * inv_d - mu * mu
    out_ref[...] = ((z - mu) * lax.rsqrt(var + LN_EPS) * ln_g_ref[...] + ln_b_ref[...]).astype(out_ref.dtype)


def _tail(x2d, ya, o_groups, lse_groups, h, w_pa, w_pb, w_out, ln_g, ln_b):
    s_len = x2d.shape[0]
    tm = TAIL_TM
    row = lambda width, cblk=0: pl.BlockSpec((tm, width), lambda i: (i, cblk))
    full = lambda a: pl.BlockSpec(a.shape, lambda i: (0,) * a.ndim, pipeline_mode=pl.Buffered(1))
    gw = B_GROUP_WIDTH
    return pl.pallas_call(
        _tail_kernel,
        grid=(s_len // tm,),
        in_specs=[row(D_MODEL), row(A_WIDTH), row(gw), row(gw), row(gw), row(LANES), row(LANES), row(LANES),
                  row(gw, PK_GATE_B // gw), row(D_MODEL, PK_MG_A // D_MODEL), row(D_MODEL, PK_MG_B // D_MODEL),
                  full(w_pa), full(w_pb), full(w_out), full(ln_g), full(ln_b)],
        out_specs=row(D_MODEL),
        out_shape=jax.ShapeDtypeStruct((s_len, D_MODEL), x2d.dtype),
        compiler_params=_compiler_params(("arbitrary",)),
        name="tail",
    )(x2d, ya, *o_groups, *lse_groups, h, h, h, w_pa, w_pb, w_out, ln_g, ln_b)


def _hybrid_layer(x, positions, w_in, b_gate, sinks, w_pa, w_pb, w_out, ln_g, ln_b):
    bn, s_len, d_model = x.shape
    assert bn == 1 and d_model == D_MODEL
    assert s_len % (BLOCK * B_PATTERNS[-1][1]) == 0 and s_len % PROJ_TM == 0
    x2d = x.reshape(s_len, d_model)
    w, kinds, bias, tables = _pack_weights_and_tables(w_in, b_gate, positions)
    h = _in_projection(x2d, w, bias, tables, kinds)
    o_groups, lse_groups = [], []
    for gi in range(B_N_GROUPS):
        o, lse = _mixer_b_group(h, gi)
        o_groups.append(o)
        lse_groups.append(lse)
    ya = _mixer_a(h, sinks)
    out = _tail(x2d, ya, o_groups, lse_groups, h, w_pa.astype(BF16), w_pb.astype(BF16), w_out.astype(BF16),
                ln_g.reshape(1, d_model).astype(F32), ln_b.reshape(1, d_model).astype(F32))
    return out.reshape(bn, s_len, d_model)


def kernel(x, positions, w_in, b_gate, sinks, w_pa, w_pb, w_out, ln_g, ln_b):
    for layer in range(w_in.shape[0]):
        x = _hybrid_layer(x, positions, w_in[layer], b_gate[layer], sinks[layer], w_pa[layer], w_pb[layer],
                          w_out[layer], ln_g[layer], ln_b[layer])
    return x
```

```python
import functools

import jax
import jax.numpy as jnp
import numpy as np
from jax import lax
from jax.experimental import pallas as pl
from jax.experimental.pallas import tpu as pltpu

F32 = jnp.float32
BF16 = jnp.bfloat16

D_MODEL = 2048
ROPE_THETA = 10000.0
LN_EPS = 1e-5
BLOCK = 128
LANES = 128
A_HEADS = 16
A_KV_HEADS = 2
A_HEAD_DIM = 64
A_WINDOW = 128
A_WIDTH = A_HEADS * A_HEAD_DIM
A_KV_WIDTH = A_KV_HEADS * A_HEAD_DIM
B_PATTERNS = ((128, 1), (512, 4), (2048, 16))
B_HEADS_PER_GROUP = 4
B_HEAD_DIM = 128
B_GROUP_WIDTH = B_HEADS_PER_GROUP * B_HEAD_DIM
B_N_GROUPS = len(B_PATTERNS)
B_QKV_WIDTH = B_N_GROUPS * B_GROUP_WIDTH
B_LSE_LANES = LANES // B_HEADS_PER_GROUP
DEPTH = 1
DN_ALPHA = float((2 * DEPTH) ** 0.25)

IN_SIZES = (A_WIDTH, A_KV_WIDTH, A_KV_WIDTH, A_WIDTH, B_QKV_WIDTH, B_QKV_WIDTH, B_QKV_WIDTH,
            B_GROUP_WIDTH, D_MODEL, D_MODEL)
IN_OFFSETS = tuple(int(o) for o in np.cumsum((0,) + IN_SIZES[:-1]))
(OFF_QA, OFF_KA, OFF_VA, OFF_GATE_A, OFF_QB, OFF_KB, OFF_VB, OFF_GATE_B, OFF_MG_A, OFF_MG_B) = IN_OFFSETS
D_IN = sum(IN_SIZES)

VMEM_LIMIT_BYTES = 60 * 1024 * 1024

EP_NONE, EP_ROPE64, EP_ROPE128, EP_SILU, EP_SIGMOID = range(5)

PROJ_TM = 1024
PROJ_TN = 2304

PK_MG_A = 0
PK_MG_B = PK_MG_A + D_MODEL
PK_VB = PK_MG_B + D_MODEL
PK_GATE_B = PK_VB + B_QKV_WIDTH
PK_QKB = PK_GATE_B + B_GROUP_WIDTH
PK_A = PK_QKB + 2 * B_QKV_WIDTH
PK_A_WIDTH = 2 * A_WIDTH + 2 * A_KV_WIDTH
A_COL_K = A_WIDTH
A_COL_V = A_COL_K + A_KV_WIDTH
A_COL_GATE = A_COL_V + A_KV_WIDTH
assert PK_A + PK_A_WIDTH == D_IN and PK_A % PK_A_WIDTH == 0 and PK_VB % B_GROUP_WIDTH == 0
assert PK_GATE_B % B_GROUP_WIDTH == 0 and PK_QKB % B_GROUP_WIDTH == 0 and D_IN % PROJ_TN == 0


def _compiler_params(semantics):
    return pltpu.CompilerParams(dimension_semantics=semantics, vmem_limit_bytes=VMEM_LIMIT_BYTES)


N_ROPE_TABLES = 4


def _rope_tables(pos, freq, select_ref, sign_ref):
    half = pos.shape[0] // 2
    lane = lax.broadcasted_iota(jnp.int32, (half, LANES), 1)
    ang = jnp.where(lane < LANES // 2, pos[:half], pos[half:]) * freq

    def to_head_layout(t):
        hi = t.astype(BF16)
        rest = t - hi.astype(F32)
        mid = rest.astype(BF16)
        low = (rest - mid.astype(F32)).astype(BF16)
        r = jnp.dot(jnp.concatenate([hi, mid, low], axis=1), select_ref[...], preferred_element_type=F32)
        return [jnp.concatenate([r[:, (2 * layout) * LANES:(2 * layout + 1) * LANES],
                                 r[:, (2 * layout + 1) * LANES:(2 * layout + 2) * LANES]], axis=0)
                for layout in range(2)]

    cos_a, cos_b = to_head_layout(jnp.cos(ang))
    sin_a, sin_b = to_head_layout(jnp.sin(ang))
    return [t * sign_ref[ti:ti + 1, :] for ti, t in enumerate((cos_a, sin_a, cos_b, sin_b))]


def _first_half_of_head64(lane):
    return jnp.bitwise_and(lane, A_HEAD_DIM - 1) < A_HEAD_DIM // 2


def _rope_inputs(positions):
    s_len = positions.shape[1]
    half_b = B_HEAD_DIM // 2
    assert A_HEAD_DIM * 2 == B_HEAD_DIM and 2 * half_b == LANES
    inv_b = ROPE_THETA ** (-jnp.arange(half_b, dtype=F32) / half_b)
    freq = jnp.concatenate([inv_b, inv_b])[None, :]
    pos = jnp.broadcast_to(positions.reshape(s_len, 1).astype(F32), (s_len, LANES))
    lane = np.arange(LANES)
    src_lane = np.stack([2 * (lane % (A_HEAD_DIM // 2)), lane % half_b])
    select = np.zeros((LANES, 4 * LANES), np.float32)
    for layout in range(2):
        for hf in range(2):
            select[hf * half_b + src_lane[layout], (2 * layout + hf) * LANES + lane] = 1.0
    select = np.concatenate([select] * 3, axis=0)
    sign = np.ones((N_ROPE_TABLES, LANES), np.float32)
    sign[1, lane % A_HEAD_DIM < A_HEAD_DIM // 2] = -1.0
    sign[3, lane < half_b] = -1.0
    return pos, freq, jnp.asarray(select, BF16), jnp.asarray(sign)


def _proj_kernel(tile_kinds, x_ref, w_ref, bias_ref, cos_a_ref, sin_a_ref, cos_b_ref, sin_b_ref, out_ref, xb_ref):
    n = pl.program_id(1)

    @pl.when(n == 0)
    def _():
        xb_ref[...] = x_ref[...].astype(BF16)

    def body(kinds):
        acc = jnp.dot(xb_ref[...], w_ref[...], preferred_element_type=F32)
        first_half = _first_half_of_head64(lax.broadcasted_iota(jnp.int32, (x_ref.shape[0], LANES), 1))
        for ci, kind in enumerate(kinds):
            cols = slice(ci * LANES, (ci + 1) * LANES)
            t = acc[:, cols]
            if kind == EP_ROPE64:
                rot = jnp.where(first_half, pltpu.roll(t, 96, axis=1), pltpu.roll(t, 32, axis=1))
                t = t * cos_a_ref[...] + rot * sin_a_ref[...]
            elif kind == EP_ROPE128:
                t = t * cos_b_ref[...] + pltpu.roll(t, 64, axis=1) * sin_b_ref[...]
            elif kind == EP_SILU:
                t = t * jax.nn.sigmoid(t)
            elif kind == EP_SIGMOID:
                t = jax.nn.sigmoid(t + bias_ref[:, cols])
            out_ref[:, cols] = t.astype(out_ref.dtype)

    branches = []
    for i, kinds in enumerate(tile_kinds):
        same = [br for br in branches if br[0] == kinds]
        if same:
            same[0][1].append(i)
        else:
            branches.append((kinds, [i]))
    for kinds, tiles in branches:
        cond = n == tiles[0]
        for i in tiles[1:]:
            cond = cond | (n == i)
        pl.when(cond)(functools.partial(body, kinds))


def _in_projection(x2d, w, bias, tables, chunk_kinds):
    s_len, k_dim = x2d.shape
    n_cols = w.shape[1]
    n_tiles = n_cols // PROJ_TN
    chunks_per_tile = PROJ_TN // LANES
    tile_kinds = tuple(tuple(chunk_kinds[t * chunks_per_tile:(t + 1) * chunks_per_tile]) for t in range(n_tiles))
    tm = min(PROJ_TM, s_len)
    row_spec = lambda width: pl.BlockSpec((tm, width), lambda m, n: (m, 0))
    return pl.pallas_call(
        functools.partial(_proj_kernel, tile_kinds),
        grid=(s_len // tm, n_tiles),
        in_specs=[row_spec(k_dim),
                  pl.BlockSpec((k_dim, PROJ_TN), lambda m, n: (0, n)),
                  pl.BlockSpec((1, PROJ_TN), lambda m, n: (0, n))] + [row_spec(LANES)] * N_ROPE_TABLES,
        out_specs=pl.BlockSpec((tm, PROJ_TN), lambda m, n: (m, n)),
        out_shape=jax.ShapeDtypeStruct((s_len, n_cols), BF16),
        scratch_shapes=[pltpu.VMEM((tm, k_dim), BF16)],
        compiler_params=_compiler_params(("arbitrary", "arbitrary")),
        name="in_projection",
    )(x2d, w, bias, *tables)


PACK_COLS = 256
PACK_BLOCKS_PER_STEP = 5
PACK_TN = PACK_BLOCKS_PER_STEP * PACK_COLS
assert D_IN % PACK_TN == 0


def _packed_layout():
    g = B_GROUP_WIDTH
    assert OFF_VA == OFF_KA + A_KV_WIDTH
    segments = [(OFF_MG_A, D_MODEL, [EP_SIGMOID]), (OFF_MG_B, D_MODEL, [EP_SIGMOID])]
    segments += [(OFF_VB, B_QKV_WIDTH, [EP_NONE]), (OFF_GATE_B, g, [EP_SILU])]
    for gi in range(B_N_GROUPS):
        segments += [(OFF_QB + gi * g, g, [EP_ROPE128]), (OFF_KB + gi * g, g, [EP_ROPE128])]
    segments += [(OFF_QA, A_WIDTH, [EP_ROPE64]),
                 (OFF_KA, 2 * A_KV_WIDTH, [EP_ROPE64] * (A_KV_WIDTH // LANES) + [EP_NONE] * (A_KV_WIDTH // LANES)),
                 (OFF_GATE_A, A_WIDTH, [EP_SILU])]
    perm, kinds = [], []
    for off, width, seg_kinds in segments:
        assert off % PACK_COLS == 0 and width % PACK_COLS == 0
        perm += [off // PACK_COLS + j for j in range(width // PACK_COLS)]
        kinds += seg_kinds * (width // LANES // len(seg_kinds))
    assert sorted(perm) == list(range(D_IN // PACK_COLS)) and len(kinds) == D_IN // LANES
    return np.asarray(perm, np.int32), kinds


def _pack_kernel(n_table_steps, perm_ref, pos_ref, freq_ref, select_ref, sign_ref, *refs):
    del perm_ref
    w_refs = refs[:PACK_BLOCKS_PER_STEP]
    out_ref = refs[PACK_BLOCKS_PER_STEP]
    table_refs = refs[PACK_BLOCKS_PER_STEP + 1:]
    for j, w_ref in enumerate(w_refs):
        out_ref[:, j * PACK_COLS:(j + 1) * PACK_COLS] = w_ref[...].astype(out_ref.dtype)

    @pl.when(pl.program_id(0) < n_table_steps)
    def _():
        for t_ref, t in zip(table_refs, _rope_tables(pos_ref[...], freq_ref[...], select_ref, sign_ref)):
            t_ref[...] = t


def _pack_weights_and_tables(w_in, b_gate, positions):
    perm, kinds = _packed_layout()
    k_dim = w_in.shape[0]
    n_steps = D_IN // PACK_TN
    pos, freq, select, sign = _rope_inputs(positions)
    s_len = pos.shape[0]
    n_table_steps = n_steps - 1
    assert s_len % (16 * n_table_steps) == 0
    const = lambda a: pl.BlockSpec(a.shape, lambda i, perm_ref: (0,) * a.ndim)
    table_rows = s_len // n_table_steps
    tab_spec = pl.BlockSpec((table_rows, LANES), lambda i, perm_ref: (jnp.minimum(i, n_table_steps - 1), 0))
    src = lambda j: pl.BlockSpec((k_dim, PACK_COLS), lambda i, perm_ref: (0, perm_ref[i * PACK_BLOCKS_PER_STEP + j]))
    tab = jax.ShapeDtypeStruct((s_len, LANES), F32)
    w, *tables = pl.pallas_call(
        functools.partial(_pack_kernel, n_table_steps),
        grid_spec=pltpu.PrefetchScalarGridSpec(
            num_scalar_prefetch=1, grid=(n_steps,),
            in_specs=[tab_spec, const(freq), const(select), const(sign)]
            + [src(j) for j in range(PACK_BLOCKS_PER_STEP)],
            out_specs=[pl.BlockSpec((k_dim, PACK_TN), lambda i, perm_ref: (0, i))] + [tab_spec] * N_ROPE_TABLES),
        out_shape=[jax.ShapeDtypeStruct((k_dim, D_IN), BF16)] + [tab] * N_ROPE_TABLES,
        compiler_params=_compiler_params(("arbitrary",)),
        name="pack_weights",
    )(jnp.asarray(perm), pos, freq, select, sign, *([w_in] * PACK_BLOCKS_PER_STEP))
    bias = jnp.concatenate([b_gate[0], b_gate[1], jnp.zeros((D_IN - 2 * D_MODEL,), F32)])[None, :]
    return w, kinds, bias, tables


ATT_TOKENS = 512
MIXER_B_TOKENS = 1024
STRIDED_MAX_DILATION = 4
PERM_ROWS = 256


def _band_bias(max_dist, reps):
    q_idx = np.arange(BLOCK)[:, None] + BLOCK
    k_idx = np.arange(2 * BLOCK)[None, :]
    dist = q_idx - k_idx
    band = (dist >= 0) & (dist <= max_dist)
    first = band & (k_idx >= BLOCK)
    both = np.stack([first, band]).astype(bool)
    bias = np.where(both, 0.0, -np.inf).astype(np.float32)
    return jnp.asarray(np.tile(bias, (1, 1, reps)))


def _block_bias(bias_ref, step, b):
    if b == 0:
        return bias_ref[jnp.where(step == 0, 0, 1)]
    return bias_ref[1]


def _swap_lane_halves(t):
    return pltpu.roll(t, LANES // 2, axis=1)


def _mixer_a_kernel(a_ref, bias_ref, out_ref, kbuf_ref, vbuf_ref):
    step = pl.program_id(0)
    tq = a_ref.shape[0]
    n_blocks = tq // BLOCK
    n_pairs = A_HEADS // 2
    pairs_per_group = n_pairs // A_KV_HEADS

    @pl.when(step == 0)
    def _():
        kbuf_ref[0:BLOCK, :] = jnp.zeros((BLOCK, A_KV_WIDTH), BF16)
        vbuf_ref[0:BLOCK, :] = jnp.zeros((BLOCK, A_KV_WIDTH), BF16)

    kbuf_ref[BLOCK:, :] = a_ref[:, A_COL_K:A_COL_K + A_KV_WIDTH]
    vbuf_ref[BLOCK:, :] = a_ref[:, A_COL_V:A_COL_V + A_KV_WIDTH]
    lane2 = lax.broadcasted_iota(jnp.int32, (2 * BLOCK, LANES), 1)
    low2 = lane2 < LANES // 2
    sink_slot = lax.broadcasted_iota(jnp.int32, (2 * BLOCK, LANES), 0) == 0
    denom_cols = jnp.concatenate([jnp.where(low2, 1.0, 0.0), jnp.where(low2, 0.0, 1.0)], axis=0).astype(BF16)

    for b in range(n_blocks):
        bias_kind = jnp.where(step == 0, 0, 1) if b == 0 else 1
        rows = slice(b * BLOCK, (b + 1) * BLOCK)
        kk = jnp.where(sink_slot, 0.0, kbuf_ref[b * BLOCK:(b + 2) * BLOCK, :].astype(F32))
        vv = jnp.where(sink_slot, 0.0, vbuf_ref[b * BLOCK:(b + 2) * BLOCK, :].astype(F32))
        kk_sw = _swap_lane_halves(kk)
        vv_sw = _swap_lane_halves(vv)
        k2, v2 = [], []
        for g in range(A_KV_HEADS):
            if g == 0:
                k_top, k_bot = jnp.where(low2, kk, 0.0), jnp.where(low2, 0.0, kk_sw)
                v_top, v_bot = jnp.where(low2, vv, 0.0), jnp.where(low2, 0.0, vv_sw)
            else:
                k_top, k_bot = jnp.where(low2, kk_sw, 0.0), jnp.where(low2, 0.0, kk)
                v_top, v_bot = jnp.where(low2, vv_sw, 0.0), jnp.where(low2, 0.0, vv)
            k2.append(jnp.concatenate([k_top, k_bot], axis=0).astype(BF16))
            v2.append(jnp.concatenate([jnp.concatenate([v_top, v_bot], axis=0).astype(BF16), denom_cols], axis=1))
        for p in range(n_pairs):
            g = p // pairs_per_group
            cols = slice(p * LANES, (p + 1) * LANES)
            qp = a_ref[rows, cols] * jnp.asarray(A_HEAD_DIM ** -0.5, BF16)
            s = (lax.dot_general(qp, k2[g], (((1,), (1,)), ((), ())), preferred_element_type=F32)
                 + bias_ref[bias_kind, p])
            m0b = jnp.broadcast_to(jnp.max(s[:, :2 * BLOCK], axis=1, keepdims=True), (BLOCK, LANES))
            m1b = jnp.broadcast_to(jnp.max(s[:, 2 * BLOCK:], axis=1, keepdims=True), (BLOCK, LANES))
            shifts = (m0b, m0b, m1b, m1b)
            prob = jnp.concatenate([jnp.exp(s[:, j * LANES:(j + 1) * LANES] - shifts[j]) for j in range(4)],
                                   axis=1).astype(BF16)
            o2 = jnp.dot(prob, v2[g], preferred_element_type=F32)
            o = o2[:, :LANES] / o2[:, LANES:]
            gate = a_ref[rows, A_COL_GATE + p * LANES:A_COL_GATE + (p + 1) * LANES].astype(F32)
            out_ref[rows, cols] = (o * gate).astype(out_ref.dtype)

    kbuf_ref[0:BLOCK, :] = kbuf_ref[tq:tq + BLOCK, :]
    vbuf_ref[0:BLOCK, :] = vbuf_ref[tq:tq + BLOCK, :]


def _mixer_a(h, sinks):
    s_len = h.shape[0]
    tq = ATT_TOKENS
    n_pairs = A_HEADS // 2
    band = _band_bias(A_WINDOW - 1, 2)
    sink_pairs = sinks.astype(F32).reshape(n_pairs, 2)
    col = jnp.arange(4 * BLOCK)
    bias = jnp.broadcast_to(band[:, None], (2, n_pairs, BLOCK, 4 * BLOCK))
    bias = jnp.where(col == 0, sink_pairs[None, :, 0, None, None], bias)
    bias = jnp.where(col == 2 * BLOCK, sink_pairs[None, :, 1, None, None], bias)
    return pl.pallas_call(
        _mixer_a_kernel,
        grid=(s_len // tq,),
        in_specs=[
            pl.BlockSpec((tq, PK_A_WIDTH), lambda i: (i, PK_A // PK_A_WIDTH)),
            pl.BlockSpec(bias.shape, lambda i: (0, 0, 0, 0), pipeline_mode=pl.Buffered(1)),
        ],
        out_specs=pl.BlockSpec((tq, A_WIDTH), lambda i: (i, 0)),
        out_shape=jax.ShapeDtypeStruct((s_len, A_WIDTH), BF16),
        scratch_shapes=[pltpu.VMEM((tq + BLOCK, A_KV_WIDTH), BF16), pltpu.VMEM((tq + BLOCK, A_KV_WIDTH), BF16)],
        compiler_params=_compiler_params(("arbitrary",)),
        name="mixer_a",
    )(h, bias)


def _mixer_b_kernel(d, nb, q_ref, k_ref, v_ref, bias_ref, *refs):
    by_mxu = d > STRIDED_MAX_DILATION
    if d == 1:
        o_ref, lse_ref, qsub, ksub, vsub = refs
    elif by_mxu:
        perm_ref, perm_t_ref, o_ref, lse_ref, qsub, ksub, vsub, ystage, lstage = refs
    else:
        o_ref, lse_ref, qsub, ksub, vsub, slab, ostage, lstage = refs
    step = pl.program_id(0)
    n_heads = B_HEADS_PER_GROUP
    span = BLOCK * d

    @pl.when(step == 0)
    def _():
        ksub[:, 0:BLOCK, :] = jnp.zeros((d, BLOCK, B_GROUP_WIDTH), BF16)
        vsub[:, 0:BLOCK, :] = jnp.zeros((d, BLOCK, B_GROUP_WIDTH), BF16)

    if d == 1:
        qsub[0] = q_ref[...]
        ksub[0, BLOCK:, :] = k_ref[...]
        vsub[0, BLOCK:, :] = v_ref[...]
    elif not by_mxu:
        for src, dst, row0 in ((q_ref, qsub, 0), (k_ref, ksub, BLOCK), (v_ref, vsub, BLOCK)):
            for c in range(n_heads):
                cols = slice(c * LANES, (c + 1) * LANES)
                sl = slab.at[c]
                sl[...] = src[:, cols].astype(F32)
                for r in range(d):
                    for b in range(nb):
                        piece = sl[pl.ds(b * span + r, BLOCK, stride=d), :]
                        dst[r, row0 + b * BLOCK:row0 + (b + 1) * BLOCK, cols] = piece.astype(BF16)
    else:
        per = PERM_ROWS // d
        n_perm_blocks = nb * span // PERM_ROWS
        for src, dst, row0 in ((q_ref, qsub, 0), (k_ref, ksub, BLOCK), (v_ref, vsub, BLOCK)):
            for j in range(n_perm_blocks):
                y = jnp.dot(perm_ref[...], src[j * PERM_ROWS:(j + 1) * PERM_ROWS, :],
                            preferred_element_type=F32).astype(BF16)
                for r in range(d):
                    dst[r, row0 + j * per:row0 + (j + 1) * per, :] = y[r * per:(r + 1) * per, :]

    ones = jnp.ones((2 * BLOCK, LANES), BF16)
    lane = lax.broadcasted_iota(jnp.int32, (BLOCK, LANES), 1)
    scale = B_HEAD_DIM ** -0.5
    exp2_scale = scale * float(np.log2(np.e))
    for r in range(d):
        for b in range(nb):
            bias = _block_bias(bias_ref, step, b)
            rows = slice(b * BLOCK, (b + 1) * BLOCK)
            lse_tile = jnp.zeros((BLOCK, LANES), F32)
            for h in range(n_heads):
                cols = slice(h * B_HEAD_DIM, (h + 1) * B_HEAD_DIM)
                kk = ksub[r, b * BLOCK:(b + 2) * BLOCK, cols]
                v2 = jnp.concatenate([vsub[r, b * BLOCK:(b + 2) * BLOCK, cols], ones], axis=1)
                s = lax.dot_general(qsub[r, rows, cols], kk, (((1,), (1,)), ((), ())),
                                    preferred_element_type=F32) + bias
                m = jnp.max(s, axis=1, keepdims=True)
                mb = jnp.broadcast_to(m, (BLOCK, LANES))
                prob = jnp.concatenate([jnp.exp2((s[:, :LANES] - mb) * exp2_scale),
                                        jnp.exp2((s[:, LANES:] - mb) * exp2_scale)], axis=1).astype(BF16)
                o2 = jnp.dot(prob, v2, preferred_element_type=F32)
                denom = o2[:, LANES:]
                o = o2[:, :LANES] / denom
                lse_h = mb * scale + jnp.log(denom)
                in_head = jnp.logical_and(lane >= h * B_LSE_LANES, lane < (h + 1) * B_LSE_LANES)
                lse_tile = jnp.where(in_head, lse_h, lse_tile)
                if d == 1:
                    o_ref[rows, cols] = o.astype(o_ref.dtype)
                elif not by_mxu:
                    ostage[h, pl.ds(b * span + r, BLOCK, stride=d), :] = o
                else:
                    o = o.astype(BF16)
                    for jj in range(BLOCK // per):
                        ystage[b * (BLOCK // per) + jj, r * per:(r + 1) * per, cols] = o[jj * per:(jj + 1) * per, :]
            if d == 1:
                lse_ref[rows, :] = lse_tile
            else:
                lstage[pl.ds(b * span + r, BLOCK, stride=d), :] = lse_tile

    if d > 1:
        if by_mxu:
            for j in range(n_perm_blocks):
                o_ref[j * PERM_ROWS:(j + 1) * PERM_ROWS, :] = jnp.dot(
                    perm_t_ref[...], ystage[j], preferred_element_type=F32).astype(o_ref.dtype)
        else:
            for h in range(n_heads):
                o_ref[:, h * B_HEAD_DIM:(h + 1) * B_HEAD_DIM] = ostage[h].astype(o_ref.dtype)
        lse_ref[...] = lstage[...]
    ksub[:, 0:BLOCK, :] = ksub[:, nb * BLOCK:(nb + 1) * BLOCK, :]
    vsub[:, 0:BLOCK, :] = vsub[:, nb * BLOCK:(nb + 1) * BLOCK, :]


def _mixer_b_group(h, gi):
    s_len = h.shape[0]
    window, d = B_PATTERNS[gi]
    assert window // d == BLOCK
    nb = max(1, MIXER_B_TOKENS // (BLOCK * d))
    t_rows = nb * BLOCK * d
    gw = B_GROUP_WIDTH
    col_q, col_v = (PK_QKB + gi * 2 * gw) // gw, (PK_VB + gi * gw) // gw
    bias = _band_bias(BLOCK, 1)
    blk = lambda c: pl.BlockSpec((t_rows, gw), lambda i: (i, c))
    scratch = [pltpu.VMEM((d, nb * BLOCK, gw), BF16), pltpu.VMEM((d, (nb + 1) * BLOCK, gw), BF16),
               pltpu.VMEM((d, (nb + 1) * BLOCK, gw), BF16)]
    operands = [h, h, h, bias]
    in_specs = [blk(col_q), blk(col_q + 1), blk(col_v), pl.BlockSpec(bias.shape, lambda i: (0, 0, 0))]
    if 1 < d <= STRIDED_MAX_DILATION:
        scratch += [pltpu.VMEM((B_HEADS_PER_GROUP, t_rows, LANES), F32),
                    pltpu.VMEM((B_HEADS_PER_GROUP, t_rows, LANES), F32), pltpu.VMEM((t_rows, LANES), F32)]
    elif d > 1:
        assert PERM_ROWS % d == 0 and (PERM_ROWS // d) % 16 == 0 and BLOCK % (PERM_ROWS // d) == 0
        per = PERM_ROWS // d
        perm = np.zeros((PERM_ROWS, PERM_ROWS), np.float32)
        l_idx, r_idx = np.meshgrid(np.arange(per), np.arange(d), indexing="ij")
        perm[r_idx * per + l_idx, l_idx * d + r_idx] = 1.0
        operands += [jnp.asarray(perm, BF16), jnp.asarray(perm.T, BF16)]
        in_specs += [pl.BlockSpec((PERM_ROWS, PERM_ROWS), lambda i: (0, 0))] * 2
        scratch += [pltpu.VMEM((t_rows // PERM_ROWS, PERM_ROWS, gw), BF16), pltpu.VMEM((t_rows, LANES), F32)]
    return pl.pallas_call(
        functools.partial(_mixer_b_kernel, d, nb),
        grid=(s_len // t_rows,),
        in_specs=in_specs,
        out_specs=[pl.BlockSpec((t_rows, gw), lambda i: (i, 0)), pl.BlockSpec((t_rows, LANES), lambda i: (i, 0))],
        out_shape=[jax.ShapeDtypeStruct((s_len, gw), BF16), jax.ShapeDtypeStruct((s_len, LANES), F32)],
        scratch_shapes=scratch,
        compiler_params=_compiler_params(("arbitrary",)),
        name=f"mixer_b_d{d}",
    )(*operands)


TAIL_TM = 512


def _tail_kernel(x_ref, ya_ref, o0_ref, o1_ref, o2_ref, l0_ref, l1_ref, l2_ref, gate_b_ref, sig_a_ref, sig_b_ref,
                 w_pa_ref, w_pb_ref, w_out_ref, ln_g_ref, ln_b_ref, out_ref):
    tm = x_ref.shape[0]
    outs = [o0_ref, o1_ref, o2_ref]
    lses = [l0_ref[...], l1_ref[...], l2_ref[...]]
    m = jnp.maximum(jnp.maximum(lses[0], lses[1]), lses[2])
    es = [jnp.exp(l - m) for l in lses]
    inv = 1.0 / (es[0] + es[1] + es[2])
    wts = [e * inv for e in es]
    yb_cols = []
    for h in range(B_HEADS_PER_GROUP):
        cols = slice(h * B_HEAD_DIM, (h + 1) * B_HEAD_DIM)
        acc = None
        for gi in range(B_N_GROUPS):
            w_h = jnp.broadcast_to(wts[gi][:, h * B_LSE_LANES:h * B_LSE_LANES + 1], (tm, B_HEAD_DIM))
            term = w_h * outs[gi][:, cols].astype(F32)
            acc = term if acc is None else acc + term
        yb_cols.append((acc * gate_b_ref[:, cols].astype(F32)).astype(BF16))
    yb = jnp.concatenate(yb_cols, axis=1)
    y_a = jnp.dot(ya_ref[...], w_pa_ref[...], preferred_element_type=F32)
    y_b = jnp.dot(yb, w_pb_ref[...], preferred_element_type=F32)
    merged = sig_a_ref[...].astype(F32) * y_a + sig_b_ref[...].astype(F32) * y_b
    sub = jnp.dot(merged.astype(BF16), w_out_ref[...], preferred_element_type=F32)
    z = DN_ALPHA * x_ref[...] + sub
    inv_d = 1.0 / z.shape[-1]
    mu = jnp.sum(z, axis=-1, keepdims=True) * inv_d
    var = jnp.sum(z * z, axis=-1, keepdims=True) * inv_d - mu * mu
    out_ref[...] = ((z - mu) * lax.rsqrt(var + LN_EPS) * ln_g_ref[...] + ln_b_ref[...]).astype(out_ref.dtype)


def _tail(x2d, ya, o_groups, lse_groups, h, w_pa, w_pb, w_out, ln_g, ln_b):
    s_len = x2d.shape[0]
    tm = TAIL_TM
    row = lambda width, cblk=0: pl.BlockSpec((tm, width), lambda i: (i, cblk))
    full = lambda a: pl.BlockSpec(a.shape, lambda i: (0,) * a.ndim, pipeline_mode=pl.Buffered(1))
    gw = B_GROUP_WIDTH
    return pl.pallas_call(
        _tail_kernel,
        grid=(s_len // tm,),
        in_specs=[row(D_MODEL), row(A_WIDTH), row(gw), row(gw), row(gw), row(LANES), row(LANES), row(LANES),
                  row(gw, PK_GATE_B // gw), row(D_MODEL, PK_MG_A // D_MODEL), row(D_MODEL, PK_MG_B // D_MODEL),
                  full(w_pa), full(w_pb), full(w_out), full(ln_g), full(ln_b)],
        out_specs=row(D_MODEL),
        out_shape=jax.ShapeDtypeStruct((s_len, D_MODEL), x2d.dtype),
        compiler_params=_compiler_params(("arbitrary",)),
        name="tail",
    )(x2d, ya, *o_groups, *lse_groups, h, h, h, w_pa, w_pb, w_out, ln_g, ln_b)


def _hybrid_layer(x, positions, w_in, b_gate, sinks, w_pa, w_pb, w_out, ln_g, ln_b):
    bn, s_len, d_model = x.shape
    assert bn == 1 and d_model == D_MODEL
    assert s_len % (BLOCK * B_PATTERNS[-1][1]) == 0 and s_len % PROJ_TM == 0
    x2d = x.reshape(s_len, d_model)
    w, kinds, bias, tables = _pack_weights_and_tables(w_in, b_gate, positions)
    h = _in_projection(x2d, w, bias, tables, kinds)
    o_groups, lse_groups = [], []
    for gi in range(B_N_GROUPS):
        o, lse = _mixer_b_group(h, gi)
        o_groups.append(o)
        lse_groups.append(lse)
    ya = _mixer_a(h, sinks)
    out = _tail(x2d, ya, o_groups, lse_groups, h, w_pa.astype(BF16), w_pb.astype(BF16), w_out.astype(BF16),
                ln_g.reshape(1, d_model).astype(F32), ln_b.reshape(1, d_model).astype(F32))
    return out.reshape(bn, s_len, d_model)


def kernel(x, positions, w_in, b_gate, sinks, w_pa, w_pb, w_out, ln_g, ln_b):
    for layer in range(w_in.shape[0]):
        x = _hybrid_layer(x, positions, w_in[layer], b_gate[layer], sinks[layer], w_pa[layer], w_pb[layer],
                          w_out[layer], ln_g[layer], ln_b[layer])
    return x
```

```python
import functools

import jax
import jax.numpy as jnp
import numpy as np
from jax import lax
from jax.experimental import pallas as pl
from jax.experimental.pallas import tpu as pltpu

F32 = jnp.float32
BF16 = jnp.bfloat16

D_MODEL = 2048
ROPE_THETA = 10000.0
LN_EPS = 1e-5
BLOCK = 128
LANES = 128
A_HEADS = 16
A_KV_HEADS = 2
A_HEAD_DIM = 64
A_WINDOW = 128
A_WIDTH = A_HEADS * A_HEAD_DIM
A_KV_WIDTH = A_KV_HEADS * A_HEAD_DIM
B_PATTERNS = ((128, 1), (512, 4), (2048, 16))
B_HEADS_PER_GROUP = 4
B_HEAD_DIM = 128
B_GROUP_WIDTH = B_HEADS_PER_GROUP * B_HEAD_DIM
B_N_GROUPS = len(B_PATTERNS)
B_QKV_WIDTH = B_N_GROUPS * B_GROUP_WIDTH
B_LSE_LANES = LANES // B_HEADS_PER_GROUP
DEPTH = 1
DN_ALPHA = float((2 * DEPTH) ** 0.25)

IN_SIZES = (A_WIDTH, A_KV_WIDTH, A_KV_WIDTH, A_WIDTH, B_QKV_WIDTH, B_QKV_WIDTH, B_QKV_WIDTH,
            B_GROUP_WIDTH, D_MODEL, D_MODEL)
IN_OFFSETS = tuple(int(o) for o in np.cumsum((0,) + IN_SIZES[:-1]))
(OFF_QA, OFF_KA, OFF_VA, OFF_GATE_A, OFF_QB, OFF_KB, OFF_VB, OFF_GATE_B, OFF_MG_A, OFF_MG_B) = IN_OFFSETS
D_IN = sum(IN_SIZES)

VMEM_LIMIT_BYTES = 60 * 1024 * 1024

EP_NONE, EP_ROPE64, EP_ROPE128, EP_SILU, EP_SIGMOID = range(5)

PROJ_TM = 1024
PROJ_TN = 2304

PK_MG_A = 0
PK_MG_B = PK_MG_A + D_MODEL
PK_VB = PK_MG_B + D_MODEL
PK_GATE_B = PK_VB + B_QKV_WIDTH
PK_QKB = PK_GATE_B + B_GROUP_WIDTH
PK_A = PK_QKB + 2 * B_QKV_WIDTH
PK_A_WIDTH = 2 * A_WIDTH + 2 * A_KV_WIDTH
A_COL_K = A_WIDTH
A_COL_V = A_COL_K + A_KV_WIDTH
A_COL_GATE = A_COL_V + A_KV_WIDTH
assert PK_A + PK_A_WIDTH == D_IN and PK_A % PK_A_WIDTH == 0 and PK_VB % B_GROUP_WIDTH == 0
assert PK_GATE_B % B_GROUP_WIDTH == 0 and PK_QKB % B_GROUP_WIDTH == 0 and D_IN % PROJ_TN == 0


def _compiler_params(semantics):
    return pltpu.CompilerParams(dimension_semantics=semantics, vmem_limit_bytes=VMEM_LIMIT_BYTES)


N_ROPE_TABLES = 4


def _bf16_pieces(t):
    hi = t.astype(BF16)
    rest = t - hi.astype(F32)
    mid = rest.astype(BF16)
    low = (rest - mid.astype(F32)).astype(BF16)
    return jnp.concatenate([hi, mid, low], axis=1)


def _rope_tables(pos_rows, freq, expand_ref, select_ref, sign_ref):
    n_chunks = pos_rows.shape[0] // 2
    eye = (lax.broadcasted_iota(jnp.int32, (LANES, LANES), 0)
           == lax.broadcasted_iota(jnp.int32, (LANES, LANES), 1))
    pos2 = []
    for c in range(n_chunks):
        diag = jnp.concatenate([jnp.where(eye, pos_rows[c:c + 1, :], 0.0),
                                jnp.where(eye, pos_rows[n_chunks + c:n_chunks + c + 1, :], 0.0)], axis=1)
        pos2.append(jnp.dot(_bf16_pieces(diag), expand_ref[...], preferred_element_type=F32))
    ang = jnp.concatenate(pos2, axis=0) * freq

    def to_head_layout(t):
        r = jnp.dot(_bf16_pieces(t), select_ref[...], preferred_element_type=F32)
        return [jnp.concatenate([r[:, (2 * layout) * LANES:(2 * layout + 1) * LANES],
                                 r[:, (2 * layout + 1) * LANES:(2 * layout + 2) * LANES]], axis=0)
                for layout in range(2)]

    cos_a, cos_b = to_head_layout(jnp.cos(ang))
    sin_a, sin_b = to_head_layout(jnp.sin(ang))
    return [t * sign_ref[ti:ti + 1, :] for ti, t in enumerate((cos_a, sin_a, cos_b, sin_b))]


def _first_half_of_head64(lane):
    return jnp.bitwise_and(lane, A_HEAD_DIM - 1) < A_HEAD_DIM // 2


def _rope_inputs(positions):
    s_len = positions.shape[1]
    half_b = B_HEAD_DIM // 2
    assert A_HEAD_DIM * 2 == B_HEAD_DIM and 2 * half_b == LANES
    inv_b = ROPE_THETA ** (-jnp.arange(half_b, dtype=F32) / half_b)
    freq = jnp.concatenate([inv_b, inv_b])[None, :]
    pos_rows = positions.astype(F32).reshape(s_len // LANES, LANES)
    lane = np.arange(LANES)
    expand = np.zeros((2 * LANES, LANES), np.float32)
    expand[:LANES, :half_b] = 1.0
    expand[LANES:, half_b:] = 1.0
    expand = np.concatenate([expand] * 3, axis=0)
    src_lane = np.stack([2 * (lane % (A_HEAD_DIM // 2)), lane % half_b])
    select = np.zeros((LANES, 4 * LANES), np.float32)
    for layout in range(2):
        for hf in range(2):
            select[hf * half_b + src_lane[layout], (2 * layout + hf) * LANES + lane] = 1.0
    select = np.concatenate([select] * 3, axis=0)
    sign = np.ones((N_ROPE_TABLES, LANES), np.float32)
    sign[1, lane % A_HEAD_DIM < A_HEAD_DIM // 2] = -1.0
    sign[3, lane < half_b] = -1.0
    return pos_rows, freq, jnp.asarray(expand, BF16), jnp.asarray(select, BF16), jnp.asarray(sign)


def _proj_kernel(tile_kinds, x_ref, w_ref, bias_ref, cos_a_ref, sin_a_ref, cos_b_ref, sin_b_ref, out_ref, xb_ref):
    n = pl.program_id(1)

    @pl.when(n == 0)
    def _():
        xb_ref[...] = x_ref[...].astype(BF16)

    def body(kinds):
        acc = jnp.dot(xb_ref[...], w_ref[...], preferred_element_type=F32)
        first_half = _first_half_of_head64(lax.broadcasted_iota(jnp.int32, (x_ref.shape[0], LANES), 1))
        for ci, kind in enumerate(kinds):
            cols = slice(ci * LANES, (ci + 1) * LANES)
            t = acc[:, cols]
            if kind == EP_ROPE64:
                rot = jnp.where(first_half, pltpu.roll(t, 96, axis=1), pltpu.roll(t, 32, axis=1))
                t = t * cos_a_ref[...] + rot * sin_a_ref[...]
            elif kind == EP_ROPE128:
                t = t * cos_b_ref[...] + pltpu.roll(t, 64, axis=1) * sin_b_ref[...]
            elif kind == EP_SILU:
                t = t * jax.nn.sigmoid(t)
            elif kind == EP_SIGMOID:
                t = jax.nn.sigmoid(t + bias_ref[:, cols])
            out_ref[:, cols] = t.astype(out_ref.dtype)

    branches = []
    for i, kinds in enumerate(tile_kinds):
        same = [br for br in branches if br[0] == kinds]
        if same:
            same[0][1].append(i)
        else:
            branches.append((kinds, [i]))
    for kinds, tiles in branches:
        cond = n == tiles[0]
        for i in tiles[1:]:
            cond = cond | (n == i)
        pl.when(cond)(functools.partial(body, kinds))


def _in_projection(x2d, w, bias, tables, chunk_kinds):
    s_len, k_dim = x2d.shape
    n_cols = w.shape[1]
    n_tiles = n_cols // PROJ_TN
    chunks_per_tile = PROJ_TN // LANES
    tile_kinds = tuple(tuple(chunk_kinds[t * chunks_per_tile:(t + 1) * chunks_per_tile]) for t in range(n_tiles))
    tm = min(PROJ_TM, s_len)
    row_spec = lambda width: pl.BlockSpec((tm, width), lambda m, n: (m, 0))
    return pl.pallas_call(
        functools.partial(_proj_kernel, tile_kinds),
        grid=(s_len // tm, n_tiles),
        in_specs=[row_spec(k_dim),
                  pl.BlockSpec((k_dim, PROJ_TN), lambda m, n: (0, n)),
                  pl.BlockSpec((1, PROJ_TN), lambda m, n: (0, n))] + [row_spec(LANES)] * N_ROPE_TABLES,
        out_specs=pl.BlockSpec((tm, PROJ_TN), lambda m, n: (m, n)),
        out_shape=jax.ShapeDtypeStruct((s_len, n_cols), BF16),
        scratch_shapes=[pltpu.VMEM((tm, k_dim), BF16)],
        compiler_params=_compiler_params(("arbitrary", "arbitrary")),
        name="in_projection",
    )(x2d, w, bias, *tables)


PACK_COLS = 256
PACK_BLOCKS_PER_STEP = 5
PACK_TN = PACK_BLOCKS_PER_STEP * PACK_COLS
assert D_IN % PACK_TN == 0


def _packed_layout():
    g = B_GROUP_WIDTH
    assert OFF_VA == OFF_KA + A_KV_WIDTH
    segments = [(OFF_MG_A, D_MODEL, [EP_SIGMOID]), (OFF_MG_B, D_MODEL, [EP_SIGMOID])]
    segments += [(OFF_VB, B_QKV_WIDTH, [EP_NONE]), (OFF_GATE_B, g, [EP_SILU])]
    for gi in range(B_N_GROUPS):
        segments += [(OFF_QB + gi * g, g, [EP_ROPE128]), (OFF_KB + gi * g, g, [EP_ROPE128])]
    segments += [(OFF_QA, A_WIDTH, [EP_ROPE64]),
                 (OFF_KA, 2 * A_KV_WIDTH, [EP_ROPE64] * (A_KV_WIDTH // LANES) + [EP_NONE] * (A_KV_WIDTH // LANES)),
                 (OFF_GATE_A, A_WIDTH, [EP_SILU])]
    perm, kinds = [], []
    for off, width, seg_kinds in segments:
        assert off % PACK_COLS == 0 and width % PACK_COLS == 0
        perm += [off // PACK_COLS + j for j in range(width // PACK_COLS)]
        kinds += seg_kinds * (width // LANES // len(seg_kinds))
    assert sorted(perm) == list(range(D_IN // PACK_COLS)) and len(kinds) == D_IN // LANES
    return np.asarray(perm, np.int32), kinds


def _pack_kernel(n_table_steps, perm_ref, pos_ref, freq_ref, expand_ref, select_ref, sign_ref, *refs):
    del perm_ref
    w_refs = refs[:PACK_BLOCKS_PER_STEP]
    out_ref = refs[PACK_BLOCKS_PER_STEP]
    table_refs = refs[PACK_BLOCKS_PER_STEP + 1:]
    for j, w_ref in enumerate(w_refs):
        out_ref[:, j * PACK_COLS:(j + 1) * PACK_COLS] = w_ref[...].astype(out_ref.dtype)

    @pl.when(pl.program_id(0) < n_table_steps)
    def _():
        tables = _rope_tables(pos_ref[...], freq_ref[...], expand_ref, select_ref, sign_ref)
        for t_ref, t in zip(table_refs, tables):
            t_ref[...] = t


def _pack_weights_and_tables(w_in, b_gate, positions):
    perm, kinds = _packed_layout()
    k_dim = w_in.shape[0]
    n_steps = D_IN // PACK_TN
    pos_rows, freq, expand, select, sign = _rope_inputs(positions)
    s_len = positions.shape[1]
    n_table_steps = n_steps - 1
    assert s_len % (8 * LANES * n_table_steps) == 0
    const = lambda a: pl.BlockSpec(a.shape, lambda i, perm_ref: (0,) * a.ndim)
    table_rows = s_len // n_table_steps
    table_block = lambda i, perm_ref: (jnp.minimum(i, n_table_steps - 1), 0)
    tab_spec = pl.BlockSpec((table_rows, LANES), table_block)
    pos_spec = pl.BlockSpec((table_rows // LANES, LANES), table_block)
    src = lambda j: pl.BlockSpec((k_dim, PACK_COLS), lambda i, perm_ref: (0, perm_ref[i * PACK_BLOCKS_PER_STEP + j]))
    tab = jax.ShapeDtypeStruct((s_len, LANES), F32)
    w, *tables = pl.pallas_call(
        functools.partial(_pack_kernel, n_table_steps),
        grid_spec=pltpu.PrefetchScalarGridSpec(
            num_scalar_prefetch=1, grid=(n_steps,),
            in_specs=[pos_spec, const(freq), const(expand), const(select), const(sign)]
            + [src(j) for j in range(PACK_BLOCKS_PER_STEP)],
            out_specs=[pl.BlockSpec((k_dim, PACK_TN), lambda i, perm_ref: (0, i))] + [tab_spec] * N_ROPE_TABLES),
        out_shape=[jax.ShapeDtypeStruct((k_dim, D_IN), BF16)] + [tab] * N_ROPE_TABLES,
        compiler_params=_compiler_params(("arbitrary",)),
        name="pack_weights",
    )(jnp.asarray(perm), pos_rows, freq, expand, select, sign, *([w_in] * PACK_BLOCKS_PER_STEP))
    bias = jnp.concatenate([b_gate[0], b_gate[1], jnp.zeros((D_IN - 2 * D_MODEL,), F32)])[None, :]
    return w, kinds, bias, tables


ATT_TOKENS = 1024
MIXER_B_TOKENS = 2048
STRIDED_MAX_DILATION = 4
PERM_ROWS = 256


def _band_bias(max_dist, reps):
    q_idx = np.arange(BLOCK)[:, None] + BLOCK
    k_idx = np.arange(2 * BLOCK)[None, :]
    dist = q_idx - k_idx
    band = (dist >= 0) & (dist <= max_dist)
    first = band & (k_idx >= BLOCK)
    both = np.stack([first, band]).astype(bool)
    bias = np.where(both, 0.0, -np.inf).astype(np.float32)
    return jnp.asarray(np.tile(bias, (1, 1, reps)))


def _block_bias(bias_ref, step, b):
    if b == 0:
        return bias_ref[jnp.where(step == 0, 0, 1)]
    return bias_ref[1]


def _swap_lane_halves(t):
    return pltpu.roll(t, LANES // 2, axis=1)


def _mixer_a_kernel(a_ref, bias_ref, out_ref, kbuf_ref, vbuf_ref):
    step = pl.program_id(0)
    tq = a_ref.shape[0]
    n_blocks = tq // BLOCK
    n_pairs = A_HEADS // 2
    pairs_per_group = n_pairs // A_KV_HEADS

    @pl.when(step == 0)
    def _():
        kbuf_ref[0:BLOCK, :] = jnp.zeros((BLOCK, A_KV_WIDTH), BF16)
        vbuf_ref[0:BLOCK, :] = jnp.zeros((BLOCK, A_KV_WIDTH), BF16)

    kbuf_ref[BLOCK:, :] = a_ref[:, A_COL_K:A_COL_K + A_KV_WIDTH]
    vbuf_ref[BLOCK:, :] = a_ref[:, A_COL_V:A_COL_V + A_KV_WIDTH]
    lane2 = lax.broadcasted_iota(jnp.int32, (2 * BLOCK, LANES), 1)
    low2 = lane2 < LANES // 2
    sink_slot = lax.broadcasted_iota(jnp.int32, (2 * BLOCK, LANES), 0) == 0
    denom_cols = jnp.concatenate([jnp.where(low2, 1.0, 0.0), jnp.where(low2, 0.0, 1.0)], axis=0).astype(BF16)

    for b in range(n_blocks):
        bias_kind = jnp.where(step == 0, 0, 1) if b == 0 else 1
        rows = slice(b * BLOCK, (b + 1) * BLOCK)
        kk = jnp.where(sink_slot, 0.0, kbuf_ref[b * BLOCK:(b + 2) * BLOCK, :].astype(F32))
        vv = jnp.where(sink_slot, 0.0, vbuf_ref[b * BLOCK:(b + 2) * BLOCK, :].astype(F32))
        kk_sw = _swap_lane_halves(kk)
        vv_sw = _swap_lane_halves(vv)
        k2, v2 = [], []
        for g in range(A_KV_HEADS):
            if g == 0:
                k_top, k_bot = jnp.where(low2, kk, 0.0), jnp.where(low2, 0.0, kk_sw)
                v_top, v_bot = jnp.where(low2, vv, 0.0), jnp.where(low2, 0.0, vv_sw)
            else:
                k_top, k_bot = jnp.where(low2, kk_sw, 0.0), jnp.where(low2, 0.0, kk)
                v_top, v_bot = jnp.where(low2, vv_sw, 0.0), jnp.where(low2, 0.0, vv)
            k2.append(jnp.concatenate([k_top, k_bot], axis=0).astype(BF16))
            v2.append(jnp.concatenate([jnp.concatenate([v_top, v_bot], axis=0).astype(BF16), denom_cols], axis=1))
        for p in range(n_pairs):
            g = p // pairs_per_group
            cols = slice(p * LANES, (p + 1) * LANES)
            qp = a_ref[rows, cols] * jnp.asarray(A_HEAD_DIM ** -0.5, BF16)
            s = (lax.dot_general(qp, k2[g], (((1,), (1,)), ((), ())), preferred_element_type=F32)
                 + bias_ref[bias_kind, p])
            m0b = jnp.broadcast_to(jnp.max(s[:, :2 * BLOCK], axis=1, keepdims=True), (BLOCK, LANES))
            m1b = jnp.broadcast_to(jnp.max(s[:, 2 * BLOCK:], axis=1, keepdims=True), (BLOCK, LANES))
            shifts = (m0b, m0b, m1b, m1b)
            prob = jnp.concatenate([jnp.exp(s[:, j * LANES:(j + 1) * LANES] - shifts[j]) for j in range(4)],
                                   axis=1).astype(BF16)
            o2 = jnp.dot(prob, v2[g], preferred_element_type=F32)
            o = o2[:, :LANES] / o2[:, LANES:]
            gate = a_ref[rows, A_COL_GATE + p * LANES:A_COL_GATE + (p + 1) * LANES].astype(F32)
            out_ref[rows, cols] = (o * gate).astype(out_ref.dtype)

    kbuf_ref[0:BLOCK, :] = kbuf_ref[tq:tq + BLOCK, :]
    vbuf_ref[0:BLOCK, :] = vbuf_ref[tq:tq + BLOCK, :]


def _mixer_a(h, sinks):
    s_len = h.shape[0]
    tq = ATT_TOKENS
    n_pairs = A_HEADS // 2
    band = _band_bias(A_WINDOW - 1, 2)
    sink_pairs = sinks.astype(F32).reshape(n_pairs, 2)
    col = jnp.arange(4 * BLOCK)
    bias = jnp.broadcast_to(band[:, None], (2, n_pairs, BLOCK, 4 * BLOCK))
    bias = jnp.where(col == 0, sink_pairs[None, :, 0, None, None], bias)
    bias = jnp.where(col == 2 * BLOCK, sink_pairs[None, :, 1, None, None], bias)
    return pl.pallas_call(
        _mixer_a_kernel,
        grid=(s_len // tq,),
        in_specs=[
            pl.BlockSpec((tq, PK_A_WIDTH), lambda i: (i, PK_A // PK_A_WIDTH)),
            pl.BlockSpec(bias.shape, lambda i: (0, 0, 0, 0), pipeline_mode=pl.Buffered(1)),
        ],
        out_specs=pl.BlockSpec((tq, A_WIDTH), lambda i: (i, 0)),
        out_shape=jax.ShapeDtypeStruct((s_len, A_WIDTH), BF16),
        scratch_shapes=[pltpu.VMEM((tq + BLOCK, A_KV_WIDTH), BF16), pltpu.VMEM((tq + BLOCK, A_KV_WIDTH), BF16)],
        compiler_params=_compiler_params(("arbitrary",)),
        name="mixer_a",
    )(h, bias)


def _mixer_b_kernel(d, nb, q_ref, k_ref, v_ref, bias_ref, *refs):
    by_mxu = d > STRIDED_MAX_DILATION
    if d == 1:
        o_ref, lse_ref, qsub, ksub, vsub = refs
    elif by_mxu:
        perm_ref, perm_t_ref, o_ref, lse_ref, qsub, ksub, vsub, ystage, lstage = refs
    else:
        o_ref, lse_ref, qsub, ksub, vsub, slab, ostage, lstage = refs
    step = pl.program_id(0)
    n_heads = B_HEADS_PER_GROUP
    span = BLOCK * d

    @pl.when(step == 0)
    def _():
        ksub[:, 0:BLOCK, :] = jnp.zeros((d, BLOCK, B_GROUP_WIDTH), BF16)
        vsub[:, 0:BLOCK, :] = jnp.zeros((d, BLOCK, B_GROUP_WIDTH), BF16)

    if d == 1:
        qsub[0] = q_ref[...]
        ksub[0, BLOCK:, :] = k_ref[...]
        vsub[0, BLOCK:, :] = v_ref[...]
    elif not by_mxu:
        for src, dst, row0 in ((q_ref, qsub, 0), (k_ref, ksub, BLOCK), (v_ref, vsub, BLOCK)):
            for c in range(n_heads):
                cols = slice(c * LANES, (c + 1) * LANES)
                sl = slab.at[c]
                sl[...] = src[:, cols].astype(F32)
                for r in range(d):
                    for b in range(nb):
                        piece = sl[pl.ds(b * span + r, BLOCK, stride=d), :]
                        dst[r, row0 + b * BLOCK:row0 + (b + 1) * BLOCK, cols] = piece.astype(BF16)
    else:
        per = PERM_ROWS // d
        n_perm_blocks = nb * span // PERM_ROWS
        for src, dst, row0 in ((q_ref, qsub, 0), (k_ref, ksub, BLOCK), (v_ref, vsub, BLOCK)):
            for j in range(n_perm_blocks):
                y = jnp.dot(perm_ref[...], src[j * PERM_ROWS:(j + 1) * PERM_ROWS, :],
                            preferred_element_type=F32).astype(BF16)
                for r in range(d):
                    dst[r, row0 + j * per:row0 + (j + 1) * per, :] = y[r * per:(r + 1) * per, :]

    ones = jnp.ones((2 * BLOCK, LANES), BF16)
    lane = lax.broadcasted_iota(jnp.int32, (BLOCK, LANES), 1)
    scale = B_HEAD_DIM ** -0.5
    exp2_scale = scale * float(np.log2(np.e))
    for r in range(d):
        for b in range(nb):
            bias = _block_bias(bias_ref, step, b)
            rows = slice(b * BLOCK, (b + 1) * BLOCK)
            lse_tile = jnp.zeros((BLOCK, LANES), F32)
            for h in range(n_heads):
                cols = slice(h * B_HEAD_DIM, (h + 1) * B_HEAD_DIM)
                kk = ksub[r, b * BLOCK:(b + 2) * BLOCK, cols]
                v2 = jnp.concatenate([vsub[r, b * BLOCK:(b + 2) * BLOCK, cols], ones], axis=1)
                s = lax.dot_general(qsub[r, rows, cols], kk, (((1,), (1,)), ((), ())),
                                    preferred_element_type=F32) + bias
                m = jnp.max(s, axis=1, keepdims=True)
                mb = jnp.broadcast_to(m, (BLOCK, LANES))
                prob = jnp.concatenate([jnp.exp2((s[:, :LANES] - mb) * exp2_scale),
                                        jnp.exp2((s[:, LANES:] - mb) * exp2_scale)], axis=1).astype(BF16)
                o2 = jnp.dot(prob, v2, preferred_element_type=F32)
                denom = o2[:, LANES:]
                o = o2[:, :LANES] / denom
                lse_h = mb * scale + jnp.log(denom)
                in_head = jnp.logical_and(lane >= h * B_LSE_LANES, lane < (h + 1) * B_LSE_LANES)
                lse_tile = jnp.where(in_head, lse_h, lse_tile)
                if d == 1:
                    o_ref[rows, cols] = o.astype(o_ref.dtype)
                elif not by_mxu:
                    ostage[h, pl.ds(b * span + r, BLOCK, stride=d), :] = o
                else:
                    o = o.astype(BF16)
                    for jj in range(BLOCK // per):
                        ystage[b * (BLOCK // per) + jj, r * per:(r + 1) * per, cols] = o[jj * per:(jj + 1) * per, :]
            if d == 1:
                lse_ref[rows, :] = lse_tile
            else:
                lstage[pl.ds(b * span + r, BLOCK, stride=d), :] = lse_tile

    if d > 1:
        if by_mxu:
            for j in range(n_perm_blocks):
                o_ref[j * PERM_ROWS:(j + 1) * PERM_ROWS, :] = jnp.dot(
                    perm_t_ref[...], ystage[j], preferred_element_type=F32).astype(o_ref.dtype)
        else:
            for h in range(n_heads):
                o_ref[:, h * B_HEAD_DIM:(h + 1) * B_HEAD_DIM] = ostage[h].astype(o_ref.dtype)
        lse_ref[...] = lstage[...]
    ksub[:, 0:BLOCK, :] = ksub[:, nb * BLOCK:(nb + 1) * BLOCK, :]
    vsub[:, 0:BLOCK, :] = vsub[:, nb * BLOCK:(nb + 1) * BLOCK, :]


def _mixer_b_group(h, gi):
    s_len = h.shape[0]
    window, d = B_PATTERNS[gi]
    assert window // d == BLOCK
    nb = max(1, MIXER_B_TOKENS // (BLOCK * d))
    t_rows = nb * BLOCK * d
    gw = B_GROUP_WIDTH
    col_q, col_v = (PK_QKB + gi * 2 * gw) // gw, (PK_VB + gi * gw) // gw
    bias = _band_bias(BLOCK, 1)
    blk = lambda c: pl.BlockSpec((t_rows, gw), lambda i: (i, c))
    scratch = [pltpu.VMEM((d, nb * BLOCK, gw), BF16), pltpu.VMEM((d, (nb + 1) * BLOCK, gw), BF16),
               pltpu.VMEM((d, (nb + 1) * BLOCK, gw), BF16)]
    operands = [h, h, h, bias]
    in_specs = [blk(col_q), blk(col_q + 1), blk(col_v), pl.BlockSpec(bias.shape, lambda i: (0, 0, 0))]
    if 1 < d <= STRIDED_MAX_DILATION:
        scratch += [pltpu.VMEM((B_HEADS_PER_GROUP, t_rows, LANES), F32),
                    pltpu.VMEM((B_HEADS_PER_GROUP, t_rows, LANES), F32), pltpu.VMEM((t_rows, LANES), F32)]
    elif d > 1:
        assert PERM_ROWS % d == 0 and (PERM_ROWS // d) % 16 == 0 and BLOCK % (PERM_ROWS // d) == 0
        per = PERM_ROWS // d
        perm = np.zeros((PERM_ROWS, PERM_ROWS), np.float32)
        l_idx, r_idx = np.meshgrid(np.arange(per), np.arange(d), indexing="ij")
        perm[r_idx * per + l_idx, l_idx * d + r_idx] = 1.0
        operands += [jnp.asarray(perm, BF16), jnp.asarray(perm.T, BF16)]
        in_specs += [pl.BlockSpec((PERM_ROWS, PERM_ROWS), lambda i: (0, 0))] * 2
        scratch += [pltpu.VMEM((t_rows // PERM_ROWS, PERM_ROWS, gw), BF16), pltpu.VMEM((t_rows, LANES), F32)]
    return pl.pallas_call(
        functools.partial(_mixer_b_kernel, d, nb),
        grid=(s_len // t_rows,),
        in_specs=in_specs,
        out_specs=[pl.BlockSpec((t_rows, gw), lambda i: (i, 0)), pl.BlockSpec((t_rows, LANES), lambda i: (i, 0))],
        out_shape=[jax.ShapeDtypeStruct((s_len, gw), BF16), jax.ShapeDtypeStruct((s_len, LANES), F32)],
        scratch_shapes=scratch,
        compiler_params=_compiler_params(("arbitrary",)),
        name=f"mixer_b_d{d}",
    )(*operands)


TAIL_TM = 512


def _tail_kernel(x_ref, ya_ref, o0_ref, o1_ref, o2_ref, l0_ref, l1_ref, l2_ref, gate_b_ref, sig_a_ref, sig_b_ref,
                 w_pa_ref, w_pb_ref, w_out_ref, ln_g_ref, ln_b_ref, out_ref):
    tm = x_ref.shape[0]
    outs = [o0_ref, o1_ref, o2_ref]
    lses = [l0_ref[...], l1_ref[...], l2_ref[...]]
    m = jnp.maximum(jnp.maximum(lses[0], lses[1]), lses[2])
    es = [jnp.exp(l - m) for l in lses]
    inv = 1.0 / (es[0] + es[1] + es[2])
    wts = [e * inv for e in es]
    yb_cols = []
    for h in range(B_HEADS_PER_GROUP):
        cols = slice(h * B_HEAD_DIM, (h + 1) * B_HEAD_DIM)
        acc = None
        for gi in range(B_N_GROUPS):
            w_h = jnp.broadcast_to(wts[gi][:, h * B_LSE_LANES:h * B_LSE_LANES + 1], (tm, B_HEAD_DIM))
            term = w_h * outs[gi][:, cols].astype(F32)
            acc = term if acc is None else acc + term
        yb_cols.append((acc * gate_b_ref[:, cols].astype(F32)).astype(BF16))
    yb = jnp.concatenate(yb_cols, axis=1)
    y_a = jnp.dot(ya_ref[...], w_pa_ref[...], preferred_element_type=F32)
    y_b = jnp.dot(yb, w_pb_ref[...], preferred_element_type=F32)
    merged = sig_a_ref[...].astype(F32) * y_a + sig_b_ref[...].astype(F32) * y_b
    sub = jnp.dot(merged.astype(BF16), w_out_ref[...], preferred_element_type=F32)
    z = DN_ALPHA * x_ref[...] + sub
    inv_d = 1.0 / z.shape[-1]
    mu = jnp.sum(z, axis=-1, keepdims=True) * inv_d
    var = jnp.sum(z * z, axis=-1, keepdims=True) * inv_d - mu * mu
    out_ref[...] = ((z - mu) * lax.rsqrt(var + LN_EPS) * ln_g_ref[...] + ln_b_ref[...]).astype(out_ref.dtype)


def _tail(x2d, ya, o_groups, lse_groups, h, w_pa, w_pb, w_out, ln_g, ln_b):
    s_len = x2d.shape[0]
    tm = TAIL_TM
    row = lambda width, cblk=0: pl.BlockSpec((tm, width), lambda i: (i, cblk))
    full = lambda a: pl.BlockSpec(a.shape, lambda i: (0,) * a.ndim, pipeline_mode=pl.Buffered(1))
    gw = B_GROUP_WIDTH
    return pl.pallas_call(
        _tail_kernel,
        grid=(s_len // tm,),
        in_specs=[row(D_MODEL), row(A_WIDTH), row(gw), row(gw), row(gw), row(LANES), row(LANES), row(LANES),
                  row(gw, PK_GATE_B // gw), row(D_MODEL, PK_MG_A // D_MODEL), row(D_MODEL, PK_MG_B // D_MODEL),
                  full(w_pa), full(w_pb), full(w_out), full(ln_g), full(ln_b)],
        out_specs=row(D_MODEL),
        out_shape=jax.ShapeDtypeStruct((s_len, D_MODEL), x2d.dtype),
        compiler_params=_compiler_params(("arbitrary",)),
        name="tail",
    )(x2d, ya, *o_groups, *lse_groups, h, h, h, w_pa, w_pb, w_out, ln_g, ln_b)


def _hybrid_layer(x, positions, w_in, b_gate, sinks, w_pa, w_pb, w_out, ln_g, ln_b):
    bn, s_len, d_model = x.shape
    assert bn == 1 and d_model == D_MODEL
    assert s_len % (BLOCK * B_PATTERNS[-1][1]) == 0 and s_len % PROJ_TM == 0
    x2d = x.reshape(s_len, d_model)
    w, kinds, bias, tables = _pack_weights_and_tables(w_in, b_gate, positions)
    h = _in_projection(x2d, w, bias, tables, kinds)
    o_groups, lse_groups = [], []
    for gi in range(B_N_GROUPS):
        o, lse = _mixer_b_group(h, gi)
        o_groups.append(o)
        lse_groups.append(lse)
    ya = _mixer_a(h, sinks)
    out = _tail(x2d, ya, o_groups, lse_groups, h, w_pa.astype(BF16), w_pb.astype(BF16), w_out.astype(BF16),
                ln_g.reshape(1, d_model).astype(F32), ln_b.reshape(1, d_model).astype(F32))
    return out.reshape(bn, s_len, d_model)


def kernel(x, positions, w_in, b_gate, sinks, w_pa, w_pb, w_out, ln_g, ln_b):
    for layer in range(w_in.shape[0]):
        x = _hybrid_layer(x, positions, w_in[layer], b_gate[layer], sinks[layer], w_pa[layer], w_pb[layer],
                          w_out[layer], ln_g[layer], ln_b[layer])
    return x
```

```python
import functools

import jax
import jax.numpy as jnp
import numpy as np
from jax import lax
from jax.experimental import pallas as pl
from jax.experimental.pallas import tpu as pltpu

F32 = jnp.float32
BF16 = jnp.bfloat16

D_MODEL = 2048
ROPE_THETA = 10000.0
LN_EPS = 1e-5
BLOCK = 128
LANES = 128
A_HEADS = 16
A_KV_HEADS = 2
A_HEAD_DIM = 64
A_WINDOW = 128
A_WIDTH = A_HEADS * A_HEAD_DIM
A_KV_WIDTH = A_KV_HEADS * A_HEAD_DIM
B_PATTERNS = ((128, 1), (512, 4), (2048, 16))
B_HEADS_PER_GROUP = 4
B_HEAD_DIM = 128
B_GROUP_WIDTH = B_HEADS_PER_GROUP * B_HEAD_DIM
B_N_GROUPS = len(B_PATTERNS)
B_QKV_WIDTH = B_N_GROUPS * B_GROUP_WIDTH
B_LSE_LANES = LANES // B_HEADS_PER_GROUP
DEPTH = 1
DN_ALPHA = float((2 * DEPTH) ** 0.25)

IN_SIZES = (A_WIDTH, A_KV_WIDTH, A_KV_WIDTH, A_WIDTH, B_QKV_WIDTH, B_QKV_WIDTH, B_QKV_WIDTH,
            B_GROUP_WIDTH, D_MODEL, D_MODEL)
IN_OFFSETS = tuple(int(o) for o in np.cumsum((0,) + IN_SIZES[:-1]))
(OFF_QA, OFF_KA, OFF_VA, OFF_GATE_A, OFF_QB, OFF_KB, OFF_VB, OFF_GATE_B, OFF_MG_A, OFF_MG_B) = IN_OFFSETS
D_IN = sum(IN_SIZES)

VMEM_LIMIT_BYTES = 60 * 1024 * 1024

EP_NONE, EP_ROPE64, EP_ROPE128, EP_SILU, EP_SIGMOID, EP_ROPE64_Q, EP_ROPE128_Q = range(7)

LOG2_E = float(np.log2(np.e))
LN_2 = float(np.log(2.0))
Q_SCALE_A = A_HEAD_DIM ** -0.5 * LOG2_E
Q_SCALE_B = B_HEAD_DIM ** -0.5 * LOG2_E

PROJ_TM = 1024
PROJ_TN = 2304

PK_MG_A = 0
PK_MG_B = PK_MG_A + D_MODEL
PK_VB = PK_MG_B + D_MODEL
PK_GATE_B = PK_VB + B_QKV_WIDTH
PK_QKB = PK_GATE_B + B_GROUP_WIDTH
PK_A = PK_QKB + 2 * B_QKV_WIDTH
PK_A_WIDTH = 2 * A_WIDTH + 2 * A_KV_WIDTH
A_COL_K = A_WIDTH
A_COL_V = A_COL_K + A_KV_WIDTH
A_COL_GATE = A_COL_V + A_KV_WIDTH
assert PK_A + PK_A_WIDTH == D_IN and PK_A % PK_A_WIDTH == 0 and PK_VB % B_GROUP_WIDTH == 0
assert PK_GATE_B % B_GROUP_WIDTH == 0 and PK_QKB % B_GROUP_WIDTH == 0 and D_IN % PROJ_TN == 0


def _compiler_params(semantics):
    return pltpu.CompilerParams(dimension_semantics=semantics, vmem_limit_bytes=VMEM_LIMIT_BYTES)


N_ROPE_TABLES = 4


def _bf16_pieces(t):
    hi = t.astype(BF16)
    rest = t - hi.astype(F32)
    mid = rest.astype(BF16)
    low = (rest - mid.astype(F32)).astype(BF16)
    return jnp.concatenate([hi, mid, low], axis=1)


def _rope_tables(pos_rows, freq, expand_ref, select_ref, sign_ref):
    n_chunks = pos_rows.shape[0] // 2
    eye = (lax.broadcasted_iota(jnp.int32, (LANES, LANES), 0)
           == lax.broadcasted_iota(jnp.int32, (LANES, LANES), 1))
    pos2 = []
    for c in range(n_chunks):
        diag = jnp.concatenate([jnp.where(eye, pos_rows[c:c + 1, :], 0.0),
                                jnp.where(eye, pos_rows[n_chunks + c:n_chunks + c + 1, :], 0.0)], axis=1)
        pos2.append(jnp.dot(_bf16_pieces(diag), expand_ref[...], preferred_element_type=F32))
    ang = jnp.concatenate(pos2, axis=0) * freq

    def to_head_layout(t):
        r = jnp.dot(_bf16_pieces(t), select_ref[...], preferred_element_type=F32)
        return [jnp.concatenate([r[:, (2 * layout) * LANES:(2 * layout + 1) * LANES],
                                 r[:, (2 * layout + 1) * LANES:(2 * layout + 2) * LANES]], axis=0)
                for layout in range(2)]

    cos_a, cos_b = to_head_layout(jnp.cos(ang))
    sin_a, sin_b = to_head_layout(jnp.sin(ang))
    return [t * sign_ref[ti:ti + 1, :] for ti, t in enumerate((cos_a, sin_a, cos_b, sin_b))]


def _first_half_of_head64(lane):
    return jnp.bitwise_and(lane, A_HEAD_DIM - 1) < A_HEAD_DIM // 2


def _rope_inputs(positions):
    s_len = positions.shape[1]
    half_b = B_HEAD_DIM // 2
    assert A_HEAD_DIM * 2 == B_HEAD_DIM and 2 * half_b == LANES
    inv_b = ROPE_THETA ** (-jnp.arange(half_b, dtype=F32) / half_b)
    freq = jnp.concatenate([inv_b, inv_b])[None, :]
    pos_rows = positions.astype(F32).reshape(s_len // LANES, LANES)
    lane = np.arange(LANES)
    expand = np.zeros((2 * LANES, LANES), np.float32)
    expand[:LANES, :half_b] = 1.0
    expand[LANES:, half_b:] = 1.0
    expand = np.concatenate([expand] * 3, axis=0)
    src_lane = np.stack([2 * (lane % (A_HEAD_DIM // 2)), lane % half_b])
    select = np.zeros((LANES, 4 * LANES), np.float32)
    for layout in range(2):
        for hf in range(2):
            select[hf * half_b + src_lane[layout], (2 * layout + hf) * LANES + lane] = 1.0
    select = np.concatenate([select] * 3, axis=0)
    sign = np.ones((N_ROPE_TABLES, LANES), np.float32)
    sign[1, lane % A_HEAD_DIM < A_HEAD_DIM // 2] = -1.0
    sign[3, lane < half_b] = -1.0
    return pos_rows, freq, jnp.asarray(expand, BF16), jnp.asarray(select, BF16), jnp.asarray(sign)


def _proj_kernel(tile_kinds, x_ref, w_ref, bias_ref, cos_a_ref, sin_a_ref, cos_b_ref, sin_b_ref, out_ref, xb_ref):
    n = pl.program_id(1)

    @pl.when(n == 0)
    def _():
        xb_ref[...] = x_ref[...].astype(BF16)

    def body(kinds):
        acc = jnp.dot(xb_ref[...], w_ref[...], preferred_element_type=F32)
        first_half = _first_half_of_head64(lax.broadcasted_iota(jnp.int32, (x_ref.shape[0], LANES), 1))
        for ci, kind in enumerate(kinds):
            cols = slice(ci * LANES, (ci + 1) * LANES)
            t = acc[:, cols]
            if kind in (EP_ROPE64, EP_ROPE64_Q):
                rot = jnp.where(first_half, pltpu.roll(t, 96, axis=1), pltpu.roll(t, 32, axis=1))
                t = t * cos_a_ref[...] + rot * sin_a_ref[...]
                if kind == EP_ROPE64_Q:
                    t = t * Q_SCALE_A
            elif kind in (EP_ROPE128, EP_ROPE128_Q):
                t = t * cos_b_ref[...] + pltpu.roll(t, 64, axis=1) * sin_b_ref[...]
                if kind == EP_ROPE128_Q:
                    t = t * Q_SCALE_B
            elif kind == EP_SILU:
                t = t * jax.nn.sigmoid(t)
            elif kind == EP_SIGMOID:
                t = jax.nn.sigmoid(t + bias_ref[:, cols])
            out_ref[:, cols] = t.astype(out_ref.dtype)

    branches = []
    for i, kinds in enumerate(tile_kinds):
        same = [br for br in branches if br[0] == kinds]
        if same:
            same[0][1].append(i)
        else:
            branches.append((kinds, [i]))
    for kinds, tiles in branches:
        cond = n == tiles[0]
        for i in tiles[1:]:
            cond = cond | (n == i)
        pl.when(cond)(functools.partial(body, kinds))


def _in_projection(x2d, w, bias, tables, chunk_kinds):
    s_len, k_dim = x2d.shape
    n_cols = w.shape[1]
    n_tiles = n_cols // PROJ_TN
    chunks_per_tile = PROJ_TN // LANES
    tile_kinds = tuple(tuple(chunk_kinds[t * chunks_per_tile:(t + 1) * chunks_per_tile]) for t in range(n_tiles))
    tm = min(PROJ_TM, s_len)
    row_spec = lambda width: pl.BlockSpec((tm, width), lambda m, n: (m, 0))
    return pl.pallas_call(
        functools.partial(_proj_kernel, tile_kinds),
        grid=(s_len // tm, n_tiles),
        in_specs=[row_spec(k_dim),
                  pl.BlockSpec((k_dim, PROJ_TN), lambda m, n: (0, n)),
                  pl.BlockSpec((1, PROJ_TN), lambda m, n: (0, n))] + [row_spec(LANES)] * N_ROPE_TABLES,
        out_specs=pl.BlockSpec((tm, PROJ_TN), lambda m, n: (m, n)),
        out_shape=jax.ShapeDtypeStruct((s_len, n_cols), BF16),
        scratch_shapes=[pltpu.VMEM((tm, k_dim), BF16)],
        compiler_params=_compiler_params(("arbitrary", "arbitrary")),
        name="in_projection",
    )(x2d, w, bias, *tables)


PACK_COLS = 256
PACK_BLOCKS_PER_STEP = 5
PACK_TN = PACK_BLOCKS_PER_STEP * PACK_COLS
assert D_IN % PACK_TN == 0


def _packed_layout():
    g = B_GROUP_WIDTH
    assert OFF_VA == OFF_KA + A_KV_WIDTH
    segments = [(OFF_MG_A, D_MODEL, [EP_SIGMOID]), (OFF_MG_B, D_MODEL, [EP_SIGMOID])]
    segments += [(OFF_VB, B_QKV_WIDTH, [EP_NONE]), (OFF_GATE_B, g, [EP_SILU])]
    for gi in range(B_N_GROUPS):
        segments += [(OFF_QB + gi * g, g, [EP_ROPE128_Q]), (OFF_KB + gi * g, g, [EP_ROPE128])]
    segments += [(OFF_QA, A_WIDTH, [EP_ROPE64_Q]),
                 (OFF_KA, 2 * A_KV_WIDTH, [EP_ROPE64] * (A_KV_WIDTH // LANES) + [EP_NONE] * (A_KV_WIDTH // LANES)),
                 (OFF_GATE_A, A_WIDTH, [EP_SILU])]
    perm, kinds = [], []
    for off, width, seg_kinds in segments:
        assert off % PACK_COLS == 0 and width % PACK_COLS == 0
        perm += [off // PACK_COLS + j for j in range(width // PACK_COLS)]
        kinds += seg_kinds * (width // LANES // len(seg_kinds))
    assert sorted(perm) == list(range(D_IN // PACK_COLS)) and len(kinds) == D_IN // LANES
    return np.asarray(perm, np.int32), kinds


def _pack_kernel(n_table_steps, perm_ref, pos_ref, freq_ref, expand_ref, select_ref, sign_ref, *refs):
    del perm_ref
    w_refs = refs[:PACK_BLOCKS_PER_STEP]
    out_ref = refs[PACK_BLOCKS_PER_STEP]
    table_refs = refs[PACK_BLOCKS_PER_STEP + 1:]
    for j, w_ref in enumerate(w_refs):
        out_ref[:, j * PACK_COLS:(j + 1) * PACK_COLS] = w_ref[...].astype(out_ref.dtype)

    @pl.when(pl.program_id(0) < n_table_steps)
    def _():
        tables = _rope_tables(pos_ref[...], freq_ref[...], expand_ref, select_ref, sign_ref)
        for t_ref, t in zip(table_refs, tables):
            t_ref[...] = t


def _pack_weights_and_tables(w_in, b_gate, positions):
    perm, kinds = _packed_layout()
    k_dim = w_in.shape[0]
    n_steps = D_IN // PACK_TN
    pos_rows, freq, expand, select, sign = _rope_inputs(positions)
    s_len = positions.shape[1]
    n_table_steps = n_steps - 1
    assert s_len % (8 * LANES * n_table_steps) == 0
    const = lambda a: pl.BlockSpec(a.shape, lambda i, perm_ref: (0,) * a.ndim)
    table_rows = s_len // n_table_steps
    table_block = lambda i, perm_ref: (jnp.minimum(i, n_table_steps - 1), 0)
    tab_spec = pl.BlockSpec((table_rows, LANES), table_block)
    pos_spec = pl.BlockSpec((table_rows // LANES, LANES), table_block)
    src = lambda j: pl.BlockSpec((k_dim, PACK_COLS), lambda i, perm_ref: (0, perm_ref[i * PACK_BLOCKS_PER_STEP + j]))
    tab = jax.ShapeDtypeStruct((s_len, LANES), F32)
    w, *tables = pl.pallas_call(
        functools.partial(_pack_kernel, n_table_steps),
        grid_spec=pltpu.PrefetchScalarGridSpec(
            num_scalar_prefetch=1, grid=(n_steps,),
            in_specs=[pos_spec, const(freq), const(expand), const(select), const(sign)]
            + [src(j) for j in range(PACK_BLOCKS_PER_STEP)],
            out_specs=[pl.BlockSpec((k_dim, PACK_TN), lambda i, perm_ref: (0, i))] + [tab_spec] * N_ROPE_TABLES),
        out_shape=[jax.ShapeDtypeStruct((k_dim, D_IN), BF16)] + [tab] * N_ROPE_TABLES,
        compiler_params=_compiler_params(("arbitrary",)),
        name="pack_weights",
    )(jnp.asarray(perm), pos_rows, freq, expand, select, sign, *([w_in] * PACK_BLOCKS_PER_STEP))
    bias = jnp.concatenate([b_gate[0], b_gate[1], jnp.zeros((D_IN - 2 * D_MODEL,), F32)])[None, :]
    return w, kinds, bias, tables


ATT_TOKENS = 1024
MIXER_B_TOKENS = 2048
STRIDED_MAX_DILATION = 4
PERM_ROWS = 256


def _band_bias(max_dist, reps):
    q_idx = np.arange(BLOCK)[:, None] + BLOCK
    k_idx = np.arange(2 * BLOCK)[None, :]
    dist = q_idx - k_idx
    band = (dist >= 0) & (dist <= max_dist)
    first = band & (k_idx >= BLOCK)
    both = np.stack([first, band]).astype(bool)
    bias = np.where(both, 0.0, -np.inf).astype(np.float32)
    return jnp.asarray(np.tile(bias, (1, 1, reps)))


def _block_bias(bias_ref, step, b):
    if b == 0:
        return bias_ref[jnp.where(step == 0, 0, 1)]
    return bias_ref[1]


def _swap_lane_halves(t):
    return pltpu.roll(t, LANES // 2, axis=1)


def _mixer_a_kernel(a_ref, bias_ref, out_ref, kbuf_ref, vbuf_ref):
    step = pl.program_id(0)
    tq = a_ref.shape[0]
    n_blocks = tq // BLOCK
    n_pairs = A_HEADS // 2
    pairs_per_group = n_pairs // A_KV_HEADS

    @pl.when(step == 0)
    def _():
        kbuf_ref[0:BLOCK, :] = jnp.zeros((BLOCK, A_KV_WIDTH), BF16)
        vbuf_ref[0:BLOCK, :] = jnp.zeros((BLOCK, A_KV_WIDTH), BF16)

    kbuf_ref[BLOCK:, :] = a_ref[:, A_COL_K:A_COL_K + A_KV_WIDTH]
    vbuf_ref[BLOCK:, :] = a_ref[:, A_COL_V:A_COL_V + A_KV_WIDTH]
    lane2 = lax.broadcasted_iota(jnp.int32, (2 * BLOCK, LANES), 1)
    low2 = lane2 < LANES // 2
    sink_slot = lax.broadcasted_iota(jnp.int32, (2 * BLOCK, LANES), 0) == 0
    denom_cols = jnp.concatenate([jnp.where(low2, 1.0, 0.0), jnp.where(low2, 0.0, 1.0)], axis=0).astype(BF16)

    for b in range(n_blocks):
        bias_kind = jnp.where(step == 0, 0, 1) if b == 0 else 1
        rows = slice(b * BLOCK, (b + 1) * BLOCK)
        kk = jnp.where(sink_slot, 0.0, kbuf_ref[b * BLOCK:(b + 2) * BLOCK, :].astype(F32))
        vv = jnp.where(sink_slot, 0.0, vbuf_ref[b * BLOCK:(b + 2) * BLOCK, :].astype(F32))
        kk_sw = _swap_lane_halves(kk)
        vv_sw = _swap_lane_halves(vv)
        k2, v2 = [], []
        for g in range(A_KV_HEADS):
            if g == 0:
                k_top, k_bot = jnp.where(low2, kk, 0.0), jnp.where(low2, 0.0, kk_sw)
                v_top, v_bot = jnp.where(low2, vv, 0.0), jnp.where(low2, 0.0, vv_sw)
            else:
                k_top, k_bot = jnp.where(low2, kk_sw, 0.0), jnp.where(low2, 0.0, kk)
                v_top, v_bot = jnp.where(low2, vv_sw, 0.0), jnp.where(low2, 0.0, vv)
            k2.append(jnp.concatenate([k_top, k_bot], axis=0).astype(BF16))
            v2.append(jnp.concatenate([jnp.concatenate([v_top, v_bot], axis=0).astype(BF16), denom_cols], axis=1))
        for p in range(n_pairs):
            g = p // pairs_per_group
            cols = slice(p * LANES, (p + 1) * LANES)
            s = (lax.dot_general(a_ref[rows, cols], k2[g], (((1,), (1,)), ((), ())), preferred_element_type=F32)
                 + bias_ref[bias_kind, p])
            m0b = jnp.broadcast_to(jnp.max(s[:, :2 * BLOCK], axis=1, keepdims=True), (BLOCK, LANES))
            m1b = jnp.broadcast_to(jnp.max(s[:, 2 * BLOCK:], axis=1, keepdims=True), (BLOCK, LANES))
            shifts = (m0b, m0b, m1b, m1b)
            prob = jnp.concatenate([jnp.exp2(s[:, j * LANES:(j + 1) * LANES] - shifts[j]) for j in range(4)],
                                   axis=1).astype(BF16)
            o2 = jnp.dot(prob, v2[g], preferred_element_type=F32)
            o = o2[:, :LANES] / o2[:, LANES:]
            gate = a_ref[rows, A_COL_GATE + p * LANES:A_COL_GATE + (p + 1) * LANES].astype(F32)
            out_ref[rows, cols] = (o * gate).astype(out_ref.dtype)

    kbuf_ref[0:BLOCK, :] = kbuf_ref[tq:tq + BLOCK, :]
    vbuf_ref[0:BLOCK, :] = vbuf_ref[tq:tq + BLOCK, :]


def _mixer_a(h, sinks):
    s_len = h.shape[0]
    tq = ATT_TOKENS
    n_pairs = A_HEADS // 2
    band = _band_bias(A_WINDOW - 1, 2)
    sink_pairs = (sinks.astype(F32) * LOG2_E).reshape(n_pairs, 2)
    col = jnp.arange(4 * BLOCK)
    bias = jnp.broadcast_to(band[:, None], (2, n_pairs, BLOCK, 4 * BLOCK))
    bias = jnp.where(col == 0, sink_pairs[None, :, 0, None, None], bias)
    bias = jnp.where(col == 2 * BLOCK, sink_pairs[None, :, 1, None, None], bias)
    return pl.pallas_call(
        _mixer_a_kernel,
        grid=(s_len // tq,),
        in_specs=[
            pl.BlockSpec((tq, PK_A_WIDTH), lambda i: (i, PK_A // PK_A_WIDTH)),
            pl.BlockSpec(bias.shape, lambda i: (0, 0, 0, 0), pipeline_mode=pl.Buffered(1)),
        ],
        out_specs=pl.BlockSpec((tq, A_WIDTH), lambda i: (i, 0)),
        out_shape=jax.ShapeDtypeStruct((s_len, A_WIDTH), BF16),
        scratch_shapes=[pltpu.VMEM((tq + BLOCK, A_KV_WIDTH), BF16), pltpu.VMEM((tq + BLOCK, A_KV_WIDTH), BF16)],
        compiler_params=_compiler_params(("arbitrary",)),
        name="mixer_a",
    )(h, bias)


def _mixer_b_kernel(d, nb, q_ref, k_ref, v_ref, bias_ref, *refs):
    by_mxu = d > STRIDED_MAX_DILATION
    if d == 1:
        o_ref, lse_ref, qsub, ksub, vsub = refs
    elif by_mxu:
        perm_ref, perm_t_ref, o_ref, lse_ref, qsub, ksub, vsub, ystage, lstage = refs
    else:
        o_ref, lse_ref, qsub, ksub, vsub, slab, ostage, lstage = refs
    step = pl.program_id(0)
    n_heads = B_HEADS_PER_GROUP
    span = BLOCK * d

    @pl.when(step == 0)
    def _():
        ksub[:, 0:BLOCK, :] = jnp.zeros((d, BLOCK, B_GROUP_WIDTH), BF16)
        vsub[:, 0:BLOCK, :] = jnp.zeros((d, BLOCK, B_GROUP_WIDTH), BF16)

    if d == 1:
        qsub[0] = q_ref[...]
        ksub[0, BLOCK:, :] = k_ref[...]
        vsub[0, BLOCK:, :] = v_ref[...]
    elif not by_mxu:
        for src, dst, row0 in ((q_ref, qsub, 0), (k_ref, ksub, BLOCK), (v_ref, vsub, BLOCK)):
            for c in range(n_heads):
                cols = slice(c * LANES, (c + 1) * LANES)
                sl = slab.at[c]
                sl[...] = src[:, cols].astype(F32)
                for r in range(d):
                    for b in range(nb):
                        piece = sl[pl.ds(b * span + r, BLOCK, stride=d), :]
                        dst[r, row0 + b * BLOCK:row0 + (b + 1) * BLOCK, cols] = piece.astype(BF16)
    else:
        per = PERM_ROWS // d
        n_perm_blocks = nb * span // PERM_ROWS
        for src, dst, row0 in ((q_ref, qsub, 0), (k_ref, ksub, BLOCK), (v_ref, vsub, BLOCK)):
            for j in range(n_perm_blocks):
                y = jnp.dot(perm_ref[...], src[j * PERM_ROWS:(j + 1) * PERM_ROWS, :],
                            preferred_element_type=F32).astype(BF16)
                for r in range(d):
                    dst[r, row0 + j * per:row0 + (j + 1) * per, :] = y[r * per:(r + 1) * per, :]

    ones = jnp.ones((2 * BLOCK, LANES), BF16)
    lane = lax.broadcasted_iota(jnp.int32, (BLOCK, LANES), 1)
    for r in range(d):
        for b in range(nb):
            bias = _block_bias(bias_ref, step, b)
            rows = slice(b * BLOCK, (b + 1) * BLOCK)
            lse_tile = jnp.zeros((BLOCK, LANES), F32)
            for h in range(n_heads):
                cols = slice(h * B_HEAD_DIM, (h + 1) * B_HEAD_DIM)
                kk = ksub[r, b * BLOCK:(b + 2) * BLOCK, cols]
                v2 = jnp.concatenate([vsub[r, b * BLOCK:(b + 2) * BLOCK, cols], ones], axis=1)
                s = lax.dot_general(qsub[r, rows, cols], kk, (((1,), (1,)), ((), ())),
                                    preferred_element_type=F32) + bias
                m = jnp.max(s, axis=1, keepdims=True)
                mb = jnp.broadcast_to(m, (BLOCK, LANES))
                prob = jnp.concatenate([jnp.exp2(s[:, :LANES] - mb), jnp.exp2(s[:, LANES:] - mb)],
                                       axis=1).astype(BF16)
                o2 = jnp.dot(prob, v2, preferred_element_type=F32)
                denom = o2[:, LANES:]
                o = o2[:, :LANES] / denom
                lse_h = mb * LN_2 + jnp.log(denom)
                in_head = jnp.logical_and(lane >= h * B_LSE_LANES, lane < (h + 1) * B_LSE_LANES)
                lse_tile = jnp.where(in_head, lse_h, lse_tile)
                if d == 1:
                    o_ref[rows, cols] = o.astype(o_ref.dtype)
                elif not by_mxu:
                    ostage[h, pl.ds(b * span + r, BLOCK, stride=d), :] = o
                else:
                    o = o.astype(BF16)
                    for jj in range(BLOCK // per):
                        ystage[b * (BLOCK // per) + jj, r * per:(r + 1) * per, cols] = o[jj * per:(jj + 1) * per, :]
            if d == 1:
                lse_ref[rows, :] = lse_tile
            else:
                lstage[pl.ds(b * span + r, BLOCK, stride=d), :] = lse_tile

    if d > 1:
        if by_mxu:
            for j in range(n_perm_blocks):
                o_ref[j * PERM_ROWS:(j + 1) * PERM_ROWS, :] = jnp.dot(
                    perm_t_ref[...], ystage[j], preferred_element_type=F32).astype(o_ref.dtype)
        else:
            for h in range(n_heads):
                o_ref[:, h * B_HEAD_DIM:(h + 1) * B_HEAD_DIM] = ostage[h].astype(o_ref.dtype)
        lse_ref[...] = lstage[...]
    ksub[:, 0:BLOCK, :] = ksub[:, nb * BLOCK:(nb + 1) * BLOCK, :]
    vsub[:, 0:BLOCK, :] = vsub[:, nb * BLOCK:(nb + 1) * BLOCK, :]


def _mixer_b_group(h, gi):
    s_len = h.shape[0]
    window, d = B_PATTERNS[gi]
    assert window // d == BLOCK
    nb = max(1, MIXER_B_TOKENS // (BLOCK * d))
    t_rows = nb * BLOCK * d
    gw = B_GROUP_WIDTH
    col_q, col_v = (PK_QKB + gi * 2 * gw) // gw, (PK_VB + gi * gw) // gw
    bias = _band_bias(BLOCK, 1)
    blk = lambda c: pl.BlockSpec((t_rows, gw), lambda i: (i, c))
    scratch = [pltpu.VMEM((d, nb * BLOCK, gw), BF16), pltpu.VMEM((d, (nb + 1) * BLOCK, gw), BF16),
               pltpu.VMEM((d, (nb + 1) * BLOCK, gw), BF16)]
    operands = [h, h, h, bias]
    in_specs = [blk(col_q), blk(col_q + 1), blk(col_v), pl.BlockSpec(bias.shape, lambda i: (0, 0, 0))]
    if 1 < d <= STRIDED_MAX_DILATION:
        scratch += [pltpu.VMEM((B_HEADS_PER_GROUP, t_rows, LANES), F32),
                    pltpu.VMEM((B_HEADS_PER_GROUP, t_rows, LANES), F32), pltpu.VMEM((t_rows, LANES), F32)]
    elif d > 1:
        assert PERM_ROWS % d == 0 and (PERM_ROWS // d) % 16 == 0 and BLOCK % (PERM_ROWS // d) == 0
        per = PERM_ROWS // d
        perm = np.zeros((PERM_ROWS, PERM_ROWS), np.float32)
        l_idx, r_idx = np.meshgrid(np.arange(per), np.arange(d), indexing="ij")
        perm[r_idx * per + l_idx, l_idx * d + r_idx] = 1.0
        operands += [jnp.asarray(perm, BF16), jnp.asarray(perm.T, BF16)]
        in_specs += [pl.BlockSpec((PERM_ROWS, PERM_ROWS), lambda i: (0, 0))] * 2
        scratch += [pltpu.VMEM((t_rows // PERM_ROWS, PERM_ROWS, gw), BF16), pltpu.VMEM((t_rows, LANES), F32)]
    return pl.pallas_call(
        functools.partial(_mixer_b_kernel, d, nb),
        grid=(s_len // t_rows,),
        in_specs=in_specs,
        out_specs=[pl.BlockSpec((t_rows, gw), lambda i: (i, 0)), pl.BlockSpec((t_rows, LANES), lambda i: (i, 0))],
        out_shape=[jax.ShapeDtypeStruct((s_len, gw), BF16), jax.ShapeDtypeStruct((s_len, LANES), F32)],
        scratch_shapes=scratch,
        compiler_params=_compiler_params(("arbitrary",)),
        name=f"mixer_b_d{d}",
    )(*operands)


TAIL_TM = 512


def _tail_kernel(x_ref, ya_ref, o0_ref, o1_ref, o2_ref, l0_ref, l1_ref, l2_ref, gate_b_ref, sig_a_ref, sig_b_ref,
                 w_pa_ref, w_pb_ref, w_out_ref, ln_g_ref, ln_b_ref, out_ref):
    tm = x_ref.shape[0]
    outs = [o0_ref, o1_ref, o2_ref]
    lses = [l0_ref[...], l1_ref[...], l2_ref[...]]
    m = jnp.maximum(jnp.maximum(lses[0], lses[1]), lses[2])
    es = [jnp.exp(l - m) for l in lses]
    inv = 1.0 / (es[0] + es[1] + es[2])
    wts = [e * inv for e in es]
    yb_cols = []
    for h in range(B_HEADS_PER_GROUP):
        cols = slice(h * B_HEAD_DIM, (h + 1) * B_HEAD_DIM)
        acc = None
        for gi in range(B_N_GROUPS):
            w_h = jnp.broadcast_to(wts[gi][:, h * B_LSE_LANES:h * B_LSE_LANES + 1], (tm, B_HEAD_DIM))
            term = w_h * outs[gi][:, cols].astype(F32)
            acc = term if acc is None else acc + term
        yb_cols.append((acc * gate_b_ref[:, cols].astype(F32)).astype(BF16))
    yb = jnp.concatenate(yb_cols, axis=1)
    y_a = jnp.dot(ya_ref[...], w_pa_ref[...], preferred_element_type=F32)
    y_b = jnp.dot(yb, w_pb_ref[...], preferred_element_type=F32)
    merged = sig_a_ref[...].astype(F32) * y_a + sig_b_ref[...].astype(F32) * y_b
    sub = jnp.dot(merged.astype(BF16), w_out_ref[...], preferred_element_type=F32)
    z = DN_ALPHA * x_ref[...] + sub
    inv_d = 1.0 / z.shape[-1]
    mu = jnp.sum(z, axis=-1, keepdims=True) * inv_d
    var = jnp.sum(z * z, axis=-1, keepdims=True) * inv_d - mu * mu
    out_ref[...] = ((z - mu) * lax.rsqrt(var + LN_EPS) * ln_g_ref[...] + ln_b_ref[...]).astype(out_ref.dtype)


def _tail(x2d, ya, o_groups, lse_groups, h, w_pa, w_pb, w_out, ln_g, ln_b):
    s_len = x2d.shape[0]
    tm = TAIL_TM
    row = lambda width, cblk=0: pl.BlockSpec((tm, width), lambda i: (i, cblk))
    full = lambda a: pl.BlockSpec(a.shape, lambda i: (0,) * a.ndim, pipeline_mode=pl.Buffered(1))
    gw = B_GROUP_WIDTH
    return pl.pallas_call(
        _tail_kernel,
        grid=(s_len // tm,),
        in_specs=[row(D_MODEL), row(A_WIDTH), row(gw), row(gw), row(gw), row(LANES), row(LANES), row(LANES),
                  row(gw, PK_GATE_B // gw), row(D_MODEL, PK_MG_A // D_MODEL), row(D_MODEL, PK_MG_B // D_MODEL),
                  full(w_pa), full(w_pb), full(w_out), full(ln_g), full(ln_b)],
        out_specs=row(D_MODEL),
        out_shape=jax.ShapeDtypeStruct((s_len, D_MODEL), x2d.dtype),
        compiler_params=_compiler_params(("arbitrary",)),
        name="tail",
    )(x2d, ya, *o_groups, *lse_groups, h, h, h, w_pa, w_pb, w_out, ln_g, ln_b)


def _hybrid_layer(x, positions, w_in, b_gate, sinks, w_pa, w_pb, w_out, ln_g, ln_b):
    bn, s_len, d_model = x.shape
    assert bn == 1 and d_model == D_MODEL
    assert s_len % (BLOCK * B_PATTERNS[-1][1]) == 0 and s_len % PROJ_TM == 0
    x2d = x.reshape(s_len, d_model)
    w, kinds, bias, tables = _pack_weights_and_tables(w_in, b_gate, positions)
    h = _in_projection(x2d, w, bias, tables, kinds)
    o_groups, lse_groups = [], []
    for gi in range(B_N_GROUPS):
        o, lse = _mixer_b_group(h, gi)
        o_groups.append(o)
        lse_groups.append(lse)
    ya = _mixer_a(h, sinks)
    out = _tail(x2d, ya, o_groups, lse_groups, h, w_pa.astype(BF16), w_pb.astype(BF16), w_out.astype(BF16),
                ln_g.reshape(1, d_model).astype(F32), ln_b.reshape(1, d_model).astype(F32))
    return out.reshape(bn, s_len, d_model)


def kernel(x, positions, w_in, b_gate, sinks, w_pa, w_pb, w_out, ln_g, ln_b):
    for layer in range(w_in.shape[0]):
        x = _hybrid_layer(x, positions, w_in[layer], b_gate[layer], sinks[layer], w_pa[layer], w_pb[layer],
                          w_out[layer], ln_g[layer], ln_b[layer])
    return x
```

```python
import functools

import jax
import jax.numpy as jnp
import numpy as np
from jax import lax
from jax.experimental import pallas as pl
from jax.experimental.pallas import tpu as pltpu

F32 = jnp.float32
BF16 = jnp.bfloat16

D_MODEL = 2048
ROPE_THETA = 10000.0
LN_EPS = 1e-5
BLOCK = 128
LANES = 128
A_HEADS = 16
A_KV_HEADS = 2
A_HEAD_DIM = 64
A_WINDOW = 128
A_WIDTH = A_HEADS * A_HEAD_DIM
A_KV_WIDTH = A_KV_HEADS * A_HEAD_DIM
B_PATTERNS = ((128, 1), (512, 4), (2048, 16))
B_HEADS_PER_GROUP = 4
B_HEAD_DIM = 128
B_GROUP_WIDTH = B_HEADS_PER_GROUP * B_HEAD_DIM
B_N_GROUPS = len(B_PATTERNS)
B_QKV_WIDTH = B_N_GROUPS * B_GROUP_WIDTH
B_LSE_LANES = LANES // B_HEADS_PER_GROUP
DEPTH = 1
DN_ALPHA = float((2 * DEPTH) ** 0.25)

IN_SIZES = (A_WIDTH, A_KV_WIDTH, A_KV_WIDTH, A_WIDTH, B_QKV_WIDTH, B_QKV_WIDTH, B_QKV_WIDTH,
            B_GROUP_WIDTH, D_MODEL, D_MODEL)
IN_OFFSETS = tuple(int(o) for o in np.cumsum((0,) + IN_SIZES[:-1]))
(OFF_QA, OFF_KA, OFF_VA, OFF_GATE_A, OFF_QB, OFF_KB, OFF_VB, OFF_GATE_B, OFF_MG_A, OFF_MG_B) = IN_OFFSETS
D_IN = sum(IN_SIZES)

VMEM_LIMIT_BYTES = 60 * 1024 * 1024

EP_NONE, EP_ROPE64, EP_ROPE128, EP_SILU, EP_SIGMOID, EP_ROPE64_Q, EP_ROPE128_Q = range(7)

LOG2_E = float(np.log2(np.e))
LN_2 = float(np.log(2.0))
Q_SCALE_A = A_HEAD_DIM ** -0.5 * LOG2_E
Q_SCALE_B = B_HEAD_DIM ** -0.5 * LOG2_E

PROJ_TM = 1024
PROJ_TN = 2304

PK_MG_A = 0
PK_MG_B = PK_MG_A + D_MODEL
PK_VB = PK_MG_B + D_MODEL
PK_GATE_B = PK_VB + B_QKV_WIDTH
PK_QKB = PK_GATE_B + B_GROUP_WIDTH
PK_A = PK_QKB + 2 * B_QKV_WIDTH
PK_A_WIDTH = 2 * A_WIDTH + 2 * A_KV_WIDTH
A_COL_K = A_WIDTH
A_COL_V = A_COL_K + A_KV_WIDTH
A_COL_GATE = A_COL_V + A_KV_WIDTH
assert PK_A + PK_A_WIDTH == D_IN and PK_A % PK_A_WIDTH == 0 and PK_VB % B_GROUP_WIDTH == 0
assert PK_GATE_B % B_GROUP_WIDTH == 0 and PK_QKB % B_GROUP_WIDTH == 0 and D_IN % PROJ_TN == 0


def _compiler_params(semantics):
    return pltpu.CompilerParams(dimension_semantics=semantics, vmem_limit_bytes=VMEM_LIMIT_BYTES)


N_ROPE_TABLES = 4


def _bf16_pieces(t):
    hi = t.astype(BF16)
    rest = t - hi.astype(F32)
    mid = rest.astype(BF16)
    low = (rest - mid.astype(F32)).astype(BF16)
    return jnp.concatenate([hi, mid, low], axis=1)


def _rope_tables(pos_rows, freq, expand_ref, select_ref, sign_ref):
    n_chunks = pos_rows.shape[0] // 2
    eye = (lax.broadcasted_iota(jnp.int32, (LANES, LANES), 0)
           == lax.broadcasted_iota(jnp.int32, (LANES, LANES), 1))
    pos2 = []
    for c in range(n_chunks):
        diag = jnp.concatenate([jnp.where(eye, pos_rows[c:c + 1, :], 0.0),
                                jnp.where(eye, pos_rows[n_chunks + c:n_chunks + c + 1, :], 0.0)], axis=1)
        pos2.append(jnp.dot(_bf16_pieces(diag), expand_ref[...], preferred_element_type=F32))
    ang = jnp.concatenate(pos2, axis=0) * freq

    def to_head_layout(t):
        r = jnp.dot(_bf16_pieces(t), select_ref[...], preferred_element_type=F32)
        return [jnp.concatenate([r[:, (2 * layout) * LANES:(2 * layout + 1) * LANES],
                                 r[:, (2 * layout + 1) * LANES:(2 * layout + 2) * LANES]], axis=0)
                for layout in range(2)]

    cos_a, cos_b = to_head_layout(jnp.cos(ang))
    sin_a, sin_b = to_head_layout(jnp.sin(ang))
    return [t * sign_ref[ti:ti + 1, :] for ti, t in enumerate((cos_a, sin_a, cos_b, sin_b))]


def _first_half_of_head64(lane):
    return jnp.bitwise_and(lane, A_HEAD_DIM - 1) < A_HEAD_DIM // 2


def _rope_inputs(positions):
    s_len = positions.shape[1]
    half_b = B_HEAD_DIM // 2
    assert A_HEAD_DIM * 2 == B_HEAD_DIM and 2 * half_b == LANES
    inv_b = ROPE_THETA ** (-jnp.arange(half_b, dtype=F32) / half_b)
    freq = jnp.concatenate([inv_b, inv_b])[None, :]
    pos_rows = positions.astype(F32).reshape(s_len // LANES, LANES)
    lane = np.arange(LANES)
    expand = np.zeros((2 * LANES, LANES), np.float32)
    expand[:LANES, :half_b] = 1.0
    expand[LANES:, half_b:] = 1.0
    expand = np.concatenate([expand] * 3, axis=0)
    src_lane = np.stack([2 * (lane % (A_HEAD_DIM // 2)), lane % half_b])
    select = np.zeros((LANES, 4 * LANES), np.float32)
    for layout in range(2):
        for hf in range(2):
            select[hf * half_b + src_lane[layout], (2 * layout + hf) * LANES + lane] = 1.0
    select = np.concatenate([select] * 3, axis=0)
    sign = np.ones((N_ROPE_TABLES, LANES), np.float32)
    sign[1, lane % A_HEAD_DIM < A_HEAD_DIM // 2] = -1.0
    sign[3, lane < half_b] = -1.0
    return pos_rows, freq, jnp.asarray(expand, BF16), jnp.asarray(select, BF16), jnp.asarray(sign)


def _sigmoid(z):
    return 0.5 * jnp.tanh(0.5 * z) + 0.5


def _proj_kernel(tile_kinds, x_ref, w_ref, bias_ref, cos_a_ref, sin_a_ref, cos_b_ref, sin_b_ref, out_ref, xb_ref):
    n = pl.program_id(1)

    @pl.when(n == 0)
    def _():
        xb_ref[...] = x_ref[...].astype(BF16)

    def body(kinds):
        acc = jnp.dot(xb_ref[...], w_ref[...], preferred_element_type=F32)
        first_half = _first_half_of_head64(lax.broadcasted_iota(jnp.int32, (x_ref.shape[0], LANES), 1))
        for ci, kind in enumerate(kinds):
            cols = slice(ci * LANES, (ci + 1) * LANES)
            t = acc[:, cols]
            if kind in (EP_ROPE64, EP_ROPE64_Q):
                rot = jnp.where(first_half, pltpu.roll(t, 96, axis=1), pltpu.roll(t, 32, axis=1))
                t = t * cos_a_ref[...] + rot * sin_a_ref[...]
                if kind == EP_ROPE64_Q:
                    t = t * Q_SCALE_A
            elif kind in (EP_ROPE128, EP_ROPE128_Q):
                t = t * cos_b_ref[...] + pltpu.roll(t, 64, axis=1) * sin_b_ref[...]
                if kind == EP_ROPE128_Q:
                    t = t * Q_SCALE_B
            elif kind == EP_SILU:
                t = t * _sigmoid(t)
            elif kind == EP_SIGMOID:
                t = _sigmoid(t + bias_ref[:, cols])
            out_ref[:, cols] = t.astype(out_ref.dtype)

    branches = []
    for i, kinds in enumerate(tile_kinds):
        same = [br for br in branches if br[0] == kinds]
        if same:
            same[0][1].append(i)
        else:
            branches.append((kinds, [i]))
    for kinds, tiles in branches:
        cond = n == tiles[0]
        for i in tiles[1:]:
            cond = cond | (n == i)
        pl.when(cond)(functools.partial(body, kinds))


def _in_projection(x2d, w, bias, tables, chunk_kinds):
    s_len, k_dim = x2d.shape
    n_cols = w.shape[1]
    n_tiles = n_cols // PROJ_TN
    chunks_per_tile = PROJ_TN // LANES
    tile_kinds = tuple(tuple(chunk_kinds[t * chunks_per_tile:(t + 1) * chunks_per_tile]) for t in range(n_tiles))
    tm = min(PROJ_TM, s_len)
    row_spec = lambda width: pl.BlockSpec((tm, width), lambda m, n: (m, 0))
    return pl.pallas_call(
        functools.partial(_proj_kernel, tile_kinds),
        grid=(s_len // tm, n_tiles),
        in_specs=[row_spec(k_dim),
                  pl.BlockSpec((k_dim, PROJ_TN), lambda m, n: (0, n)),
                  pl.BlockSpec((1, PROJ_TN), lambda m, n: (0, n))] + [row_spec(LANES)] * N_ROPE_TABLES,
        out_specs=pl.BlockSpec((tm, PROJ_TN), lambda m, n: (m, n)),
        out_shape=jax.ShapeDtypeStruct((s_len, n_cols), BF16),
        scratch_shapes=[pltpu.VMEM((tm, k_dim), BF16)],
        compiler_params=_compiler_params(("arbitrary", "arbitrary")),
        name="in_projection",
    )(x2d, w, bias, *tables)


PACK_COLS = 256
PACK_BLOCKS_PER_STEP = 5
PACK_TN = PACK_BLOCKS_PER_STEP * PACK_COLS
assert D_IN % PACK_TN == 0


def _packed_layout():
    g = B_GROUP_WIDTH
    assert OFF_VA == OFF_KA + A_KV_WIDTH
    segments = [(OFF_MG_A, D_MODEL, [EP_SIGMOID]), (OFF_MG_B, D_MODEL, [EP_SIGMOID])]
    segments += [(OFF_VB, B_QKV_WIDTH, [EP_NONE]), (OFF_GATE_B, g, [EP_SILU])]
    for gi in range(B_N_GROUPS):
        segments += [(OFF_QB + gi * g, g, [EP_ROPE128_Q]), (OFF_KB + gi * g, g, [EP_ROPE128])]
    segments += [(OFF_QA, A_WIDTH, [EP_ROPE64_Q]),
                 (OFF_KA, 2 * A_KV_WIDTH, [EP_ROPE64] * (A_KV_WIDTH // LANES) + [EP_NONE] * (A_KV_WIDTH // LANES)),
                 (OFF_GATE_A, A_WIDTH, [EP_SILU])]
    perm, kinds = [], []
    for off, width, seg_kinds in segments:
        assert off % PACK_COLS == 0 and width % PACK_COLS == 0
        perm += [off // PACK_COLS + j for j in range(width // PACK_COLS)]
        kinds += seg_kinds * (width // LANES // len(seg_kinds))
    assert sorted(perm) == list(range(D_IN // PACK_COLS)) and len(kinds) == D_IN // LANES
    return np.asarray(perm, np.int32), kinds


def _pack_kernel(n_table_steps, perm_ref, pos_ref, freq_ref, expand_ref, select_ref, sign_ref, *refs):
    del perm_ref
    w_refs = refs[:PACK_BLOCKS_PER_STEP]
    out_ref = refs[PACK_BLOCKS_PER_STEP]
    table_refs = refs[PACK_BLOCKS_PER_STEP + 1:]
    for j, w_ref in enumerate(w_refs):
        out_ref[:, j * PACK_COLS:(j + 1) * PACK_COLS] = w_ref[...].astype(out_ref.dtype)

    @pl.when(pl.program_id(0) < n_table_steps)
    def _():
        tables = _rope_tables(pos_ref[...], freq_ref[...], expand_ref, select_ref, sign_ref)
        for t_ref, t in zip(table_refs, tables):
            t_ref[...] = t


def _pack_weights_and_tables(w_in, b_gate, positions):
    perm, kinds = _packed_layout()
    k_dim = w_in.shape[0]
    n_steps = D_IN // PACK_TN
    pos_rows, freq, expand, select, sign = _rope_inputs(positions)
    s_len = positions.shape[1]
    n_table_steps = n_steps - 1
    assert s_len % (8 * LANES * n_table_steps) == 0
    const = lambda a: pl.BlockSpec(a.shape, lambda i, perm_ref: (0,) * a.ndim)
    table_rows = s_len // n_table_steps
    table_block = lambda i, perm_ref: (jnp.minimum(i, n_table_steps - 1), 0)
    tab_spec = pl.BlockSpec((table_rows, LANES), table_block)
    pos_spec = pl.BlockSpec((table_rows // LANES, LANES), table_block)
    src = lambda j: pl.BlockSpec((k_dim, PACK_COLS), lambda i, perm_ref: (0, perm_ref[i * PACK_BLOCKS_PER_STEP + j]))
    tab = jax.ShapeDtypeStruct((s_len, LANES), F32)
    w, *tables = pl.pallas_call(
        functools.partial(_pack_kernel, n_table_steps),
        grid_spec=pltpu.PrefetchScalarGridSpec(
            num_scalar_prefetch=1, grid=(n_steps,),
            in_specs=[pos_spec, const(freq), const(expand), const(select), const(sign)]
            + [src(j) for j in range(PACK_BLOCKS_PER_STEP)],
            out_specs=[pl.BlockSpec((k_dim, PACK_TN), lambda i, perm_ref: (0, i))] + [tab_spec] * N_ROPE_TABLES),
        out_shape=[jax.ShapeDtypeStruct((k_dim, D_IN), BF16)] + [tab] * N_ROPE_TABLES,
        compiler_params=_compiler_params(("arbitrary",)),
        name="pack_weights",
    )(jnp.asarray(perm), pos_rows, freq, expand, select, sign, *([w_in] * PACK_BLOCKS_PER_STEP))
    bias = jnp.concatenate([b_gate[0], b_gate[1], jnp.zeros((D_IN - 2 * D_MODEL,), F32)])[None, :]
    return w, kinds, bias, tables


ATT_TOKENS = 1024
MIXER_B_TOKENS = 2048
STRIDED_MAX_DILATION = 4
PERM_ROWS = 256


def _band_bias(max_dist, reps):
    q_idx = np.arange(BLOCK)[:, None] + BLOCK
    k_idx = np.arange(2 * BLOCK)[None, :]
    dist = q_idx - k_idx
    band = (dist >= 0) & (dist <= max_dist)
    first = band & (k_idx >= BLOCK)
    both = np.stack([first, band]).astype(bool)
    bias = np.where(both, 0.0, -np.inf).astype(np.float32)
    return jnp.asarray(np.tile(bias, (1, 1, reps)))


def _block_bias(bias_ref, step, b):
    if b == 0:
        return bias_ref[jnp.where(step == 0, 0, 1)]
    return bias_ref[1]


def _swap_lane_halves(t):
    return pltpu.roll(t, LANES // 2, axis=1)


def _mixer_a_kernel(a_ref, bias_ref, out_ref, kbuf_ref, vbuf_ref):
    step = pl.program_id(0)
    tq = a_ref.shape[0]
    n_blocks = tq // BLOCK
    n_pairs = A_HEADS // 2
    pairs_per_group = n_pairs // A_KV_HEADS

    @pl.when(step == 0)
    def _():
        kbuf_ref[0:BLOCK, :] = jnp.zeros((BLOCK, A_KV_WIDTH), BF16)
        vbuf_ref[0:BLOCK, :] = jnp.zeros((BLOCK, A_KV_WIDTH), BF16)

    kbuf_ref[BLOCK:, :] = a_ref[:, A_COL_K:A_COL_K + A_KV_WIDTH]
    vbuf_ref[BLOCK:, :] = a_ref[:, A_COL_V:A_COL_V + A_KV_WIDTH]
    lane2 = lax.broadcasted_iota(jnp.int32, (2 * BLOCK, LANES), 1)
    low2 = lane2 < LANES // 2
    sink_slot = lax.broadcasted_iota(jnp.int32, (2 * BLOCK, LANES), 0) == 0
    denom_cols = jnp.concatenate([jnp.where(low2, 1.0, 0.0), jnp.where(low2, 0.0, 1.0)], axis=0).astype(BF16)

    for b in range(n_blocks):
        bias_kind = jnp.where(step == 0, 0, 1) if b == 0 else 1
        rows = slice(b * BLOCK, (b + 1) * BLOCK)
        kk = jnp.where(sink_slot, 0.0, kbuf_ref[b * BLOCK:(b + 2) * BLOCK, :].astype(F32))
        vv = jnp.where(sink_slot, 0.0, vbuf_ref[b * BLOCK:(b + 2) * BLOCK, :].astype(F32))
        kk_sw = _swap_lane_halves(kk)
        vv_sw = _swap_lane_halves(vv)
        k2, v2 = [], []
        for g in range(A_KV_HEADS):
            if g == 0:
                k_top, k_bot = jnp.where(low2, kk, 0.0), jnp.where(low2, 0.0, kk_sw)
                v_top, v_bot = jnp.where(low2, vv, 0.0), jnp.where(low2, 0.0, vv_sw)
            else:
                k_top, k_bot = jnp.where(low2, kk_sw, 0.0), jnp.where(low2, 0.0, kk)
                v_top, v_bot = jnp.where(low2, vv_sw, 0.0), jnp.where(low2, 0.0, vv)
            k2.append(jnp.concatenate([k_top, k_bot], axis=0).astype(BF16))
            v2.append(jnp.concatenate([jnp.concatenate([v_top, v_bot], axis=0).astype(BF16), denom_cols], axis=1))
        for p in range(n_pairs):
            g = p // pairs_per_group
            cols = slice(p * LANES, (p + 1) * LANES)
            s = (lax.dot_general(a_ref[rows, cols], k2[g], (((1,), (1,)), ((), ())), preferred_element_type=F32)
                 + bias_ref[bias_kind, p])
            m0b = jnp.broadcast_to(jnp.max(s[:, :2 * BLOCK], axis=1, keepdims=True), (BLOCK, LANES))
            m1b = jnp.broadcast_to(jnp.max(s[:, 2 * BLOCK:], axis=1, keepdims=True), (BLOCK, LANES))
            shifts = (m0b, m0b, m1b, m1b)
            prob = jnp.concatenate([jnp.exp2(s[:, j * LANES:(j + 1) * LANES] - shifts[j]) for j in range(4)],
                                   axis=1).astype(BF16)
            o2 = jnp.dot(prob, v2[g], preferred_element_type=F32)
            o = o2[:, :LANES] / o2[:, LANES:]
            gate = a_ref[rows, A_COL_GATE + p * LANES:A_COL_GATE + (p + 1) * LANES].astype(F32)
            out_ref[rows, cols] = (o * gate).astype(out_ref.dtype)

    kbuf_ref[0:BLOCK, :] = kbuf_ref[tq:tq + BLOCK, :]
    vbuf_ref[0:BLOCK, :] = vbuf_ref[tq:tq + BLOCK, :]


def _mixer_a(h, sinks):
    s_len = h.shape[0]
    tq = ATT_TOKENS
    n_pairs = A_HEADS // 2
    band = _band_bias(A_WINDOW - 1, 2)
    sink_pairs = (sinks.astype(F32) * LOG2_E).reshape(n_pairs, 2)
    col = jnp.arange(4 * BLOCK)
    bias = jnp.broadcast_to(band[:, None], (2, n_pairs, BLOCK, 4 * BLOCK))
    bias = jnp.where(col == 0, sink_pairs[None, :, 0, None, None], bias)
    bias = jnp.where(col == 2 * BLOCK, sink_pairs[None, :, 1, None, None], bias)
    return pl.pallas_call(
        _mixer_a_kernel,
        grid=(s_len // tq,),
        in_specs=[
            pl.BlockSpec((tq, PK_A_WIDTH), lambda i: (i, PK_A // PK_A_WIDTH)),
            pl.BlockSpec(bias.shape, lambda i: (0, 0, 0, 0), pipeline_mode=pl.Buffered(1)),
        ],
        out_specs=pl.BlockSpec((tq, A_WIDTH), lambda i: (i, 0)),
        out_shape=jax.ShapeDtypeStruct((s_len, A_WIDTH), BF16),
        scratch_shapes=[pltpu.VMEM((tq + BLOCK, A_KV_WIDTH), BF16), pltpu.VMEM((tq + BLOCK, A_KV_WIDTH), BF16)],
        compiler_params=_compiler_params(("arbitrary",)),
        name="mixer_a",
    )(h, bias)


def _mixer_b_kernel(d, nb, q_ref, k_ref, v_ref, bias_ref, *refs):
    by_mxu = d > STRIDED_MAX_DILATION
    if d == 1:
        o_ref, lse_ref, qsub, ksub, vsub = refs
    elif by_mxu:
        perm_ref, perm_t_ref, o_ref, lse_ref, qsub, ksub, vsub, ystage, lstage = refs
    else:
        o_ref, lse_ref, qsub, ksub, vsub, slab, ostage, lstage = refs
    step = pl.program_id(0)
    n_heads = B_HEADS_PER_GROUP
    span = BLOCK * d

    @pl.when(step == 0)
    def _():
        ksub[:, 0:BLOCK, :] = jnp.zeros((d, BLOCK, B_GROUP_WIDTH), BF16)
        vsub[:, 0:BLOCK, :] = jnp.zeros((d, BLOCK, B_GROUP_WIDTH), BF16)

    if d == 1:
        qsub[0] = q_ref[...]
        ksub[0, BLOCK:, :] = k_ref[...]
        vsub[0, BLOCK:, :] = v_ref[...]
    elif not by_mxu:
        for src, dst, row0 in ((q_ref, qsub, 0), (k_ref, ksub, BLOCK), (v_ref, vsub, BLOCK)):
            for c in range(n_heads):
                cols = slice(c * LANES, (c + 1) * LANES)
                sl = slab.at[c]
                sl[...] = src[:, cols].astype(F32)
                for r in range(d):
                    for b in range(nb):
                        piece = sl[pl.ds(b * span + r, BLOCK, stride=d), :]
                        dst[r, row0 + b * BLOCK:row0 + (b + 1) * BLOCK, cols] = piece.astype(BF16)
    else:
        per = PERM_ROWS // d
        n_perm_blocks = nb * span // PERM_ROWS
        for src, dst, row0 in ((q_ref, qsub, 0), (k_ref, ksub, BLOCK), (v_ref, vsub, BLOCK)):
            for j in range(n_perm_blocks):
                y = jnp.dot(perm_ref[...], src[j * PERM_ROWS:(j + 1) * PERM_ROWS, :],
                            preferred_element_type=F32).astype(BF16)
                for r in range(d):
                    dst[r, row0 + j * per:row0 + (j + 1) * per, :] = y[r * per:(r + 1) * per, :]

    ones = jnp.ones((2 * BLOCK, LANES), BF16)
    lane = lax.broadcasted_iota(jnp.int32, (BLOCK, LANES), 1)
    for r in range(d):
        for b in range(nb):
            bias = _block_bias(bias_ref, step, b)
            rows = slice(b * BLOCK, (b + 1) * BLOCK)
            lse_tile = jnp.zeros((BLOCK, LANES), F32)
            for h in range(n_heads):
                cols = slice(h * B_HEAD_DIM, (h + 1) * B_HEAD_DIM)
                kk = ksub[r, b * BLOCK:(b + 2) * BLOCK, cols]
                v2 = jnp.concatenate([vsub[r, b * BLOCK:(b + 2) * BLOCK, cols], ones], axis=1)
                s = lax.dot_general(qsub[r, rows, cols], kk, (((1,), (1,)), ((), ())),
                                    preferred_element_type=F32) + bias
                m = jnp.max(s, axis=1, keepdims=True)
                mb = jnp.broadcast_to(m, (BLOCK, LANES))
                prob = jnp.concatenate([jnp.exp2(s[:, :LANES] - mb), jnp.exp2(s[:, LANES:] - mb)],
                                       axis=1).astype(BF16)
                o2 = jnp.dot(prob, v2, preferred_element_type=F32)
                denom = o2[:, LANES:]
                o = o2[:, :LANES] / denom
                lse_h = mb * LN_2 + jnp.log(denom)
                in_head = jnp.logical_and(lane >= h * B_LSE_LANES, lane < (h + 1) * B_LSE_LANES)
                lse_tile = jnp.where(in_head, lse_h, lse_tile)
                if d == 1:
                    o_ref[rows, cols] = o.astype(o_ref.dtype)
                elif not by_mxu:
                    ostage[h, pl.ds(b * span + r, BLOCK, stride=d), :] = o
                else:
                    o = o.astype(BF16)
                    for jj in range(BLOCK // per):
                        ystage[b * (BLOCK // per) + jj, r * per:(r + 1) * per, cols] = o[jj * per:(jj + 1) * per, :]
            if d == 1:
                lse_ref[rows, :] = lse_tile
            else:
                lstage[pl.ds(b * span + r, BLOCK, stride=d), :] = lse_tile

    if d > 1:
        if by_mxu:
            for j in range(n_perm_blocks):
                o_ref[j * PERM_ROWS:(j + 1) * PERM_ROWS, :] = jnp.dot(
                    perm_t_ref[...], ystage[j], preferred_element_type=F32).astype(o_ref.dtype)
        else:
            for h in range(n_heads):
                o_ref[:, h * B_HEAD_DIM:(h + 1) * B_HEAD_DIM] = ostage[h].astype(o_ref.dtype)
        lse_ref[...] = lstage[...]
    ksub[:, 0:BLOCK, :] = ksub[:, nb * BLOCK:(nb + 1) * BLOCK, :]
    vsub[:, 0:BLOCK, :] = vsub[:, nb * BLOCK:(nb + 1) * BLOCK, :]


def _mixer_b_group(h, gi):
    s_len = h.shape[0]
    window, d = B_PATTERNS[gi]
    assert window // d == BLOCK
    nb = max(1, MIXER_B_TOKENS // (BLOCK * d))
    t_rows = nb * BLOCK * d
    gw = B_GROUP_WIDTH
    col_q, col_v = (PK_QKB + gi * 2 * gw) // gw, (PK_VB + gi * gw) // gw
    bias = _band_bias(BLOCK, 1)
    blk = lambda c: pl.BlockSpec((t_rows, gw), lambda i: (i, c))
    scratch = [pltpu.VMEM((d, nb * BLOCK, gw), BF16), pltpu.VMEM((d, (nb + 1) * BLOCK, gw), BF16),
               pltpu.VMEM((d, (nb + 1) * BLOCK, gw), BF16)]
    operands = [h, h, h, bias]
    in_specs = [blk(col_q), blk(col_q + 1), blk(col_v), pl.BlockSpec(bias.shape, lambda i: (0, 0, 0))]
    if 1 < d <= STRIDED_MAX_DILATION:
        scratch += [pltpu.VMEM((B_HEADS_PER_GROUP, t_rows, LANES), F32),
                    pltpu.VMEM((B_HEADS_PER_GROUP, t_rows, LANES), F32), pltpu.VMEM((t_rows, LANES), F32)]
    elif d > 1:
        assert PERM_ROWS % d == 0 and (PERM_ROWS // d) % 16 == 0 and BLOCK % (PERM_ROWS // d) == 0
        per = PERM_ROWS // d
        perm = np.zeros((PERM_ROWS, PERM_ROWS), np.float32)
        l_idx, r_idx = np.meshgrid(np.arange(per), np.arange(d), indexing="ij")
        perm[r_idx * per + l_idx, l_idx * d + r_idx] = 1.0
        operands += [jnp.asarray(perm, BF16), jnp.asarray(perm.T, BF16)]
        in_specs += [pl.BlockSpec((PERM_ROWS, PERM_ROWS), lambda i: (0, 0))] * 2
        scratch += [pltpu.VMEM((t_rows // PERM_ROWS, PERM_ROWS, gw), BF16), pltpu.VMEM((t_rows, LANES), F32)]
    return pl.pallas_call(
        functools.partial(_mixer_b_kernel, d, nb),
        grid=(s_len // t_rows,),
        in_specs=in_specs,
        out_specs=[pl.BlockSpec((t_rows, gw), lambda i: (i, 0)), pl.BlockSpec((t_rows, LANES), lambda i: (i, 0))],
        out_shape=[jax.ShapeDtypeStruct((s_len, gw), BF16), jax.ShapeDtypeStruct((s_len, LANES), F32)],
        scratch_shapes=scratch,
        compiler_params=_compiler_params(("arbitrary",)),
        name=f"mixer_b_d{d}",
    )(*operands)


TAIL_TM = 512


def _tail_kernel(x_ref, ya_ref, o0_ref, o1_ref, o2_ref, l0_ref, l1_ref, l2_ref, gate_b_ref, sig_a_ref, sig_b_ref,
                 w_pa_ref, w_pb_ref, w_out_ref, ln_g_ref, ln_b_ref, out_ref):
    tm = x_ref.shape[0]
    outs = [o0_ref, o1_ref, o2_ref]
    lses = [l0_ref[...], l1_ref[...], l2_ref[...]]
    m = jnp.maximum(jnp.maximum(lses[0], lses[1]), lses[2])
    es = [jnp.exp(l - m) for l in lses]
    inv = 1.0 / (es[0] + es[1] + es[2])
    wts = [e * inv for e in es]
    yb_cols = []
    for h in range(B_HEADS_PER_GROUP):
        cols = slice(h * B_HEAD_DIM, (h + 1) * B_HEAD_DIM)
        acc = None
        for gi in range(B_N_GROUPS):
            w_h = jnp.broadcast_to(wts[gi][:, h * B_LSE_LANES:h * B_LSE_LANES + 1], (tm, B_HEAD_DIM))
            term = w_h * outs[gi][:, cols].astype(F32)
            acc = term if acc is None else acc + term
        yb_cols.append((acc * gate_b_ref[:, cols].astype(F32)).astype(BF16))
    yb = jnp.concatenate(yb_cols, axis=1)
    y_a = jnp.dot(ya_ref[...], w_pa_ref[...], preferred_element_type=F32)
    y_b = jnp.dot(yb, w_pb_ref[...], preferred_element_type=F32)
    merged = sig_a_ref[...].astype(F32) * y_a + sig_b_ref[...].astype(F32) * y_b
    sub = jnp.dot(merged.astype(BF16), w_out_ref[...], preferred_element_type=F32)
    z = DN_ALPHA * x_ref[...] + sub
    inv_d = 1.0 / z.shape[-1]
    mu = jnp.sum(z, axis=-1, keepdims=True) * inv_d
    var = jnp.sum(z * z, axis=-1, keepdims=True) * inv_d - mu * mu
    out_ref[...] = ((z - mu) * lax.rsqrt(var + LN_EPS) * ln_g_ref[...] + ln_b_ref[...]).astype(out_ref.dtype)


def _tail(x2d, ya, o_groups, lse_groups, h, w_pa, w_pb, w_out, ln_g, ln_b):
    s_len = x2d.shape[0]
    tm = TAIL_TM
    row = lambda width, cblk=0: pl.BlockSpec((tm, width), lambda i: (i, cblk))
    full = lambda a: pl.BlockSpec(a.shape, lambda i: (0,) * a.ndim, pipeline_mode=pl.Buffered(1))
    gw = B_GROUP_WIDTH
    return pl.pallas_call(
        _tail_kernel,
        grid=(s_len // tm,),
        in_specs=[row(D_MODEL), row(A_WIDTH), row(gw), row(gw), row(gw), row(LANES), row(LANES), row(LANES),
                  row(gw, PK_GATE_B // gw), row(D_MODEL, PK_MG_A // D_MODEL), row(D_MODEL, PK_MG_B // D_MODEL),
                  full(w_pa), full(w_pb), full(w_out), full(ln_g), full(ln_b)],
        out_specs=row(D_MODEL),
        out_shape=jax.ShapeDtypeStruct((s_len, D_MODEL), x2d.dtype),
        compiler_params=_compiler_params(("arbitrary",)),
        name="tail",
    )(x2d, ya, *o_groups, *lse_groups, h, h, h, w_pa, w_pb, w_out, ln_g, ln_b)


def _hybrid_layer(x, positions, w_in, b_gate, sinks, w_pa, w_pb, w_out, ln_g, ln_b):
    bn, s_len, d_model = x.shape
    assert bn == 1 and d_model == D_MODEL
    assert s_len % (BLOCK * B_PATTERNS[-1][1]) == 0 and s_len % PROJ_TM == 0
    x2d = x.reshape(s_len, d_model)
    w, kinds, bias, tables = _pack_weights_and_tables(w_in, b_gate, positions)
    h = _in_projection(x2d, w, bias, tables, kinds)
    o_groups, lse_groups = [], []
    for gi in range(B_N_GROUPS):
        o, lse = _mixer_b_group(h, gi)
        o_groups.append(o)
        lse_groups.append(lse)
    ya = _mixer_a(h, sinks)
    out = _tail(x2d, ya, o_groups, lse_groups, h, w_pa.astype(BF16), w_pb.astype(BF16), w_out.astype(BF16),
                ln_g.reshape(1, d_model).astype(F32), ln_b.reshape(1, d_model).astype(F32))
    return out.reshape(bn, s_len, d_model)


def kernel(x, positions, w_in, b_gate, sinks, w_pa, w_pb, w_out, ln_g, ln_b):
    for layer in range(w_in.shape[0]):
        x = _hybrid_layer(x, positions, w_in[layer], b_gate[layer], sinks[layer], w_pa[layer], w_pb[layer],
                          w_out[layer], ln_g[layer], ln_b[layer])
    return x
```

```python
import functools

import jax
import jax.numpy as jnp
import numpy as np
from jax import lax
from jax.experimental import pallas as pl
from jax.experimental.pallas import tpu as pltpu

F32 = jnp.float32
BF16 = jnp.bfloat16

D_MODEL = 2048
ROPE_THETA = 10000.0
LN_EPS = 1e-5
BLOCK = 128
LANES = 128
A_HEADS = 16
A_KV_HEADS = 2
A_HEAD_DIM = 64
A_WINDOW = 128
A_WIDTH = A_HEADS * A_HEAD_DIM
A_KV_WIDTH = A_KV_HEADS * A_HEAD_DIM
B_PATTERNS = ((128, 1), (512, 4), (2048, 16))
B_HEADS_PER_GROUP = 4
B_HEAD_DIM = 128
B_GROUP_WIDTH = B_HEADS_PER_GROUP * B_HEAD_DIM
B_N_GROUPS = len(B_PATTERNS)
B_QKV_WIDTH = B_N_GROUPS * B_GROUP_WIDTH
B_LSE_LANES = LANES // B_HEADS_PER_GROUP
DEPTH = 1
DN_ALPHA = float((2 * DEPTH) ** 0.25)

IN_SIZES = (A_WIDTH, A_KV_WIDTH, A_KV_WIDTH, A_WIDTH, B_QKV_WIDTH, B_QKV_WIDTH, B_QKV_WIDTH,
            B_GROUP_WIDTH, D_MODEL, D_MODEL)
IN_OFFSETS = tuple(int(o) for o in np.cumsum((0,) + IN_SIZES[:-1]))
(OFF_QA, OFF_KA, OFF_VA, OFF_GATE_A, OFF_QB, OFF_KB, OFF_VB, OFF_GATE_B, OFF_MG_A, OFF_MG_B) = IN_OFFSETS
D_IN = sum(IN_SIZES)

VMEM_LIMIT_BYTES = 60 * 1024 * 1024

EP_NONE, EP_ROPE64, EP_ROPE128, EP_SILU, EP_SIGMOID, EP_ROPE64_Q, EP_ROPE128_Q = range(7)

LOG2_E = float(np.log2(np.e))
LN_2 = float(np.log(2.0))
Q_SCALE_A = A_HEAD_DIM ** -0.5 * LOG2_E
Q_SCALE_B = B_HEAD_DIM ** -0.5 * LOG2_E

PROJ_TM = 1024
PROJ_TN = 2304

PK_MG_A = 0
PK_MG_B = PK_MG_A + D_MODEL
PK_VB = PK_MG_B + D_MODEL
PK_GATE_B = PK_VB + B_QKV_WIDTH
PK_QKB = PK_GATE_B + B_GROUP_WIDTH
PK_A = PK_QKB + 2 * B_QKV_WIDTH
PK_A_WIDTH = 2 * A_WIDTH + 2 * A_KV_WIDTH
A_COL_K = A_WIDTH
A_COL_V = A_COL_K + A_KV_WIDTH
A_COL_GATE = A_COL_V + A_KV_WIDTH
assert PK_A + PK_A_WIDTH == D_IN and PK_A % PK_A_WIDTH == 0 and PK_VB % B_GROUP_WIDTH == 0
assert PK_GATE_B % B_GROUP_WIDTH == 0 and PK_QKB % B_GROUP_WIDTH == 0 and D_IN % PROJ_TN == 0


def _compiler_params(semantics):
    return pltpu.CompilerParams(dimension_semantics=semantics, vmem_limit_bytes=VMEM_LIMIT_BYTES)


N_ROPE_TABLES = 4


def _bf16_pieces(t):
    hi = t.astype(BF16)
    rest = t - hi.astype(F32)
    mid = rest.astype(BF16)
    low = (rest - mid.astype(F32)).astype(BF16)
    return jnp.concatenate([hi, mid, low], axis=1)


def _rope_tables(pos_rows, freq, expand_ref, select_ref, sign_ref):
    n_chunks = pos_rows.shape[0] // 2
    eye = (lax.broadcasted_iota(jnp.int32, (LANES, LANES), 0)
           == lax.broadcasted_iota(jnp.int32, (LANES, LANES), 1))
    pos2 = []
    for c in range(n_chunks):
        diag = jnp.concatenate([jnp.where(eye, pos_rows[c:c + 1, :], 0.0),
                                jnp.where(eye, pos_rows[n_chunks + c:n_chunks + c + 1, :], 0.0)], axis=1)
        pos2.append(jnp.dot(_bf16_pieces(diag), expand_ref[...], preferred_element_type=F32))
    ang = jnp.concatenate(pos2, axis=0) * freq

    def to_head_layout(t):
        r = jnp.dot(_bf16_pieces(t), select_ref[...], preferred_element_type=F32)
        return [jnp.concatenate([r[:, (2 * layout) * LANES:(2 * layout + 1) * LANES],
                                 r[:, (2 * layout + 1) * LANES:(2 * layout + 2) * LANES]], axis=0)
                for layout in range(2)]

    cos_a, cos_b = to_head_layout(jnp.cos(ang))
    sin_a, sin_b = to_head_layout(jnp.sin(ang))
    return [t * sign_ref[ti:ti + 1, :] for ti, t in enumerate((cos_a, sin_a, cos_b, sin_b))]


def _first_half_of_head64(lane):
    return jnp.bitwise_and(lane, A_HEAD_DIM - 1) < A_HEAD_DIM // 2


def _rope_inputs(positions):
    s_len = positions.shape[1]
    half_b = B_HEAD_DIM // 2
    assert A_HEAD_DIM * 2 == B_HEAD_DIM and 2 * half_b == LANES
    inv_b = ROPE_THETA ** (-jnp.arange(half_b, dtype=F32) / half_b)
    freq = jnp.concatenate([inv_b, inv_b])[None, :]
    pos_rows = positions.astype(F32).reshape(s_len // LANES, LANES)
    lane = np.arange(LANES)
    expand = np.zeros((2 * LANES, LANES), np.float32)
    expand[:LANES, :half_b] = 1.0
    expand[LANES:, half_b:] = 1.0
    expand = np.concatenate([expand] * 3, axis=0)
    src_lane = np.stack([2 * (lane % (A_HEAD_DIM // 2)), lane % half_b])
    select = np.zeros((LANES, 4 * LANES), np.float32)
    for layout in range(2):
        for hf in range(2):
            select[hf * half_b + src_lane[layout], (2 * layout + hf) * LANES + lane] = 1.0
    select = np.concatenate([select] * 3, axis=0)
    sign = np.ones((N_ROPE_TABLES, LANES), np.float32)
    sign[1, lane % A_HEAD_DIM < A_HEAD_DIM // 2] = -1.0
    sign[3, lane < half_b] = -1.0
    return pos_rows, freq, jnp.asarray(expand, BF16), jnp.asarray(select, BF16), jnp.asarray(sign)


def _sigmoid_of_twice(h):
    return 0.5 * jnp.tanh(h) + 0.5


def _proj_kernel(tile_kinds, x_ref, w_ref, bias_ref, cos_a_ref, sin_a_ref, cos_b_ref, sin_b_ref, out_ref, xb_ref):
    n = pl.program_id(1)

    @pl.when(n == 0)
    def _():
        xb_ref[...] = x_ref[...].astype(BF16)

    def body(kinds):
        acc = jnp.dot(xb_ref[...], w_ref[...], preferred_element_type=F32)
        first_half = _first_half_of_head64(lax.broadcasted_iota(jnp.int32, (x_ref.shape[0], LANES), 1))
        for ci, kind in enumerate(kinds):
            cols = slice(ci * LANES, (ci + 1) * LANES)
            t = acc[:, cols]
            if kind in (EP_ROPE64, EP_ROPE64_Q):
                rot = jnp.where(first_half, pltpu.roll(t, 96, axis=1), pltpu.roll(t, 32, axis=1))
                t = t * cos_a_ref[...] + rot * sin_a_ref[...]
                if kind == EP_ROPE64_Q:
                    t = t * Q_SCALE_A
            elif kind in (EP_ROPE128, EP_ROPE128_Q):
                t = t * cos_b_ref[...] + pltpu.roll(t, 64, axis=1) * sin_b_ref[...]
                if kind == EP_ROPE128_Q:
                    t = t * Q_SCALE_B
            elif kind == EP_SILU:
                t = t * jnp.tanh(t) + t
            elif kind == EP_SIGMOID:
                t = _sigmoid_of_twice(t + bias_ref[:, cols])
            out_ref[:, cols] = t.astype(out_ref.dtype)

    branches = []
    for i, kinds in enumerate(tile_kinds):
        same = [br for br in branches if br[0] == kinds]
        if same:
            same[0][1].append(i)
        else:
            branches.append((kinds, [i]))
    for kinds, tiles in branches:
        cond = n == tiles[0]
        for i in tiles[1:]:
            cond = cond | (n == i)
        pl.when(cond)(functools.partial(body, kinds))


def _in_projection(x2d, w, bias, tables, chunk_kinds):
    s_len, k_dim = x2d.shape
    n_cols = w.shape[1]
    n_tiles = n_cols // PROJ_TN
    chunks_per_tile = PROJ_TN // LANES
    tile_kinds = tuple(tuple(chunk_kinds[t * chunks_per_tile:(t + 1) * chunks_per_tile]) for t in range(n_tiles))
    tm = min(PROJ_TM, s_len)
    row_spec = lambda width: pl.BlockSpec((tm, width), lambda m, n: (m, 0))
    return pl.pallas_call(
        functools.partial(_proj_kernel, tile_kinds),
        grid=(s_len // tm, n_tiles),
        in_specs=[row_spec(k_dim),
                  pl.BlockSpec((k_dim, PROJ_TN), lambda m, n: (0, n)),
                  pl.BlockSpec((1, PROJ_TN), lambda m, n: (0, n))] + [row_spec(LANES)] * N_ROPE_TABLES,
        out_specs=pl.BlockSpec((tm, PROJ_TN), lambda m, n: (m, n)),
        out_shape=jax.ShapeDtypeStruct((s_len, n_cols), BF16),
        scratch_shapes=[pltpu.VMEM((tm, k_dim), BF16)],
        compiler_params=_compiler_params(("arbitrary", "arbitrary")),
        name="in_projection",
    )(x2d, w, bias, *tables)


PACK_COLS = 256
PACK_BLOCKS_PER_STEP = 5
PACK_TN = PACK_BLOCKS_PER_STEP * PACK_COLS
assert D_IN % PACK_TN == 0


def _packed_layout():
    g = B_GROUP_WIDTH
    assert OFF_VA == OFF_KA + A_KV_WIDTH
    segments = [(OFF_MG_A, D_MODEL, [EP_SIGMOID]), (OFF_MG_B, D_MODEL, [EP_SIGMOID])]
    segments += [(OFF_VB, B_QKV_WIDTH, [EP_NONE]), (OFF_GATE_B, g, [EP_SILU])]
    for gi in range(B_N_GROUPS):
        segments += [(OFF_QB + gi * g, g, [EP_ROPE128_Q]), (OFF_KB + gi * g, g, [EP_ROPE128])]
    segments += [(OFF_QA, A_WIDTH, [EP_ROPE64_Q]),
                 (OFF_KA, 2 * A_KV_WIDTH, [EP_ROPE64] * (A_KV_WIDTH // LANES) + [EP_NONE] * (A_KV_WIDTH // LANES)),
                 (OFF_GATE_A, A_WIDTH, [EP_SILU])]
    perm, halve, kinds = [], [], []
    for off, width, seg_kinds in segments:
        assert off % PACK_COLS == 0 and width % PACK_COLS == 0
        perm += [off // PACK_COLS + j for j in range(width // PACK_COLS)]
        halve += [int(seg_kinds[0] in (EP_SIGMOID, EP_SILU))] * (width // PACK_COLS)
        kinds += seg_kinds * (width // LANES // len(seg_kinds))
    assert sorted(perm) == list(range(D_IN // PACK_COLS)) and len(kinds) == D_IN // LANES
    return np.asarray(perm, np.int32), np.asarray(halve, np.int32), kinds


def _pack_kernel(n_table_steps, perm_ref, halve_ref, pos_ref, freq_ref, expand_ref, select_ref, sign_ref, *refs):
    del perm_ref
    w_refs = refs[:PACK_BLOCKS_PER_STEP]
    out_ref = refs[PACK_BLOCKS_PER_STEP]
    table_refs = refs[PACK_BLOCKS_PER_STEP + 1:]
    for j, w_ref in enumerate(w_refs):
        scale = jnp.where(halve_ref[pl.program_id(0) * PACK_BLOCKS_PER_STEP + j] == 1, 0.5, 1.0)
        out_ref[:, j * PACK_COLS:(j + 1) * PACK_COLS] = (w_ref[...] * scale).astype(out_ref.dtype)

    @pl.when(pl.program_id(0) < n_table_steps)
    def _():
        tables = _rope_tables(pos_ref[...], freq_ref[...], expand_ref, select_ref, sign_ref)
        for t_ref, t in zip(table_refs, tables):
            t_ref[...] = t


def _pack_weights_and_tables(w_in, b_gate, positions):
    perm, halve, kinds = _packed_layout()
    k_dim = w_in.shape[0]
    n_steps = D_IN // PACK_TN
    pos_rows, freq, expand, select, sign = _rope_inputs(positions)
    s_len = positions.shape[1]
    n_table_steps = n_steps - 1
    assert s_len % (8 * LANES * n_table_steps) == 0
    const = lambda a: pl.BlockSpec(a.shape, lambda i, perm_ref, halve_ref: (0,) * a.ndim)
    table_rows = s_len // n_table_steps
    table_block = lambda i, perm_ref, halve_ref: (jnp.minimum(i, n_table_steps - 1), 0)
    tab_spec = pl.BlockSpec((table_rows, LANES), table_block)
    pos_spec = pl.BlockSpec((table_rows // LANES, LANES), table_block)
    src = lambda j: pl.BlockSpec((k_dim, PACK_COLS),
                                 lambda i, perm_ref, halve_ref: (0, perm_ref[i * PACK_BLOCKS_PER_STEP + j]))
    tab = jax.ShapeDtypeStruct((s_len, LANES), F32)
    w, *tables = pl.pallas_call(
        functools.partial(_pack_kernel, n_table_steps),
        grid_spec=pltpu.PrefetchScalarGridSpec(
            num_scalar_prefetch=2, grid=(n_steps,),
            in_specs=[pos_spec, const(freq), const(expand), const(select), const(sign)]
            + [src(j) for j in range(PACK_BLOCKS_PER_STEP)],
            out_specs=[pl.BlockSpec((k_dim, PACK_TN), lambda i, perm_ref, halve_ref: (0, i))]
            + [tab_spec] * N_ROPE_TABLES),
        out_shape=[jax.ShapeDtypeStruct((k_dim, D_IN), BF16)] + [tab] * N_ROPE_TABLES,
        compiler_params=_compiler_params(("arbitrary",)),
        name="pack_weights",
    )(jnp.asarray(perm), jnp.asarray(halve), pos_rows, freq, expand, select, sign,
      *([w_in] * PACK_BLOCKS_PER_STEP))
    bias = jnp.concatenate([0.5 * b_gate[0], 0.5 * b_gate[1], jnp.zeros((D_IN - 2 * D_MODEL,), F32)])[None, :]
    return w, kinds, bias, tables


ATT_TOKENS = 1024
MIXER_B_TOKENS = 2048
STRIDED_MAX_DILATION = 4
PERM_ROWS = 256


def _band_bias(max_dist, reps):
    q_idx = np.arange(BLOCK)[:, None] + BLOCK
    k_idx = np.arange(2 * BLOCK)[None, :]
    dist = q_idx - k_idx
    band = (dist >= 0) & (dist <= max_dist)
    first = band & (k_idx >= BLOCK)
    both = np.stack([first, band]).astype(bool)
    bias = np.where(both, 0.0, -np.inf).astype(np.float32)
    return jnp.asarray(np.tile(bias, (1, 1, reps)))


def _block_bias(bias_ref, step, b):
    if b == 0:
        return bias_ref[jnp.where(step == 0, 0, 1)]
    return bias_ref[1]


def _swap_lane_halves(t):
    return pltpu.roll(t, LANES // 2, axis=1)


def _mixer_a_kernel(a_ref, bias_ref, out_ref, kbuf_ref, vbuf_ref):
    step = pl.program_id(0)
    tq = a_ref.shape[0]
    n_blocks = tq // BLOCK
    n_pairs = A_HEADS // 2
    pairs_per_group = n_pairs // A_KV_HEADS

    @pl.when(step == 0)
    def _():
        kbuf_ref[0:BLOCK, :] = jnp.zeros((BLOCK, A_KV_WIDTH), BF16)
        vbuf_ref[0:BLOCK, :] = jnp.zeros((BLOCK, A_KV_WIDTH), BF16)

    kbuf_ref[BLOCK:, :] = a_ref[:, A_COL_K:A_COL_K + A_KV_WIDTH]
    vbuf_ref[BLOCK:, :] = a_ref[:, A_COL_V:A_COL_V + A_KV_WIDTH]
    lane2 = lax.broadcasted_iota(jnp.int32, (2 * BLOCK, LANES), 1)
    low2 = lane2 < LANES // 2
    sink_slot = lax.broadcasted_iota(jnp.int32, (2 * BLOCK, LANES), 0) == 0
    denom_cols = jnp.concatenate([jnp.where(low2, 1.0, 0.0), jnp.where(low2, 0.0, 1.0)], axis=0).astype(BF16)

    for b in range(n_blocks):
        bias_kind = jnp.where(step == 0, 0, 1) if b == 0 else 1
        rows = slice(b * BLOCK, (b + 1) * BLOCK)
        kk = jnp.where(sink_slot, 0.0, kbuf_ref[b * BLOCK:(b + 2) * BLOCK, :].astype(F32))
        vv = jnp.where(sink_slot, 0.0, vbuf_ref[b * BLOCK:(b + 2) * BLOCK, :].astype(F32))
        kk_sw = _swap_lane_halves(kk)
        vv_sw = _swap_lane_halves(vv)
        k2, v2 = [], []
        for g in range(A_KV_HEADS):
            if g == 0:
                k_top, k_bot = jnp.where(low2, kk, 0.0), jnp.where(low2, 0.0, kk_sw)
                v_top, v_bot = jnp.where(low2, vv, 0.0), jnp.where(low2, 0.0, vv_sw)
            else:
                k_top, k_bot = jnp.where(low2, kk_sw, 0.0), jnp.where(low2, 0.0, kk)
                v_top, v_bot = jnp.where(low2, vv_sw, 0.0), jnp.where(low2, 0.0, vv)
            k2.append(jnp.concatenate([k_top, k_bot], axis=0).astype(BF16))
            v2.append(jnp.concatenate([jnp.concatenate([v_top, v_bot], axis=0).astype(BF16), denom_cols], axis=1))
        for p in range(n_pairs):
            g = p // pairs_per_group
            cols = slice(p * LANES, (p + 1) * LANES)
            s = (lax.dot_general(a_ref[rows, cols], k2[g], (((1,), (1,)), ((), ())), preferred_element_type=F32)
                 + bias_ref[bias_kind, p])
            m0b = jnp.broadcast_to(jnp.max(s[:, :2 * BLOCK], axis=1, keepdims=True), (BLOCK, LANES))
            m1b = jnp.broadcast_to(jnp.max(s[:, 2 * BLOCK:], axis=1, keepdims=True), (BLOCK, LANES))
            shifts = (m0b, m0b, m1b, m1b)
            prob = jnp.concatenate([jnp.exp2(s[:, j * LANES:(j + 1) * LANES] - shifts[j]) for j in range(4)],
                                   axis=1).astype(BF16)
            o2 = jnp.dot(prob, v2[g], preferred_element_type=F32)
            o = o2[:, :LANES] / o2[:, LANES:]
            gate = a_ref[rows, A_COL_GATE + p * LANES:A_COL_GATE + (p + 1) * LANES].astype(F32)
            out_ref[rows, cols] = (o * gate).astype(out_ref.dtype)

    kbuf_ref[0:BLOCK, :] = kbuf_ref[tq:tq + BLOCK, :]
    vbuf_ref[0:BLOCK, :] = vbuf_ref[tq:tq + BLOCK, :]


def _mixer_a(h, sinks):
    s_len = h.shape[0]
    tq = ATT_TOKENS
    n_pairs = A_HEADS // 2
    band = _band_bias(A_WINDOW - 1, 2)
    sink_pairs = (sinks.astype(F32) * LOG2_E).reshape(n_pairs, 2)
    col = jnp.arange(4 * BLOCK)
    bias = jnp.broadcast_to(band[:, None], (2, n_pairs, BLOCK, 4 * BLOCK))
    bias = jnp.where(col == 0, sink_pairs[None, :, 0, None, None], bias)
    bias = jnp.where(col == 2 * BLOCK, sink_pairs[None, :, 1, None, None], bias)
    return pl.pallas_call(
        _mixer_a_kernel,
        grid=(s_len // tq,),
        in_specs=[
            pl.BlockSpec((tq, PK_A_WIDTH), lambda i: (i, PK_A // PK_A_WIDTH)),
            pl.BlockSpec(bias.shape, lambda i: (0, 0, 0, 0), pipeline_mode=pl.Buffered(1)),
        ],
        out_specs=pl.BlockSpec((tq, A_WIDTH), lambda i: (i, 0)),
        out_shape=jax.ShapeDtypeStruct((s_len, A_WIDTH), BF16),
        scratch_shapes=[pltpu.VMEM((tq + BLOCK, A_KV_WIDTH), BF16), pltpu.VMEM((tq + BLOCK, A_KV_WIDTH), BF16)],
        compiler_params=_compiler_params(("arbitrary",)),
        name="mixer_a",
    )(h, bias)


def _mixer_b_kernel(d, nb, q_ref, k_ref, v_ref, bias_ref, *refs):
    by_mxu = d > STRIDED_MAX_DILATION
    if d == 1:
        o_ref, lse_ref, qsub, ksub, vsub = refs
    elif by_mxu:
        perm_ref, perm_t_ref, o_ref, lse_ref, qsub, ksub, vsub, ystage, lstage = refs
    else:
        o_ref, lse_ref, qsub, ksub, vsub, slab, ostage, lstage = refs
    step = pl.program_id(0)
    n_heads = B_HEADS_PER_GROUP
    span = BLOCK * d

    @pl.when(step == 0)
    def _():
        ksub[:, 0:BLOCK, :] = jnp.zeros((d, BLOCK, B_GROUP_WIDTH), BF16)
        vsub[:, 0:BLOCK, :] = jnp.zeros((d, BLOCK, B_GROUP_WIDTH), BF16)

    if d == 1:
        qsub[0] = q_ref[...]
        ksub[0, BLOCK:, :] = k_ref[...]
        vsub[0, BLOCK:, :] = v_ref[...]
    elif not by_mxu:
        for src, dst, row0 in ((q_ref, qsub, 0), (k_ref, ksub, BLOCK), (v_ref, vsub, BLOCK)):
            for c in range(n_heads):
                cols = slice(c * LANES, (c + 1) * LANES)
                sl = slab.at[c]
                sl[...] = src[:, cols].astype(F32)
                for r in range(d):
                    for b in range(nb):
                        piece = sl[pl.ds(b * span + r, BLOCK, stride=d), :]
                        dst[r, row0 + b * BLOCK:row0 + (b + 1) * BLOCK, cols] = piece.astype(BF16)
    else:
        per = PERM_ROWS // d
        n_perm_blocks = nb * span // PERM_ROWS
        for src, dst, row0 in ((q_ref, qsub, 0), (k_ref, ksub, BLOCK), (v_ref, vsub, BLOCK)):
            for j in range(n_perm_blocks):
                y = jnp.dot(perm_ref[...], src[j * PERM_ROWS:(j + 1) * PERM_ROWS, :],
                            preferred_element_type=F32).astype(BF16)
                for r in range(d):
                    dst[r, row0 + j * per:row0 + (j + 1) * per, :] = y[r * per:(r + 1) * per, :]

    ones = jnp.ones((2 * BLOCK, LANES), BF16)
    lane = lax.broadcasted_iota(jnp.int32, (BLOCK, LANES), 1)
    for r in range(d):
        for b in range(nb):
            bias = _block_bias(bias_ref, step, b)
            rows = slice(b * BLOCK, (b + 1) * BLOCK)
            lse_tile = jnp.zeros((BLOCK, LANES), F32)
            for h in range(n_heads):
                cols = slice(h * B_HEAD_DIM, (h + 1) * B_HEAD_DIM)
                kk = ksub[r, b * BLOCK:(b + 2) * BLOCK, cols]
                v2 = jnp.concatenate([vsub[r, b * BLOCK:(b + 2) * BLOCK, cols], ones], axis=1)
                s = lax.dot_general(qsub[r, rows, cols], kk, (((1,), (1,)), ((), ())),
                                    preferred_element_type=F32) + bias
                m = jnp.max(s, axis=1, keepdims=True)
                mb = jnp.broadcast_to(m, (BLOCK, LANES))
                prob = jnp.concatenate([jnp.exp2(s[:, :LANES] - mb), jnp.exp2(s[:, LANES:] - mb)],
                                       axis=1).astype(BF16)
                o2 = jnp.dot(prob, v2, preferred_element_type=F32)
                denom = o2[:, LANES:]
                o = o2[:, :LANES] / denom
                lse_h = mb * LN_2 + jnp.log(denom)
                in_head = jnp.logical_and(lane >= h * B_LSE_LANES, lane < (h + 1) * B_LSE_LANES)
                lse_tile = jnp.where(in_head, lse_h, lse_tile)
                if d == 1:
                    o_ref[rows, cols] = o.astype(o_ref.dtype)
                elif not by_mxu:
                    ostage[h, pl.ds(b * span + r, BLOCK, stride=d), :] = o
                else:
                    o = o.astype(BF16)
                    for jj in range(BLOCK // per):
                        ystage[b * (BLOCK // per) + jj, r * per:(r + 1) * per, cols] = o[jj * per:(jj + 1) * per, :]
            if d == 1:
                lse_ref[rows, :] = lse_tile
            else:
                lstage[pl.ds(b * span + r, BLOCK, stride=d), :] = lse_tile

    if d > 1:
        if by_mxu:
            for j in range(n_perm_blocks):
                o_ref[j * PERM_ROWS:(j + 1) * PERM_ROWS, :] = jnp.dot(
                    perm_t_ref[...], ystage[j], preferred_element_type=F32).astype(o_ref.dtype)
        else:
            for h in range(n_heads):
                o_ref[:, h * B_HEAD_DIM:(h + 1) * B_HEAD_DIM] = ostage[h].astype(o_ref.dtype)
        lse_ref[...] = lstage[...]
    ksub[:, 0:BLOCK, :] = ksub[:, nb * BLOCK:(nb + 1) * BLOCK, :]
    vsub[:, 0:BLOCK, :] = vsub[:, nb * BLOCK:(nb + 1) * BLOCK, :]


def _mixer_b_group(h, gi):
    s_len = h.shape[0]
    window, d = B_PATTERNS[gi]
    assert window // d == BLOCK
    nb = max(1, MIXER_B_TOKENS // (BLOCK * d))
    t_rows = nb * BLOCK * d
    gw = B_GROUP_WIDTH
    col_q, col_v = (PK_QKB + gi * 2 * gw) // gw, (PK_VB + gi * gw) // gw
    bias = _band_bias(BLOCK, 1)
    blk = lambda c: pl.BlockSpec((t_rows, gw), lambda i: (i, c))
    scratch = [pltpu.VMEM((d, nb * BLOCK, gw), BF16), pltpu.VMEM((d, (nb + 1) * BLOCK, gw), BF16),
               pltpu.VMEM((d, (nb + 1) * BLOCK, gw), BF16)]
    operands = [h, h, h, bias]
    in_specs = [blk(col_q), blk(col_q + 1), blk(col_v), pl.BlockSpec(bias.shape, lambda i: (0, 0, 0))]
    if 1 < d <= STRIDED_MAX_DILATION:
        scratch += [pltpu.VMEM((B_HEADS_PER_GROUP, t_rows, LANES), F32),
                    pltpu.VMEM((B_HEADS_PER_GROUP, t_rows, LANES), F32), pltpu.VMEM((t_rows, LANES), F32)]
    elif d > 1:
        assert PERM_ROWS % d == 0 and (PERM_ROWS // d) % 16 == 0 and BLOCK % (PERM_ROWS // d) == 0
        per = PERM_ROWS // d
        perm = np.zeros((PERM_ROWS, PERM_ROWS), np.float32)
        l_idx, r_idx = np.meshgrid(np.arange(per), np.arange(d), indexing="ij")
        perm[r_idx * per + l_idx, l_idx * d + r_idx] = 1.0
        operands += [jnp.asarray(perm, BF16), jnp.asarray(perm.T, BF16)]
        in_specs += [pl.BlockSpec((PERM_ROWS, PERM_ROWS), lambda i: (0, 0))] * 2
        scratch += [pltpu.VMEM((t_rows // PERM_ROWS, PERM_ROWS, gw), BF16), pltpu.VMEM((t_rows, LANES), F32)]
    return pl.pallas_call(
        functools.partial(_mixer_b_kernel, d, nb),
        grid=(s_len // t_rows,),
        in_specs=in_specs,
        out_specs=[pl.BlockSpec((t_rows, gw), lambda i: (i, 0)), pl.BlockSpec((t_rows, LANES), lambda i: (i, 0))],
        out_shape=[jax.ShapeDtypeStruct((s_len, gw), BF16), jax.ShapeDtypeStruct((s_len, LANES), F32)],
        scratch_shapes=scratch,
        compiler_params=_compiler_params(("arbitrary",)),
        name=f"mixer_b_d{d}",
    )(*operands)


TAIL_TM = 512


def _tail_kernel(x_ref, ya_ref, o0_ref, o1_ref, o2_ref, l0_ref, l1_ref, l2_ref, gate_b_ref, sig_a_ref, sig_b_ref,
                 w_pa_ref, w_pb_ref, w_out_ref, ln_g_ref, ln_b_ref, out_ref):
    tm = x_ref.shape[0]
    outs = [o0_ref, o1_ref, o2_ref]
    lses = [l0_ref[...], l1_ref[...], l2_ref[...]]
    m = jnp.maximum(jnp.maximum(lses[0], lses[1]), lses[2])
    es = [jnp.exp(l - m) for l in lses]
    inv = 1.0 / (es[0] + es[1] + es[2])
    wts = [e * inv for e in es]
    yb_cols = []
    for h in range(B_HEADS_PER_GROUP):
        cols = slice(h * B_HEAD_DIM, (h + 1) * B_HEAD_DIM)
        acc = None
        for gi in range(B_N_GROUPS):
            w_h = jnp.broadcast_to(wts[gi][:, h * B_LSE_LANES:h * B_LSE_LANES + 1], (tm, B_HEAD_DIM))
            term = w_h * outs[gi][:, cols].astype(F32)
            acc = term if acc is None else acc + term
        yb_cols.append((acc * gate_b_ref[:, cols].astype(F32)).astype(BF16))
    yb = jnp.concatenate(yb_cols, axis=1)
    y_a = jnp.dot(ya_ref[...], w_pa_ref[...], preferred_element_type=F32)
    y_b = jnp.dot(yb, w_pb_ref[...], preferred_element_type=F32)
    merged = sig_a_ref[...].astype(F32) * y_a + sig_b_ref[...].astype(F32) * y_b
    sub = jnp.dot(merged.astype(BF16), w_out_ref[...], preferred_element_type=F32)
    z = DN_ALPHA * x_ref[...] + sub
    inv_d = 1.0 / z.shape[-1]
    mu = jnp.sum(z, axis=-1, keepdims=True) * inv_d
    var = jnp.sum(z * z, axis=-1, keepdims=True) * inv_d - mu * mu
    out_ref[...] = ((z - mu) * lax.rsqrt(var + LN_EPS) * ln_g_ref[...] + ln_b_ref[...]).astype(out_ref.dtype)


def _tail(x2d, ya, o_groups, lse_groups, h, w_pa, w_pb, w_out, ln_g, ln_b):
    s_len = x2d.shape[0]
    tm = TAIL_TM
    row = lambda width, cblk=0: pl.BlockSpec((tm, width), lambda i: (i, cblk))
    full = lambda a: pl.BlockSpec(a.shape, lambda i: (0,) * a.ndim, pipeline_mode=pl.Buffered(1))
    gw = B_GROUP_WIDTH
    return pl.pallas_call(
        _tail_kernel,
        grid=(s_len // tm,),
        in_specs=[row(D_MODEL), row(A_WIDTH), row(gw), row(gw), row(gw), row(LANES), row(LANES), row(LANES),
                  row(gw, PK_GATE_B // gw), row(D_MODEL, PK_MG_A // D_MODEL), row(D_MODEL, PK_MG_B // D_MODEL),
                  full(w_pa), full(w_pb), full(w_out), full(ln_g), full(ln_b)],
        out_specs=row(D_MODEL),
        out_shape=jax.ShapeDtypeStruct((s_len, D_MODEL), x2d.dtype),
        compiler_params=_compiler_params(("arbitrary",)),
        name="tail",
    )(x2d, ya, *o_groups, *lse_groups, h, h, h, w_pa, w_pb, w_out, ln_g, ln_b)


def _hybrid_layer(x, positions, w_in, b_gate, sinks, w_pa, w_pb, w_out, ln_g, ln_b):
    bn, s_len, d_model = x.shape
    assert bn == 1 and d_model == D_MODEL
    assert s_len % (BLOCK * B_PATTERNS[-1][1]) == 0 and s_len % PROJ_TM == 0
    x2d = x.reshape(s_len, d_model)
    w, kinds, bias, tables = _pack_weights_and_tables(w_in, b_gate, positions)
    h = _in_projection(x2d, w, bias, tables, kinds)
    o_groups, lse_groups = [], []
    for gi in range(B_N_GROUPS):
        o, lse = _mixer_b_group(h, gi)
        o_groups.append(o)
        lse_groups.append(lse)
    ya = _mixer_a(h, sinks)
    out = _tail(x2d, ya, o_groups, lse_groups, h, w_pa.astype(BF16), w_pb.astype(BF16), w_out.astype(BF16),
                ln_g.reshape(1, d_model).astype(F32), ln_b.reshape(1, d_model).astype(F32))
    return out.reshape(bn, s_len, d_model)


def kernel(x, positions, w_in, b_gate, sinks, w_pa, w_pb, w_out, ln_g, ln_b):
    for layer in range(w_in.shape[0]):
        x = _hybrid_layer(x, positions, w_in[layer], b_gate[layer], sinks[layer], w_pa[layer], w_pb[layer],
                          w_out[layer], ln_g[layer], ln_b[layer])
    return x
```

```python
import functools

import jax
import jax.numpy as jnp
import numpy as np
from jax import lax
from jax.experimental import pallas as pl
from jax.experimental.pallas import tpu as pltpu

F32 = jnp.float32
BF16 = jnp.bfloat16

D_MODEL = 2048
ROPE_THETA = 10000.0
LN_EPS = 1e-5
BLOCK = 128
LANES = 128
A_HEADS = 16
A_KV_HEADS = 2
A_HEAD_DIM = 64
A_WINDOW = 128
A_WIDTH = A_HEADS * A_HEAD_DIM
A_KV_WIDTH = A_KV_HEADS * A_HEAD_DIM
B_PATTERNS = ((128, 1), (512, 4), (2048, 16))
B_HEADS_PER_GROUP = 4
B_HEAD_DIM = 128
B_GROUP_WIDTH = B_HEADS_PER_GROUP * B_HEAD_DIM
B_N_GROUPS = len(B_PATTERNS)
B_QKV_WIDTH = B_N_GROUPS * B_GROUP_WIDTH
B_LSE_LANES = LANES // B_HEADS_PER_GROUP
DEPTH = 1
DN_ALPHA = float((2 * DEPTH) ** 0.25)

IN_SIZES = (A_WIDTH, A_KV_WIDTH, A_KV_WIDTH, A_WIDTH, B_QKV_WIDTH, B_QKV_WIDTH, B_QKV_WIDTH,
            B_GROUP_WIDTH, D_MODEL, D_MODEL)
IN_OFFSETS = tuple(int(o) for o in np.cumsum((0,) + IN_SIZES[:-1]))
(OFF_QA, OFF_KA, OFF_VA, OFF_GATE_A, OFF_QB, OFF_KB, OFF_VB, OFF_GATE_B, OFF_MG_A, OFF_MG_B) = IN_OFFSETS
D_IN = sum(IN_SIZES)

VMEM_LIMIT_BYTES = 60 * 1024 * 1024

EP_NONE, EP_ROPE64, EP_ROPE128, EP_SILU, EP_SIGMOID, EP_ROPE64_Q, EP_ROPE128_Q = range(7)

LOG2_E = float(np.log2(np.e))
LN_2 = float(np.log(2.0))
Q_SCALE_A = A_HEAD_DIM ** -0.5 * LOG2_E
Q_SCALE_B = B_HEAD_DIM ** -0.5 * LOG2_E

PROJ_TM = 1024
PROJ_TN = 2304

PK_MG_A = 0
PK_MG_B = PK_MG_A + D_MODEL
PK_VB = PK_MG_B + D_MODEL
PK_GATE_B = PK_VB + B_QKV_WIDTH
PK_QKB = PK_GATE_B + B_GROUP_WIDTH
PK_A = PK_QKB + 2 * B_QKV_WIDTH
PK_A_WIDTH = 2 * A_WIDTH + 2 * A_KV_WIDTH
A_COL_K = A_WIDTH
A_COL_V = A_COL_K + A_KV_WIDTH
A_COL_GATE = A_COL_V + A_KV_WIDTH
assert PK_A + PK_A_WIDTH == D_IN and PK_A % PK_A_WIDTH == 0 and PK_VB % B_GROUP_WIDTH == 0
assert PK_GATE_B % B_GROUP_WIDTH == 0 and PK_QKB % B_GROUP_WIDTH == 0 and D_IN % PROJ_TN == 0


def _compiler_params(semantics):
    return pltpu.CompilerParams(dimension_semantics=semantics, vmem_limit_bytes=VMEM_LIMIT_BYTES)


N_ROPE_TABLES = 4


def _bf16_pieces(t):
    hi = t.astype(BF16)
    rest = t - hi.astype(F32)
    mid = rest.astype(BF16)
    low = (rest - mid.astype(F32)).astype(BF16)
    return jnp.concatenate([hi, mid, low], axis=1)


def _rope_tables(pos_rows, freq, expand_ref, select_ref, sign_ref):
    n_chunks = pos_rows.shape[0] // 2
    eye = (lax.broadcasted_iota(jnp.int32, (LANES, LANES), 0)
           == lax.broadcasted_iota(jnp.int32, (LANES, LANES), 1))
    pos2 = []
    for c in range(n_chunks):
        diag = jnp.concatenate([jnp.where(eye, pos_rows[c:c + 1, :], 0.0),
                                jnp.where(eye, pos_rows[n_chunks + c:n_chunks + c + 1, :], 0.0)], axis=1)
        pos2.append(jnp.dot(_bf16_pieces(diag), expand_ref[...], preferred_element_type=F32))
    ang = jnp.concatenate(pos2, axis=0) * freq

    def to_head_layout(t):
        r = jnp.dot(_bf16_pieces(t), select_ref[...], preferred_element_type=F32)
        return [jnp.concatenate([r[:, (2 * layout) * LANES:(2 * layout + 1) * LANES],
                                 r[:, (2 * layout + 1) * LANES:(2 * layout + 2) * LANES]], axis=0)
                for layout in range(2)]

    cos_a, cos_b = to_head_layout(jnp.cos(ang))
    sin_a, sin_b = to_head_layout(jnp.sin(ang))
    return [t * sign_ref[ti:ti + 1, :] for ti, t in enumerate((cos_a, sin_a, cos_b, sin_b))]


def _first_half_of_head64(lane):
    return jnp.bitwise_and(lane, A_HEAD_DIM - 1) < A_HEAD_DIM // 2


def _rope_inputs(positions):
    s_len = positions.shape[1]
    half_b = B_HEAD_DIM // 2
    assert A_HEAD_DIM * 2 == B_HEAD_DIM and 2 * half_b == LANES
    inv_b = ROPE_THETA ** (-jnp.arange(half_b, dtype=F32) / half_b)
    freq = jnp.concatenate([inv_b, inv_b])[None, :]
    pos_rows = positions.astype(F32).reshape(s_len // LANES, LANES)
    lane = np.arange(LANES)
    expand = np.zeros((2 * LANES, LANES), np.float32)
    expand[:LANES, :half_b] = 1.0
    expand[LANES:, half_b:] = 1.0
    expand = np.concatenate([expand] * 3, axis=0)
    src_lane = np.stack([2 * (lane % (A_HEAD_DIM // 2)), lane % half_b])
    select = np.zeros((LANES, 4 * LANES), np.float32)
    for layout in range(2):
        for hf in range(2):
            select[hf * half_b + src_lane[layout], (2 * layout + hf) * LANES + lane] = 1.0
    select = np.concatenate([select] * 3, axis=0)
    sign = np.ones((N_ROPE_TABLES, LANES), np.float32)
    sign[1, lane % A_HEAD_DIM < A_HEAD_DIM // 2] = -1.0
    sign[3, lane < half_b] = -1.0
    return pos_rows, freq, jnp.asarray(expand, BF16), jnp.asarray(select, BF16), jnp.asarray(sign)


def _sigmoid_of_twice(h):
    return 0.5 * jnp.tanh(h) + 0.5


def _proj_kernel(tile_kinds, x_ref, w_ref, bias_ref, cos_a_ref, sin_a_ref, cos_b_ref, sin_b_ref, out_ref, xb_ref):
    n = pl.program_id(1)

    @pl.when(n == 0)
    def _():
        xb_ref[...] = x_ref[...].astype(BF16)

    def body(kinds):
        acc = jnp.dot(xb_ref[...], w_ref[...], preferred_element_type=F32)
        first_half = _first_half_of_head64(lax.broadcasted_iota(jnp.int32, (x_ref.shape[0], LANES), 1))
        for ci, kind in enumerate(kinds):
            cols = slice(ci * LANES, (ci + 1) * LANES)
            t = acc[:, cols]
            if kind in (EP_ROPE64, EP_ROPE64_Q):
                rot = jnp.where(first_half, pltpu.roll(t, 96, axis=1), pltpu.roll(t, 32, axis=1))
                t = t * cos_a_ref[...] + rot * sin_a_ref[...]
                if kind == EP_ROPE64_Q:
                    t = t * Q_SCALE_A
            elif kind in (EP_ROPE128, EP_ROPE128_Q):
                t = t * cos_b_ref[...] + pltpu.roll(t, 64, axis=1) * sin_b_ref[...]
                if kind == EP_ROPE128_Q:
                    t = t * Q_SCALE_B
            elif kind == EP_SILU:
                t = t * jnp.tanh(t) + t
            elif kind == EP_SIGMOID:
                t = _sigmoid_of_twice(t + bias_ref[:, cols])
            out_ref[:, cols] = t.astype(out_ref.dtype)

    branches = []
    for i, kinds in enumerate(tile_kinds):
        same = [br for br in branches if br[0] == kinds]
        if same:
            same[0][1].append(i)
        else:
            branches.append((kinds, [i]))
    for kinds, tiles in branches:
        cond = n == tiles[0]
        for i in tiles[1:]:
            cond = cond | (n == i)
        pl.when(cond)(functools.partial(body, kinds))


def _in_projection(x2d, w, bias, tables, chunk_kinds):
    s_len, k_dim = x2d.shape
    n_cols = w.shape[1]
    n_tiles = n_cols // PROJ_TN
    chunks_per_tile = PROJ_TN // LANES
    tile_kinds = tuple(tuple(chunk_kinds[t * chunks_per_tile:(t + 1) * chunks_per_tile]) for t in range(n_tiles))
    tm = min(PROJ_TM, s_len)
    row_spec = lambda width: pl.BlockSpec((tm, width), lambda m, n: (m, 0))
    return pl.pallas_call(
        functools.partial(_proj_kernel, tile_kinds),
        grid=(s_len // tm, n_tiles),
        in_specs=[row_spec(k_dim),
                  pl.BlockSpec((k_dim, PROJ_TN), lambda m, n: (0, n)),
                  pl.BlockSpec((1, PROJ_TN), lambda m, n: (0, n))] + [row_spec(LANES)] * N_ROPE_TABLES,
        out_specs=pl.BlockSpec((tm, PROJ_TN), lambda m, n: (m, n)),
        out_shape=jax.ShapeDtypeStruct((s_len, n_cols), BF16),
        scratch_shapes=[pltpu.VMEM((tm, k_dim), BF16)],
        compiler_params=_compiler_params(("arbitrary", "arbitrary")),
        name="in_projection",
    )(x2d, w, bias, *tables)


PACK_COLS = 256
PACK_ROWS = 256


def _packed_layout():
    g = B_GROUP_WIDTH
    assert OFF_VA == OFF_KA + A_KV_WIDTH
    segments = [(OFF_MG_A, D_MODEL, [EP_SIGMOID]), (OFF_MG_B, D_MODEL, [EP_SIGMOID])]
    segments += [(OFF_VB, B_QKV_WIDTH, [EP_NONE]), (OFF_GATE_B, g, [EP_SILU])]
    for gi in range(B_N_GROUPS):
        segments += [(OFF_QB + gi * g, g, [EP_ROPE128_Q]), (OFF_KB + gi * g, g, [EP_ROPE128])]
    segments += [(OFF_QA, A_WIDTH, [EP_ROPE64_Q]),
                 (OFF_KA, 2 * A_KV_WIDTH, [EP_ROPE64] * (A_KV_WIDTH // LANES) + [EP_NONE] * (A_KV_WIDTH // LANES)),
                 (OFF_GATE_A, A_WIDTH, [EP_SILU])]
    perm, halve, kinds = [], [], []
    for off, width, seg_kinds in segments:
        assert off % PACK_COLS == 0 and width % PACK_COLS == 0
        perm += [off // PACK_COLS + j for j in range(width // PACK_COLS)]
        halve += [int(seg_kinds[0] in (EP_SIGMOID, EP_SILU))] * (width // PACK_COLS)
        kinds += seg_kinds * (width // LANES // len(seg_kinds))
    assert sorted(perm) == list(range(D_IN // PACK_COLS)) and len(kinds) == D_IN // LANES
    return np.asarray(perm, np.int32), np.asarray(halve, np.int32), kinds


def _pack_kernel(perm, halve, pos_ref, freq_ref, expand_ref, select_ref, sign_ref, w_ref, out_ref, *table_refs):
    for j, (src, half) in enumerate(zip(perm, halve)):
        block = w_ref[:, src * PACK_COLS:(src + 1) * PACK_COLS]
        if half:
            block = block * 0.5
        out_ref[:, j * PACK_COLS:(j + 1) * PACK_COLS] = block.astype(out_ref.dtype)
    tables = _rope_tables(pos_ref[...], freq_ref[...], expand_ref, select_ref, sign_ref)
    for t_ref, t in zip(table_refs, tables):
        t_ref[...] = t


def _pack_weights_and_tables(w_in, b_gate, positions):
    perm, halve, kinds = _packed_layout()
    k_dim = w_in.shape[0]
    n_steps = k_dim // PACK_ROWS
    pos_rows, freq, expand, select, sign = _rope_inputs(positions)
    s_len = positions.shape[1]
    assert s_len % (8 * LANES * n_steps) == 0
    const = lambda a: pl.BlockSpec(a.shape, lambda i: (0,) * a.ndim)
    table_rows = s_len // n_steps
    tab_spec = pl.BlockSpec((table_rows, LANES), lambda i: (i, 0))
    band = lambda: pl.BlockSpec((PACK_ROWS, D_IN), lambda i: (i, 0))
    tab = jax.ShapeDtypeStruct((s_len, LANES), F32)
    w, *tables = pl.pallas_call(
        functools.partial(_pack_kernel, tuple(int(p) for p in perm), tuple(int(f) for f in halve)),
        grid=(n_steps,),
        in_specs=[pl.BlockSpec((table_rows // LANES, LANES), lambda i: (i, 0)),
                  const(freq), const(expand), const(select), const(sign), band()],
        out_specs=[band()] + [tab_spec] * N_ROPE_TABLES,
        out_shape=[jax.ShapeDtypeStruct((k_dim, D_IN), BF16)] + [tab] * N_ROPE_TABLES,
        compiler_params=_compiler_params(("arbitrary",)),
        name="pack_weights",
    )(pos_rows, freq, expand, select, sign, w_in)
    bias = jnp.concatenate([0.5 * b_gate[0], 0.5 * b_gate[1], jnp.zeros((D_IN - 2 * D_MODEL,), F32)])[None, :]
    return w, kinds, bias, tables


ATT_TOKENS = 1024
MIXER_B_TOKENS = 2048
STRIDED_MAX_DILATION = 4
PERM_ROWS = 256


def _band_bias(max_dist, reps):
    q_idx = np.arange(BLOCK)[:, None] + BLOCK
    k_idx = np.arange(2 * BLOCK)[None, :]
    dist = q_idx - k_idx
    band = (dist >= 0) & (dist <= max_dist)
    first = band & (k_idx >= BLOCK)
    both = np.stack([first, band]).astype(bool)
    bias = np.where(both, 0.0, -np.inf).astype(np.float32)
    return jnp.asarray(np.tile(bias, (1, 1, reps)))


def _block_bias(bias_ref, step, b):
    if b == 0:
        return bias_ref[jnp.where(step == 0, 0, 1)]
    return bias_ref[1]


def _swap_lane_halves(t):
    return pltpu.roll(t, LANES // 2, axis=1)


def _mixer_a_kernel(a_ref, bias_ref, out_ref, kbuf_ref, vbuf_ref):
    step = pl.program_id(0)
    tq = a_ref.shape[0]
    n_blocks = tq // BLOCK
    n_pairs = A_HEADS // 2
    pairs_per_group = n_pairs // A_KV_HEADS

    @pl.when(step == 0)
    def _():
        kbuf_ref[0:BLOCK, :] = jnp.zeros((BLOCK, A_KV_WIDTH), BF16)
        vbuf_ref[0:BLOCK, :] = jnp.zeros((BLOCK, A_KV_WIDTH), BF16)

    kbuf_ref[BLOCK:, :] = a_ref[:, A_COL_K:A_COL_K + A_KV_WIDTH]
    vbuf_ref[BLOCK:, :] = a_ref[:, A_COL_V:A_COL_V + A_KV_WIDTH]
    lane2 = lax.broadcasted_iota(jnp.int32, (2 * BLOCK, LANES), 1)
    low2 = lane2 < LANES // 2
    sink_slot = lax.broadcasted_iota(jnp.int32, (2 * BLOCK, LANES), 0) == 0
    denom_cols = jnp.concatenate([jnp.where(low2, 1.0, 0.0), jnp.where(low2, 0.0, 1.0)], axis=0).astype(BF16)

    for b in range(n_blocks):
        bias_kind = jnp.where(step == 0, 0, 1) if b == 0 else 1
        rows = slice(b * BLOCK, (b + 1) * BLOCK)
        kk = jnp.where(sink_slot, 0.0, kbuf_ref[b * BLOCK:(b + 2) * BLOCK, :].astype(F32))
        vv = jnp.where(sink_slot, 0.0, vbuf_ref[b * BLOCK:(b + 2) * BLOCK, :].astype(F32))
        kk_sw = _swap_lane_halves(kk)
        vv_sw = _swap_lane_halves(vv)
        k2, v2 = [], []
        for g in range(A_KV_HEADS):
            if g == 0:
                k_top, k_bot = jnp.where(low2, kk, 0.0), jnp.where(low2, 0.0, kk_sw)
                v_top, v_bot = jnp.where(low2, vv, 0.0), jnp.where(low2, 0.0, vv_sw)
            else:
                k_top, k_bot = jnp.where(low2, kk_sw, 0.0), jnp.where(low2, 0.0, kk)
                v_top, v_bot = jnp.where(low2, vv_sw, 0.0), jnp.where(low2, 0.0, vv)
            k2.append(jnp.concatenate([k_top, k_bot], axis=0).astype(BF16))
            v2.append(jnp.concatenate([jnp.concatenate([v_top, v_bot], axis=0).astype(BF16), denom_cols], axis=1))
        for p in range(n_pairs):
            g = p // pairs_per_group
            cols = slice(p * LANES, (p + 1) * LANES)
            s = (lax.dot_general(a_ref[rows, cols], k2[g], (((1,), (1,)), ((), ())), preferred_element_type=F32)
                 + bias_ref[bias_kind, p])
            m0b = jnp.broadcast_to(jnp.max(s[:, :2 * BLOCK], axis=1, keepdims=True), (BLOCK, LANES))
            m1b = jnp.broadcast_to(jnp.max(s[:, 2 * BLOCK:], axis=1, keepdims=True), (BLOCK, LANES))
            shifts = (m0b, m0b, m1b, m1b)
            prob = jnp.concatenate([jnp.exp2(s[:, j * LANES:(j + 1) * LANES] - shifts[j]) for j in range(4)],
                                   axis=1).astype(BF16)
            o2 = jnp.dot(prob, v2[g], preferred_element_type=F32)
            o = o2[:, :LANES] / o2[:, LANES:]
            gate = a_ref[rows, A_COL_GATE + p * LANES:A_COL_GATE + (p + 1) * LANES].astype(F32)
            out_ref[rows, cols] = (o * gate).astype(out_ref.dtype)

    kbuf_ref[0:BLOCK, :] = kbuf_ref[tq:tq + BLOCK, :]
    vbuf_ref[0:BLOCK, :] = vbuf_ref[tq:tq + BLOCK, :]


def _mixer_a(h, sinks):
    s_len = h.shape[0]
    tq = ATT_TOKENS
    n_pairs = A_HEADS // 2
    band = _band_bias(A_WINDOW - 1, 2)
    sink_pairs = (sinks.astype(F32) * LOG2_E).reshape(n_pairs, 2)
    col = jnp.arange(4 * BLOCK)
    bias = jnp.broadcast_to(band[:, None], (2, n_pairs, BLOCK, 4 * BLOCK))
    bias = jnp.where(col == 0, sink_pairs[None, :, 0, None, None], bias)
    bias = jnp.where(col == 2 * BLOCK, sink_pairs[None, :, 1, None, None], bias)
    return pl.pallas_call(
        _mixer_a_kernel,
        grid=(s_len // tq,),
        in_specs=[
            pl.BlockSpec((tq, PK_A_WIDTH), lambda i: (i, PK_A // PK_A_WIDTH)),
            pl.BlockSpec(bias.shape, lambda i: (0, 0, 0, 0), pipeline_mode=pl.Buffered(1)),
        ],
        out_specs=pl.BlockSpec((tq, A_WIDTH), lambda i: (i, 0)),
        out_shape=jax.ShapeDtypeStruct((s_len, A_WIDTH), BF16),
        scratch_shapes=[pltpu.VMEM((tq + BLOCK, A_KV_WIDTH), BF16), pltpu.VMEM((tq + BLOCK, A_KV_WIDTH), BF16)],
        compiler_params=_compiler_params(("arbitrary",)),
        name="mixer_a",
    )(h, bias)


def _mixer_b_kernel(d, nb, q_ref, k_ref, v_ref, bias_ref, *refs):
    by_mxu = d > STRIDED_MAX_DILATION
    if d == 1:
        o_ref, lse_ref, qsub, ksub, vsub = refs
    elif by_mxu:
        perm_ref, perm_t_ref, o_ref, lse_ref, qsub, ksub, vsub, ystage, lstage = refs
    else:
        o_ref, lse_ref, qsub, ksub, vsub, slab, ostage, lstage = refs
    step = pl.program_id(0)
    n_heads = B_HEADS_PER_GROUP
    span = BLOCK * d

    @pl.when(step == 0)
    def _():
        ksub[:, 0:BLOCK, :] = jnp.zeros((d, BLOCK, B_GROUP_WIDTH), BF16)
        vsub[:, 0:BLOCK, :] = jnp.zeros((d, BLOCK, B_GROUP_WIDTH), BF16)

    if d == 1:
        qsub[0] = q_ref[...]
        ksub[0, BLOCK:, :] = k_ref[...]
        vsub[0, BLOCK:, :] = v_ref[...]
    elif not by_mxu:
        for src, dst, row0 in ((q_ref, qsub, 0), (k_ref, ksub, BLOCK), (v_ref, vsub, BLOCK)):
            for c in range(n_heads):
                cols = slice(c * LANES, (c + 1) * LANES)
                sl = slab.at[c]
                sl[...] = src[:, cols].astype(F32)
                for r in range(d):
                    for b in range(nb):
                        piece = sl[pl.ds(b * span + r, BLOCK, stride=d), :]
                        dst[r, row0 + b * BLOCK:row0 + (b + 1) * BLOCK, cols] = piece.astype(BF16)
    else:
        per = PERM_ROWS // d
        n_perm_blocks = nb * span // PERM_ROWS
        for src, dst, row0 in ((q_ref, qsub, 0), (k_ref, ksub, BLOCK), (v_ref, vsub, BLOCK)):
            for j in range(n_perm_blocks):
                y = jnp.dot(perm_ref[...], src[j * PERM_ROWS:(j + 1) * PERM_ROWS, :],
                            preferred_element_type=F32).astype(BF16)
                for r in range(d):
                    dst[r, row0 + j * per:row0 + (j + 1) * per, :] = y[r * per:(r + 1) * per, :]

    ones = jnp.ones((2 * BLOCK, LANES), BF16)
    lane = lax.broadcasted_iota(jnp.int32, (BLOCK, LANES), 1)
    for r in range(d):
        for b in range(nb):
            bias = _block_bias(bias_ref, step, b)
            rows = slice(b * BLOCK, (b + 1) * BLOCK)
            lse_tile = jnp.zeros((BLOCK, LANES), F32)
            for h in range(n_heads):
                cols = slice(h * B_HEAD_DIM, (h + 1) * B_HEAD_DIM)
                kk = ksub[r, b * BLOCK:(b + 2) * BLOCK, cols]
                v2 = jnp.concatenate([vsub[r, b * BLOCK:(b + 2) * BLOCK, cols], ones], axis=1)
                s = lax.dot_general(qsub[r, rows, cols], kk, (((1,), (1,)), ((), ())),
                                    preferred_element_type=F32) + bias
                m = jnp.max(s, axis=1, keepdims=True)
                mb = jnp.broadcast_to(m, (BLOCK, LANES))
                prob = jnp.concatenate([jnp.exp2(s[:, :LANES] - mb), jnp.exp2(s[:, LANES:] - mb)],
                                       axis=1).astype(BF16)
                o2 = jnp.dot(prob, v2, preferred_element_type=F32)
                denom = o2[:, LANES:]
                o = o2[:, :LANES] / denom
                lse_h = mb * LN_2 + jnp.log(denom)
                in_head = jnp.logical_and(lane >= h * B_LSE_LANES, lane < (h + 1) * B_LSE_LANES)
                lse_tile = jnp.where(in_head, lse_h, lse_tile)
                if d == 1:
                    o_ref[rows, cols] = o.astype(o_ref.dtype)
                elif not by_mxu:
                    ostage[h, pl.ds(b * span + r, BLOCK, stride=d), :] = o
                else:
                    o = o.astype(BF16)
                    for jj in range(BLOCK // per):
                        ystage[b * (BLOCK // per) + jj, r * per:(r + 1) * per, cols] = o[jj * per:(jj + 1) * per, :]
            if d == 1:
                lse_ref[rows, :] = lse_tile
            else:
                lstage[pl.ds(b * span + r, BLOCK, stride=d), :] = lse_tile

    if d > 1:
        if by_mxu:
            for j in range(n_perm_blocks):
                o_ref[j * PERM_ROWS:(j + 1) * PERM_ROWS, :] = jnp.dot(
                    perm_t_ref[...], ystage[j], preferred_element_type=F32).astype(o_ref.dtype)
        else:
            for h in range(n_heads):
                o_ref[:, h * B_HEAD_DIM:(h + 1) * B_HEAD_DIM] = ostage[h].astype(o_ref.dtype)
        lse_ref[...] = lstage[...]
    ksub[:, 0:BLOCK, :] = ksub[:, nb * BLOCK:(nb + 1) * BLOCK, :]
    vsub[:, 0:BLOCK, :] = vsub[:, nb * BLOCK:(nb + 1) * BLOCK, :]


def _mixer_b_group(h, gi):
    s_len = h.shape[0]
    window, d = B_PATTERNS[gi]
    assert window // d == BLOCK
    nb = max(1, MIXER_B_TOKENS // (BLOCK * d))
    t_rows = nb * BLOCK * d
    gw = B_GROUP_WIDTH
    col_q, col_v = (PK_QKB + gi * 2 * gw) // gw, (PK_VB + gi * gw) // gw
    bias = _band_bias(BLOCK, 1)
    blk = lambda c: pl.BlockSpec((t_rows, gw), lambda i: (i, c))
    scratch = [pltpu.VMEM((d, nb * BLOCK, gw), BF16), pltpu.VMEM((d, (nb + 1) * BLOCK, gw), BF16),
               pltpu.VMEM((d, (nb + 1) * BLOCK, gw), BF16)]
    operands = [h, h, h, bias]
    in_specs = [blk(col_q), blk(col_q + 1), blk(col_v), pl.BlockSpec(bias.shape, lambda i: (0, 0, 0))]
    if 1 < d <= STRIDED_MAX_DILATION:
        scratch += [pltpu.VMEM((B_HEADS_PER_GROUP, t_rows, LANES), F32),
                    pltpu.VMEM((B_HEADS_PER_GROUP, t_rows, LANES), F32), pltpu.VMEM((t_rows, LANES), F32)]
    elif d > 1:
        assert PERM_ROWS % d == 0 and (PERM_ROWS // d) % 16 == 0 and BLOCK % (PERM_ROWS // d) == 0
        per = PERM_ROWS // d
        perm = np.zeros((PERM_ROWS, PERM_ROWS), np.float32)
        l_idx, r_idx = np.meshgrid(np.arange(per), np.arange(d), indexing="ij")
        perm[r_idx * per + l_idx, l_idx * d + r_idx] = 1.0
        operands += [jnp.asarray(perm, BF16), jnp.asarray(perm.T, BF16)]
        in_specs += [pl.BlockSpec((PERM_ROWS, PERM_ROWS), lambda i: (0, 0))] * 2
        scratch += [pltpu.VMEM((t_rows // PERM_ROWS, PERM_ROWS, gw), BF16), pltpu.VMEM((t_rows, LANES), F32)]
    return pl.pallas_call(
        functools.partial(_mixer_b_kernel, d, nb),
        grid=(s_len // t_rows,),
        in_specs=in_specs,
        out_specs=[pl.BlockSpec((t_rows, gw), lambda i: (i, 0)), pl.BlockSpec((t_rows, LANES), lambda i: (i, 0))],
        out_shape=[jax.ShapeDtypeStruct((s_len, gw), BF16), jax.ShapeDtypeStruct((s_len, LANES), F32)],
        scratch_shapes=scratch,
        compiler_params=_compiler_params(("arbitrary",)),
        name=f"mixer_b_d{d}",
    )(*operands)


TAIL_TM = 512


def _tail_kernel(x_ref, ya_ref, o0_ref, o1_ref, o2_ref, l0_ref, l1_ref, l2_ref, gate_b_ref, sig_a_ref, sig_b_ref,
                 w_pa_ref, w_pb_ref, w_out_ref, ln_g_ref, ln_b_ref, out_ref):
    tm = x_ref.shape[0]
    outs = [o0_ref, o1_ref, o2_ref]
    lses = [l0_ref[...], l1_ref[...], l2_ref[...]]
    m = jnp.maximum(jnp.maximum(lses[0], lses[1]), lses[2])
    es = [jnp.exp(l - m) for l in lses]
    inv = 1.0 / (es[0] + es[1] + es[2])
    wts = [e * inv for e in es]
    yb_cols = []
    for h in range(B_HEADS_PER_GROUP):
        cols = slice(h * B_HEAD_DIM, (h + 1) * B_HEAD_DIM)
        acc = None
        for gi in range(B_N_GROUPS):
            w_h = jnp.broadcast_to(wts[gi][:, h * B_LSE_LANES:h * B_LSE_LANES + 1], (tm, B_HEAD_DIM))
            term = w_h * outs[gi][:, cols].astype(F32)
            acc = term if acc is None else acc + term
        yb_cols.append((acc * gate_b_ref[:, cols].astype(F32)).astype(BF16))
    yb = jnp.concatenate(yb_cols, axis=1)
    y_a = jnp.dot(ya_ref[...], w_pa_ref[...], preferred_element_type=F32)
    y_b = jnp.dot(yb, w_pb_ref[...], preferred_element_type=F32)
    merged = sig_a_ref[...].astype(F32) * y_a + sig_b_ref[...].astype(F32) * y_b
    sub = jnp.dot(merged.astype(BF16), w_out_ref[...], preferred_element_type=F32)
    z = DN_ALPHA * x_ref[...] + sub
    inv_d = 1.0 / z.shape[-1]
    mu = jnp.sum(z, axis=-1, keepdims=True) * inv_d
    var = jnp.sum(z * z, axis=-1, keepdims=True) * inv_d - mu * mu
    out_ref[...] = ((z - mu) * lax.rsqrt(var + LN_EPS) * ln_g_ref[...] + ln_b_ref[...]).astype(out_ref.dtype)


def _tail(x2d, ya, o_groups, lse_groups, h, w_pa, w_pb, w_out, ln_g, ln_b):
    s_len = x2d.shape[0]
    tm = TAIL_TM
    row = lambda width, cblk=0: pl.BlockSpec((tm, width), lambda i: (i, cblk))
    full = lambda a: pl.BlockSpec(a.shape, lambda i: (0,) * a.ndim, pipeline_mode=pl.Buffered(1))
    gw = B_GROUP_WIDTH
    return pl.pallas_call(
        _tail_kernel,
        grid=(s_len // tm,),
        in_specs=[row(D_MODEL), row(A_WIDTH), row(gw), row(gw), row(gw), row(LANES), row(LANES), row(LANES),
                  row(gw, PK_GATE_B // gw), row(D_MODEL, PK_MG_A // D_MODEL), row(D_MODEL, PK_MG_B // D_MODEL),
                  full(w_pa), full(w_pb), full(w_out), full(ln_g), full(ln_b)],
        out_specs=row(D_MODEL),
        out_shape=jax.ShapeDtypeStruct((s_len, D_MODEL), x2d.dtype),
        compiler_params=_compiler_params(("arbitrary",)),
        name="tail",
    )(x2d, ya, *o_groups, *lse_groups, h, h, h, w_pa, w_pb, w_out, ln_g, ln_b)


def _hybrid_layer(x, positions, w_in, b_gate, sinks, w_pa, w_pb, w_out, ln_g, ln_b):
    bn, s_len, d_model = x.shape
    assert bn == 1 and d_model == D_MODEL
    assert s_len % (BLOCK * B_PATTERNS[-1][1]) == 0 and s_len % PROJ_TM == 0
    x2d = x.reshape(s_len, d_model)
    w, kinds, bias, tables = _pack_weights_and_tables(w_in, b_gate, positions)
    h = _in_projection(x2d, w, bias, tables, kinds)
    o_groups, lse_groups = [], []
    for gi in range(B_N_GROUPS):
        o, lse = _mixer_b_group(h, gi)
        o_groups.append(o)
        lse_groups.append(lse)
    ya = _mixer_a(h, sinks)
    out = _tail(x2d, ya, o_groups, lse_groups, h, w_pa.astype(BF16), w_pb.astype(BF16), w_out.astype(BF16),
                ln_g.reshape(1, d_model).astype(F32), ln_b.reshape(1, d_model).astype(F32))
    return out.reshape(bn, s_len, d_model)


def kernel(x, positions, w_in, b_gate, sinks, w_pa, w_pb, w_out, ln_g, ln_b):
    for layer in range(w_in.shape[0]):
        x = _hybrid_layer(x, positions, w_in[layer], b_gate[layer], sinks[layer], w_pa[layer], w_pb[layer],
                          w_out[layer], ln_g[layer], ln_b[layer])
    return x
```

```python
import functools

import jax
import jax.numpy as jnp
import numpy as np
from jax import lax
from jax.experimental import pallas as pl
from jax.experimental.pallas import tpu as pltpu

F32 = jnp.float32
BF16 = jnp.bfloat16

D_MODEL = 2048
ROPE_THETA = 10000.0
LN_EPS = 1e-5
BLOCK = 128
LANES = 128
A_HEADS = 16
A_KV_HEADS = 2
A_HEAD_DIM = 64
A_WINDOW = 128
A_WIDTH = A_HEADS * A_HEAD_DIM
A_KV_WIDTH = A_KV_HEADS * A_HEAD_DIM
B_PATTERNS = ((128, 1), (512, 4), (2048, 16))
B_HEADS_PER_GROUP = 4
B_HEAD_DIM = 128
B_GROUP_WIDTH = B_HEADS_PER_GROUP * B_HEAD_DIM
B_N_GROUPS = len(B_PATTERNS)
B_QKV_WIDTH = B_N_GROUPS * B_GROUP_WIDTH
B_LSE_LANES = LANES // B_HEADS_PER_GROUP
DEPTH = 1
DN_ALPHA = float((2 * DEPTH) ** 0.25)

IN_SIZES = (A_WIDTH, A_KV_WIDTH, A_KV_WIDTH, A_WIDTH, B_QKV_WIDTH, B_QKV_WIDTH, B_QKV_WIDTH,
            B_GROUP_WIDTH, D_MODEL, D_MODEL)
IN_OFFSETS = tuple(int(o) for o in np.cumsum((0,) + IN_SIZES[:-1]))
(OFF_QA, OFF_KA, OFF_VA, OFF_GATE_A, OFF_QB, OFF_KB, OFF_VB, OFF_GATE_B, OFF_MG_A, OFF_MG_B) = IN_OFFSETS
D_IN = sum(IN_SIZES)

VMEM_LIMIT_BYTES = 60 * 1024 * 1024

EP_NONE, EP_ROPE64, EP_ROPE128, EP_SILU, EP_SIGMOID, EP_ROPE64_Q, EP_ROPE128_Q = range(7)

LOG2_E = float(np.log2(np.e))
LN_2 = float(np.log(2.0))
Q_SCALE_A = A_HEAD_DIM ** -0.5 * LOG2_E
Q_SCALE_B = B_HEAD_DIM ** -0.5 * LOG2_E

PROJ_TM = 1024
PROJ_TN = 2304

PK_MG_A = 0
PK_MG_B = PK_MG_A + D_MODEL
PK_VB = PK_MG_B + D_MODEL
PK_GATE_B = PK_VB + B_QKV_WIDTH
PK_QKB = PK_GATE_B + B_GROUP_WIDTH
PK_A = PK_QKB + 2 * B_QKV_WIDTH
PK_A_WIDTH = 2 * A_WIDTH + 2 * A_KV_WIDTH
A_COL_K = A_WIDTH
A_COL_V = A_COL_K + A_KV_WIDTH
A_COL_GATE = A_COL_V + A_KV_WIDTH
assert PK_A + PK_A_WIDTH == D_IN and PK_A % PK_A_WIDTH == 0 and PK_VB % B_GROUP_WIDTH == 0
assert PK_GATE_B % B_GROUP_WIDTH == 0 and PK_QKB % B_GROUP_WIDTH == 0 and D_IN % PROJ_TN == 0


def _compiler_params(semantics):
    return pltpu.CompilerParams(dimension_semantics=semantics, vmem_limit_bytes=VMEM_LIMIT_BYTES)


N_ROPE_TABLES = 4


def _bf16_pieces(t):
    hi = t.astype(BF16)
    rest = t - hi.astype(F32)
    mid = rest.astype(BF16)
    low = (rest - mid.astype(F32)).astype(BF16)
    return jnp.concatenate([hi, mid, low], axis=1)


def _rope_tables(pos_rows, freq, expand_ref, select_ref, sign_ref):
    n_chunks = pos_rows.shape[0] // 2
    eye = (lax.broadcasted_iota(jnp.int32, (LANES, LANES), 0)
           == lax.broadcasted_iota(jnp.int32, (LANES, LANES), 1))
    pos2 = []
    for c in range(n_chunks):
        diag = jnp.concatenate([jnp.where(eye, pos_rows[c:c + 1, :], 0.0),
                                jnp.where(eye, pos_rows[n_chunks + c:n_chunks + c + 1, :], 0.0)], axis=1)
        pos2.append(jnp.dot(_bf16_pieces(diag), expand_ref[...], preferred_element_type=F32))
    ang = jnp.concatenate(pos2, axis=0) * freq

    def to_head_layout(t):
        r = jnp.dot(_bf16_pieces(t), select_ref[...], preferred_element_type=F32)
        return [jnp.concatenate([r[:, (2 * layout) * LANES:(2 * layout + 1) * LANES],
                                 r[:, (2 * layout + 1) * LANES:(2 * layout + 2) * LANES]], axis=0)
                for layout in range(2)]

    cos_a, cos_b = to_head_layout(jnp.cos(ang))
    sin_a, sin_b = to_head_layout(jnp.sin(ang))
    return [t * sign_ref[ti:ti + 1, :] for ti, t in enumerate((cos_a, sin_a, cos_b, sin_b))]


def _first_half_of_head64(lane):
    return jnp.bitwise_and(lane, A_HEAD_DIM - 1) < A_HEAD_DIM // 2


def _rope_inputs(positions):
    s_len = positions.shape[1]
    half_b = B_HEAD_DIM // 2
    assert A_HEAD_DIM * 2 == B_HEAD_DIM and 2 * half_b == LANES
    inv_b = ROPE_THETA ** (-jnp.arange(half_b, dtype=F32) / half_b)
    freq = jnp.concatenate([inv_b, inv_b])[None, :]
    pos_rows = positions.astype(F32).reshape(s_len // LANES, LANES)
    lane = np.arange(LANES)
    expand = np.zeros((2 * LANES, LANES), np.float32)
    expand[:LANES, :half_b] = 1.0
    expand[LANES:, half_b:] = 1.0
    expand = np.concatenate([expand] * 3, axis=0)
    src_lane = np.stack([2 * (lane % (A_HEAD_DIM // 2)), lane % half_b])
    select = np.zeros((LANES, 4 * LANES), np.float32)
    for layout in range(2):
        for hf in range(2):
            select[hf * half_b + src_lane[layout], (2 * layout + hf) * LANES + lane] = 1.0
    select = np.concatenate([select] * 3, axis=0)
    sign = np.ones((N_ROPE_TABLES, LANES), np.float32)
    sign[1, lane % A_HEAD_DIM < A_HEAD_DIM // 2] = -1.0
    sign[3, lane < half_b] = -1.0
    return pos_rows, freq, jnp.asarray(expand, BF16), jnp.asarray(select, BF16), jnp.asarray(sign)


def _sigmoid_of_twice(h):
    return 0.5 * jnp.tanh(h) + 0.5


def _proj_kernel(tile_kinds, x_ref, w_ref, bias_ref, cos_a_ref, sin_a_ref, cos_b_ref, sin_b_ref, out_ref, xb_ref):
    n = pl.program_id(1)

    @pl.when(n == 0)
    def _():
        xb_ref[...] = x_ref[...].astype(BF16)

    def body(kinds):
        acc = jnp.dot(xb_ref[...], w_ref[...], preferred_element_type=F32)
        first_half = _first_half_of_head64(lax.broadcasted_iota(jnp.int32, (x_ref.shape[0], LANES), 1))
        for ci, kind in enumerate(kinds):
            cols = slice(ci * LANES, (ci + 1) * LANES)
            t = acc[:, cols]
            if kind in (EP_ROPE64, EP_ROPE64_Q):
                rot = jnp.where(first_half, pltpu.roll(t, 96, axis=1), pltpu.roll(t, 32, axis=1))
                t = t * cos_a_ref[...] + rot * sin_a_ref[...]
                if kind == EP_ROPE64_Q:
                    t = t * Q_SCALE_A
            elif kind in (EP_ROPE128, EP_ROPE128_Q):
                t = t * cos_b_ref[...] + pltpu.roll(t, 64, axis=1) * sin_b_ref[...]
                if kind == EP_ROPE128_Q:
                    t = t * Q_SCALE_B
            elif kind == EP_SILU:
                t = t * jnp.tanh(t) + t
            elif kind == EP_SIGMOID:
                t = _sigmoid_of_twice(t + bias_ref[:, cols])
            out_ref[:, cols] = t.astype(out_ref.dtype)

    branches = []
    for i, kinds in enumerate(tile_kinds):
        same = [br for br in branches if br[0] == kinds]
        if same:
            same[0][1].append(i)
        else:
            branches.append((kinds, [i]))
    for kinds, tiles in branches:
        cond = n == tiles[0]
        for i in tiles[1:]:
            cond = cond | (n == i)
        pl.when(cond)(functools.partial(body, kinds))


def _in_projection(x2d, w, bias, tables, chunk_kinds):
    s_len, k_dim = x2d.shape
    n_cols = w.shape[1]
    n_tiles = n_cols // PROJ_TN
    chunks_per_tile = PROJ_TN // LANES
    tile_kinds = tuple(tuple(chunk_kinds[t * chunks_per_tile:(t + 1) * chunks_per_tile]) for t in range(n_tiles))
    tm = min(PROJ_TM, s_len)
    row_spec = lambda width: pl.BlockSpec((tm, width), lambda m, n: (m, 0))
    return pl.pallas_call(
        functools.partial(_proj_kernel, tile_kinds),
        grid=(s_len // tm, n_tiles),
        in_specs=[row_spec(k_dim),
                  pl.BlockSpec((k_dim, PROJ_TN), lambda m, n: (0, n)),
                  pl.BlockSpec((1, PROJ_TN), lambda m, n: (0, n))] + [row_spec(LANES)] * N_ROPE_TABLES,
        out_specs=pl.BlockSpec((tm, PROJ_TN), lambda m, n: (m, n)),
        out_shape=jax.ShapeDtypeStruct((s_len, n_cols), BF16),
        scratch_shapes=[pltpu.VMEM((tm, k_dim), BF16)],
        compiler_params=_compiler_params(("arbitrary", "arbitrary")),
        name="in_projection",
    )(x2d, w, bias, *tables)


PACK_COLS = 256
PACK_ROWS = 256


def _packed_layout():
    g = B_GROUP_WIDTH
    assert OFF_VA == OFF_KA + A_KV_WIDTH
    segments = [(OFF_MG_A, D_MODEL, [EP_SIGMOID]), (OFF_MG_B, D_MODEL, [EP_SIGMOID])]
    segments += [(OFF_VB, B_QKV_WIDTH, [EP_NONE]), (OFF_GATE_B, g, [EP_SILU])]
    for gi in range(B_N_GROUPS):
        segments += [(OFF_QB + gi * g, g, [EP_ROPE128_Q]), (OFF_KB + gi * g, g, [EP_ROPE128])]
    segments += [(OFF_QA, A_WIDTH, [EP_ROPE64_Q]),
                 (OFF_KA, 2 * A_KV_WIDTH, [EP_ROPE64] * (A_KV_WIDTH // LANES) + [EP_NONE] * (A_KV_WIDTH // LANES)),
                 (OFF_GATE_A, A_WIDTH, [EP_SILU])]
    perm, halve, kinds = [], [], []
    for off, width, seg_kinds in segments:
        assert off % PACK_COLS == 0 and width % PACK_COLS == 0
        perm += [off // PACK_COLS + j for j in range(width // PACK_COLS)]
        halve += [int(seg_kinds[0] in (EP_SIGMOID, EP_SILU))] * (width // PACK_COLS)
        kinds += seg_kinds * (width // LANES // len(seg_kinds))
    assert sorted(perm) == list(range(D_IN // PACK_COLS)) and len(kinds) == D_IN // LANES
    return np.asarray(perm, np.int32), np.asarray(halve, np.int32), kinds


def _pack_kernel(perm, halve, pos_ref, freq_ref, expand_ref, select_ref, sign_ref, w_ref, out_ref, *table_refs):
    for j, (src, half) in enumerate(zip(perm, halve)):
        block = w_ref[:, src * PACK_COLS:(src + 1) * PACK_COLS]
        if half:
            block = block * 0.5
        out_ref[:, j * PACK_COLS:(j + 1) * PACK_COLS] = block.astype(out_ref.dtype)
    tables = _rope_tables(pos_ref[...], freq_ref[...], expand_ref, select_ref, sign_ref)
    for t_ref, t in zip(table_refs, tables):
        t_ref[...] = t


def _pack_weights_and_tables(w_in, b_gate, positions):
    perm, halve, kinds = _packed_layout()
    k_dim = w_in.shape[0]
    n_steps = k_dim // PACK_ROWS
    pos_rows, freq, expand, select, sign = _rope_inputs(positions)
    s_len = positions.shape[1]
    assert s_len % (8 * LANES * n_steps) == 0
    const = lambda a: pl.BlockSpec(a.shape, lambda i: (0,) * a.ndim)
    table_rows = s_len // n_steps
    tab_spec = pl.BlockSpec((table_rows, LANES), lambda i: (i, 0))
    band = lambda: pl.BlockSpec((PACK_ROWS, D_IN), lambda i: (i, 0))
    tab = jax.ShapeDtypeStruct((s_len, LANES), F32)
    w, *tables = pl.pallas_call(
        functools.partial(_pack_kernel, tuple(int(p) for p in perm), tuple(int(f) for f in halve)),
        grid=(n_steps,),
        in_specs=[pl.BlockSpec((table_rows // LANES, LANES), lambda i: (i, 0)),
                  const(freq), const(expand), const(select), const(sign), band()],
        out_specs=[band()] + [tab_spec] * N_ROPE_TABLES,
        out_shape=[jax.ShapeDtypeStruct((k_dim, D_IN), BF16)] + [tab] * N_ROPE_TABLES,
        compiler_params=_compiler_params(("arbitrary",)),
        name="pack_weights",
    )(pos_rows, freq, expand, select, sign, w_in)
    bias = jnp.concatenate([0.5 * b_gate[0], 0.5 * b_gate[1], jnp.zeros((D_IN - 2 * D_MODEL,), F32)])[None, :]
    return w, kinds, bias, tables


ATT_TOKENS = 1024
MIXER_B_TOKENS = 2048
STRIDED_MAX_DILATION = 4
PERM_ROWS = 256


def _band_bias(max_dist, reps):
    q_idx = np.arange(BLOCK)[:, None] + BLOCK
    k_idx = np.arange(2 * BLOCK)[None, :]
    dist = q_idx - k_idx
    band = (dist >= 0) & (dist <= max_dist)
    first = band & (k_idx >= BLOCK)
    both = np.stack([first, band]).astype(bool)
    bias = np.where(both, 0.0, -np.inf).astype(np.float32)
    return jnp.asarray(np.tile(bias, (1, 1, reps)))


def _block_bias(bias_ref, step, b):
    if b == 0:
        return bias_ref[jnp.where(step == 0, 0, 1)]
    return bias_ref[1]


def _swap_lane_halves(t):
    return pltpu.roll(t, LANES // 2, axis=1)


def _mixer_b_head(q, kk, vv, bias):
    v2 = jnp.concatenate([vv, jnp.ones((2 * BLOCK, LANES), BF16)], axis=1)
    s = lax.dot_general(q, kk, (((1,), (1,)), ((), ())), preferred_element_type=F32) + bias
    mb = jnp.broadcast_to(jnp.max(s, axis=1, keepdims=True), (BLOCK, LANES))
    prob = jnp.concatenate([jnp.exp2(s[:, :LANES] - mb), jnp.exp2(s[:, LANES:] - mb)], axis=1).astype(BF16)
    o2 = jnp.dot(prob, v2, preferred_element_type=F32)
    denom = o2[:, LANES:]
    return o2[:, :LANES] / denom, mb * LN_2 + jnp.log(denom)


def _lse_lanes(lse_tile, lse_h, h):
    lane = lax.broadcasted_iota(jnp.int32, (BLOCK, LANES), 1)
    return jnp.where(jnp.logical_and(lane >= h * B_LSE_LANES, lane < (h + 1) * B_LSE_LANES), lse_h, lse_tile)


def _mixer_ab_kernel(a_ref, bias_ref, qb_ref, kb_ref, vb_ref, bias_b_ref, out_ref, ob_ref, lse_ref,
                     kbuf_ref, vbuf_ref, kbbuf_ref, vbbuf_ref):
    step = pl.program_id(0)
    tq = a_ref.shape[0]
    n_blocks = tq // BLOCK
    n_pairs = A_HEADS // 2
    pairs_per_group = n_pairs // A_KV_HEADS

    heads_b = B_HEADS_PER_GROUP
    assert n_pairs % heads_b == 0

    @pl.when(step == 0)
    def _():
        kbuf_ref[0:BLOCK, :] = jnp.zeros((BLOCK, A_KV_WIDTH), BF16)
        vbuf_ref[0:BLOCK, :] = jnp.zeros((BLOCK, A_KV_WIDTH), BF16)
        kbbuf_ref[0:BLOCK, :] = jnp.zeros((BLOCK, B_GROUP_WIDTH), BF16)
        vbbuf_ref[0:BLOCK, :] = jnp.zeros((BLOCK, B_GROUP_WIDTH), BF16)

    kbuf_ref[BLOCK:, :] = a_ref[:, A_COL_K:A_COL_K + A_KV_WIDTH]
    vbuf_ref[BLOCK:, :] = a_ref[:, A_COL_V:A_COL_V + A_KV_WIDTH]
    kbbuf_ref[BLOCK:, :] = kb_ref[...]
    vbbuf_ref[BLOCK:, :] = vb_ref[...]
    lane2 = lax.broadcasted_iota(jnp.int32, (2 * BLOCK, LANES), 1)
    low2 = lane2 < LANES // 2
    sink_slot = lax.broadcasted_iota(jnp.int32, (2 * BLOCK, LANES), 0) == 0
    denom_cols = jnp.concatenate([jnp.where(low2, 1.0, 0.0), jnp.where(low2, 0.0, 1.0)], axis=0).astype(BF16)

    for b in range(n_blocks):
        bias_kind = jnp.where(step == 0, 0, 1) if b == 0 else 1
        rows = slice(b * BLOCK, (b + 1) * BLOCK)
        kk = jnp.where(sink_slot, 0.0, kbuf_ref[b * BLOCK:(b + 2) * BLOCK, :].astype(F32))
        vv = jnp.where(sink_slot, 0.0, vbuf_ref[b * BLOCK:(b + 2) * BLOCK, :].astype(F32))
        kk_sw = _swap_lane_halves(kk)
        vv_sw = _swap_lane_halves(vv)
        k2, v2 = [], []
        for g in range(A_KV_HEADS):
            if g == 0:
                k_top, k_bot = jnp.where(low2, kk, 0.0), jnp.where(low2, 0.0, kk_sw)
                v_top, v_bot = jnp.where(low2, vv, 0.0), jnp.where(low2, 0.0, vv_sw)
            else:
                k_top, k_bot = jnp.where(low2, kk_sw, 0.0), jnp.where(low2, 0.0, kk)
                v_top, v_bot = jnp.where(low2, vv_sw, 0.0), jnp.where(low2, 0.0, vv)
            k2.append(jnp.concatenate([k_top, k_bot], axis=0).astype(BF16))
            v2.append(jnp.concatenate([jnp.concatenate([v_top, v_bot], axis=0).astype(BF16), denom_cols], axis=1))
        for p in range(n_pairs):
            g = p // pairs_per_group
            cols = slice(p * LANES, (p + 1) * LANES)
            s = (lax.dot_general(a_ref[rows, cols], k2[g], (((1,), (1,)), ((), ())), preferred_element_type=F32)
                 + bias_ref[bias_kind, p])
            m0b = jnp.broadcast_to(jnp.max(s[:, :2 * BLOCK], axis=1, keepdims=True), (BLOCK, LANES))
            m1b = jnp.broadcast_to(jnp.max(s[:, 2 * BLOCK:], axis=1, keepdims=True), (BLOCK, LANES))
            shifts = (m0b, m0b, m1b, m1b)
            prob = jnp.concatenate([jnp.exp2(s[:, j * LANES:(j + 1) * LANES] - shifts[j]) for j in range(4)],
                                   axis=1).astype(BF16)
            o2 = jnp.dot(prob, v2[g], preferred_element_type=F32)
            o = o2[:, :LANES] / o2[:, LANES:]
            gate = a_ref[rows, A_COL_GATE + p * LANES:A_COL_GATE + (p + 1) * LANES].astype(F32)
            out_ref[rows, cols] = (o * gate).astype(out_ref.dtype)
    for b in range(n_blocks):
        rows = slice(b * BLOCK, (b + 1) * BLOCK)
        bias_b = _block_bias(bias_b_ref, step, b)
        lse_tile = jnp.zeros((BLOCK, LANES), F32)
        for hb in range(heads_b):
            cols_b = slice(hb * B_HEAD_DIM, (hb + 1) * B_HEAD_DIM)
            o_b, lse_h = _mixer_b_head(qb_ref[rows, cols_b], kbbuf_ref[b * BLOCK:(b + 2) * BLOCK, cols_b],
                                       vbbuf_ref[b * BLOCK:(b + 2) * BLOCK, cols_b], bias_b)
            ob_ref[rows, cols_b] = o_b.astype(ob_ref.dtype)
            lse_tile = _lse_lanes(lse_tile, lse_h, hb)
        lse_ref[rows, :] = lse_tile

    kbuf_ref[0:BLOCK, :] = kbuf_ref[tq:tq + BLOCK, :]
    vbuf_ref[0:BLOCK, :] = vbuf_ref[tq:tq + BLOCK, :]
    kbbuf_ref[0:BLOCK, :] = kbbuf_ref[tq:tq + BLOCK, :]
    vbbuf_ref[0:BLOCK, :] = vbbuf_ref[tq:tq + BLOCK, :]


def _mixer_a_and_b0(h, sinks):
    s_len = h.shape[0]
    tq = ATT_TOKENS
    n_pairs = A_HEADS // 2
    assert B_PATTERNS[0] == (BLOCK, 1)
    gw = B_GROUP_WIDTH
    bias_b = _band_bias(BLOCK, 1)
    blk_b = lambda c: pl.BlockSpec((tq, gw), lambda i: (i, c))
    band = _band_bias(A_WINDOW - 1, 2)
    sink_pairs = (sinks.astype(F32) * LOG2_E).reshape(n_pairs, 2)
    col = jnp.arange(4 * BLOCK)
    bias = jnp.broadcast_to(band[:, None], (2, n_pairs, BLOCK, 4 * BLOCK))
    bias = jnp.where(col == 0, sink_pairs[None, :, 0, None, None], bias)
    bias = jnp.where(col == 2 * BLOCK, sink_pairs[None, :, 1, None, None], bias)
    return pl.pallas_call(
        _mixer_ab_kernel,
        grid=(s_len // tq,),
        in_specs=[
            pl.BlockSpec((tq, PK_A_WIDTH), lambda i: (i, PK_A // PK_A_WIDTH)),
            pl.BlockSpec(bias.shape, lambda i: (0, 0, 0, 0), pipeline_mode=pl.Buffered(1)),
            blk_b(PK_QKB // gw), blk_b(PK_QKB // gw + 1), blk_b(PK_VB // gw),
            pl.BlockSpec(bias_b.shape, lambda i: (0, 0, 0), pipeline_mode=pl.Buffered(1)),
        ],
        out_specs=[pl.BlockSpec((tq, A_WIDTH), lambda i: (i, 0)), pl.BlockSpec((tq, gw), lambda i: (i, 0)),
                   pl.BlockSpec((tq, LANES), lambda i: (i, 0))],
        out_shape=[jax.ShapeDtypeStruct((s_len, A_WIDTH), BF16), jax.ShapeDtypeStruct((s_len, gw), BF16),
                   jax.ShapeDtypeStruct((s_len, LANES), F32)],
        scratch_shapes=[pltpu.VMEM((tq + BLOCK, A_KV_WIDTH), BF16), pltpu.VMEM((tq + BLOCK, A_KV_WIDTH), BF16),
                        pltpu.VMEM((tq + BLOCK, gw), BF16), pltpu.VMEM((tq + BLOCK, gw), BF16)],
        compiler_params=_compiler_params(("arbitrary",)),
        name="mixer_a_b0",
    )(h, bias, h, h, h, bias_b)


def _mixer_b_kernel(d, nb, q_ref, k_ref, v_ref, bias_ref, *refs):
    by_mxu = d > STRIDED_MAX_DILATION
    if d == 1:
        o_ref, lse_ref, qsub, ksub, vsub = refs
    elif by_mxu:
        perm_ref, perm_t_ref, o_ref, lse_ref, qsub, ksub, vsub, ystage, lstage = refs
    else:
        o_ref, lse_ref, qsub, ksub, vsub, slab, ostage, lstage = refs
    step = pl.program_id(0)
    n_heads = B_HEADS_PER_GROUP
    span = BLOCK * d

    @pl.when(step == 0)
    def _():
        ksub[:, 0:BLOCK, :] = jnp.zeros((d, BLOCK, B_GROUP_WIDTH), BF16)
        vsub[:, 0:BLOCK, :] = jnp.zeros((d, BLOCK, B_GROUP_WIDTH), BF16)

    if d == 1:
        qsub[0] = q_ref[...]
        ksub[0, BLOCK:, :] = k_ref[...]
        vsub[0, BLOCK:, :] = v_ref[...]
    elif not by_mxu:
        for src, dst, row0 in ((q_ref, qsub, 0), (k_ref, ksub, BLOCK), (v_ref, vsub, BLOCK)):
            for c in range(n_heads):
                cols = slice(c * LANES, (c + 1) * LANES)
                sl = slab.at[c]
                sl[...] = src[:, cols].astype(F32)
                for r in range(d):
                    for b in range(nb):
                        piece = sl[pl.ds(b * span + r, BLOCK, stride=d), :]
                        dst[r, row0 + b * BLOCK:row0 + (b + 1) * BLOCK, cols] = piece.astype(BF16)
    else:
        per = PERM_ROWS // d
        n_perm_blocks = nb * span // PERM_ROWS
        for src, dst, row0 in ((q_ref, qsub, 0), (k_ref, ksub, BLOCK), (v_ref, vsub, BLOCK)):
            for j in range(n_perm_blocks):
                y = jnp.dot(perm_ref[...], src[j * PERM_ROWS:(j + 1) * PERM_ROWS, :],
                            preferred_element_type=F32).astype(BF16)
                for r in range(d):
                    dst[r, row0 + j * per:row0 + (j + 1) * per, :] = y[r * per:(r + 1) * per, :]

    for r in range(d):
        for b in range(nb):
            bias = _block_bias(bias_ref, step, b)
            rows = slice(b * BLOCK, (b + 1) * BLOCK)
            lse_tile = jnp.zeros((BLOCK, LANES), F32)
            for h in range(n_heads):
                cols = slice(h * B_HEAD_DIM, (h + 1) * B_HEAD_DIM)
                o, lse_h = _mixer_b_head(qsub[r, rows, cols], ksub[r, b * BLOCK:(b + 2) * BLOCK, cols],
                                         vsub[r, b * BLOCK:(b + 2) * BLOCK, cols], bias)
                lse_tile = _lse_lanes(lse_tile, lse_h, h)
                if d == 1:
                    o_ref[rows, cols] = o.astype(o_ref.dtype)
                elif not by_mxu:
                    ostage[h, pl.ds(b * span + r, BLOCK, stride=d), :] = o
                else:
                    o = o.astype(BF16)
                    for jj in range(BLOCK // per):
                        ystage[b * (BLOCK // per) + jj, r * per:(r + 1) * per, cols] = o[jj * per:(jj + 1) * per, :]
            if d == 1:
                lse_ref[rows, :] = lse_tile
            else:
                lstage[pl.ds(b * span + r, BLOCK, stride=d), :] = lse_tile

    if d > 1:
        if by_mxu:
            for j in range(n_perm_blocks):
                o_ref[j * PERM_ROWS:(j + 1) * PERM_ROWS, :] = jnp.dot(
                    perm_t_ref[...], ystage[j], preferred_element_type=F32).astype(o_ref.dtype)
        else:
            for h in range(n_heads):
                o_ref[:, h * B_HEAD_DIM:(h + 1) * B_HEAD_DIM] = ostage[h].astype(o_ref.dtype)
        lse_ref[...] = lstage[...]
    ksub[:, 0:BLOCK, :] = ksub[:, nb * BLOCK:(nb + 1) * BLOCK, :]
    vsub[:, 0:BLOCK, :] = vsub[:, nb * BLOCK:(nb + 1) * BLOCK, :]


def _mixer_b_group(h, gi):
    s_len = h.shape[0]
    window, d = B_PATTERNS[gi]
    assert window // d == BLOCK
    nb = max(1, MIXER_B_TOKENS // (BLOCK * d))
    t_rows = nb * BLOCK * d
    gw = B_GROUP_WIDTH
    col_q, col_v = (PK_QKB + gi * 2 * gw) // gw, (PK_VB + gi * gw) // gw
    bias = _band_bias(BLOCK, 1)
    blk = lambda c: pl.BlockSpec((t_rows, gw), lambda i: (i, c))
    scratch = [pltpu.VMEM((d, nb * BLOCK, gw), BF16), pltpu.VMEM((d, (nb + 1) * BLOCK, gw), BF16),
               pltpu.VMEM((d, (nb + 1) * BLOCK, gw), BF16)]
    operands = [h, h, h, bias]
    in_specs = [blk(col_q), blk(col_q + 1), blk(col_v), pl.BlockSpec(bias.shape, lambda i: (0, 0, 0))]
    if 1 < d <= STRIDED_MAX_DILATION:
        scratch += [pltpu.VMEM((B_HEADS_PER_GROUP, t_rows, LANES), F32),
                    pltpu.VMEM((B_HEADS_PER_GROUP, t_rows, LANES), F32), pltpu.VMEM((t_rows, LANES), F32)]
    elif d > 1:
        assert PERM_ROWS % d == 0 and (PERM_ROWS // d) % 16 == 0 and BLOCK % (PERM_ROWS // d) == 0
        per = PERM_ROWS // d
        perm = np.zeros((PERM_ROWS, PERM_ROWS), np.float32)
        l_idx, r_idx = np.meshgrid(np.arange(per), np.arange(d), indexing="ij")
        perm[r_idx * per + l_idx, l_idx * d + r_idx] = 1.0
        operands += [jnp.asarray(perm, BF16), jnp.asarray(perm.T, BF16)]
        in_specs += [pl.BlockSpec((PERM_ROWS, PERM_ROWS), lambda i: (0, 0))] * 2
        scratch += [pltpu.VMEM((t_rows // PERM_ROWS, PERM_ROWS, gw), BF16), pltpu.VMEM((t_rows, LANES), F32)]
    return pl.pallas_call(
        functools.partial(_mixer_b_kernel, d, nb),
        grid=(s_len // t_rows,),
        in_specs=in_specs,
        out_specs=[pl.BlockSpec((t_rows, gw), lambda i: (i, 0)), pl.BlockSpec((t_rows, LANES), lambda i: (i, 0))],
        out_shape=[jax.ShapeDtypeStruct((s_len, gw), BF16), jax.ShapeDtypeStruct((s_len, LANES), F32)],
        scratch_shapes=scratch,
        compiler_params=_compiler_params(("arbitrary",)),
        name=f"mixer_b_d{d}",
    )(*operands)


TAIL_TM = 512


def _tail_kernel(x_ref, ya_ref, o0_ref, o1_ref, o2_ref, l0_ref, l1_ref, l2_ref, gate_b_ref, sig_a_ref, sig_b_ref,
                 w_pa_ref, w_pb_ref, w_out_ref, ln_g_ref, ln_b_ref, out_ref):
    tm = x_ref.shape[0]
    outs = [o0_ref, o1_ref, o2_ref]
    lses = [l0_ref[...], l1_ref[...], l2_ref[...]]
    m = jnp.maximum(jnp.maximum(lses[0], lses[1]), lses[2])
    es = [jnp.exp(l - m) for l in lses]
    inv = 1.0 / (es[0] + es[1] + es[2])
    wts = [e * inv for e in es]
    yb_cols = []
    for h in range(B_HEADS_PER_GROUP):
        cols = slice(h * B_HEAD_DIM, (h + 1) * B_HEAD_DIM)
        acc = None
        for gi in range(B_N_GROUPS):
            w_h = jnp.broadcast_to(wts[gi][:, h * B_LSE_LANES:h * B_LSE_LANES + 1], (tm, B_HEAD_DIM))
            term = w_h * outs[gi][:, cols].astype(F32)
            acc = term if acc is None else acc + term
        yb_cols.append((acc * gate_b_ref[:, cols].astype(F32)).astype(BF16))
    yb = jnp.concatenate(yb_cols, axis=1)
    y_a = jnp.dot(ya_ref[...], w_pa_ref[...], preferred_element_type=F32)
    y_b = jnp.dot(yb, w_pb_ref[...], preferred_element_type=F32)
    merged = sig_a_ref[...].astype(F32) * y_a + sig_b_ref[...].astype(F32) * y_b
    sub = jnp.dot(merged.astype(BF16), w_out_ref[...], preferred_element_type=F32)
    z = DN_ALPHA * x_ref[...] + sub
    inv_d = 1.0 / z.shape[-1]
    mu = jnp.sum(z, axis=-1, keepdims=True) * inv_d
    var = jnp.sum(z * z, axis=-1, keepdims=True) * inv_d - mu * mu
    out_ref[...] = ((z - mu) * lax.rsqrt(var + LN_EPS) * ln_g_ref[...] + ln_b_ref[...]).astype(out_ref.dtype)


def _tail(x2d, ya, o_groups, lse_groups, h, w_pa, w_pb, w_out, ln_g, ln_b):
    s_len = x2d.shape[0]
    tm = TAIL_TM
    row = lambda width, cblk=0: pl.BlockSpec((tm, width), lambda i: (i, cblk))
    full = lambda a: pl.BlockSpec(a.shape, lambda i: (0,) * a.ndim, pipeline_mode=pl.Buffered(1))
    gw = B_GROUP_WIDTH
    return pl.pallas_call(
        _tail_kernel,
        grid=(s_len // tm,),
        in_specs=[row(D_MODEL), row(A_WIDTH), row(gw), row(gw), row(gw), row(LANES), row(LANES), row(LANES),
                  row(gw, PK_GATE_B // gw), row(D_MODEL, PK_MG_A // D_MODEL), row(D_MODEL, PK_MG_B // D_MODEL),
                  full(w_pa), full(w_pb), full(w_out), full(ln_g), full(ln_b)],
        out_specs=row(D_MODEL),
        out_shape=jax.ShapeDtypeStruct((s_len, D_MODEL), x2d.dtype),
        compiler_params=_compiler_params(("arbitrary",)),
        name="tail",
    )(x2d, ya, *o_groups, *lse_groups, h, h, h, w_pa, w_pb, w_out, ln_g, ln_b)


def _hybrid_layer(x, positions, w_in, b_gate, sinks, w_pa, w_pb, w_out, ln_g, ln_b):
    bn, s_len, d_model = x.shape
    assert bn == 1 and d_model == D_MODEL
    assert s_len % (BLOCK * B_PATTERNS[-1][1]) == 0 and s_len % PROJ_TM == 0
    x2d = x.reshape(s_len, d_model)
    w, kinds, bias, tables = _pack_weights_and_tables(w_in, b_gate, positions)
    h = _in_projection(x2d, w, bias, tables, kinds)
    ya, o, lse = _mixer_a_and_b0(h, sinks)
    o_groups, lse_groups = [o], [lse]
    for gi in range(1, B_N_GROUPS):
        o, lse = _mixer_b_group(h, gi)
        o_groups.append(o)
        lse_groups.append(lse)
    out = _tail(x2d, ya, o_groups, lse_groups, h, w_pa.astype(BF16), w_pb.astype(BF16), w_out.astype(BF16),
                ln_g.reshape(1, d_model).astype(F32), ln_b.reshape(1, d_model).astype(F32))
    return out.reshape(bn, s_len, d_model)


def kernel(x, positions, w_in, b_gate, sinks, w_pa, w_pb, w_out, ln_g, ln_b):
    for layer in range(w_in.shape[0]):
        x = _hybrid_layer(x, positions, w_in[layer], b_gate[layer], sinks[layer], w_pa[layer], w_pb[layer],
                          w_out[layer], ln_g[layer], ln_b[layer])
    return x
```

```python
import functools

import jax
import jax.numpy as jnp
import numpy as np
from jax import lax
from jax.experimental import pallas as pl
from jax.experimental.pallas import tpu as pltpu

F32 = jnp.float32
BF16 = jnp.bfloat16

D_MODEL = 2048
ROPE_THETA = 10000.0
LN_EPS = 1e-5
BLOCK = 128
LANES = 128
A_HEADS = 16
A_KV_HEADS = 2
A_HEAD_DIM = 64
A_WINDOW = 128
A_WIDTH = A_HEADS * A_HEAD_DIM
A_KV_WIDTH = A_KV_HEADS * A_HEAD_DIM
B_PATTERNS = ((128, 1), (512, 4), (2048, 16))
B_HEADS_PER_GROUP = 4
B_HEAD_DIM = 128
B_GROUP_WIDTH = B_HEADS_PER_GROUP * B_HEAD_DIM
B_N_GROUPS = len(B_PATTERNS)
B_QKV_WIDTH = B_N_GROUPS * B_GROUP_WIDTH
B_LSE_LANES = LANES // B_HEADS_PER_GROUP
DEPTH = 1
DN_ALPHA = float((2 * DEPTH) ** 0.25)

IN_SIZES = (A_WIDTH, A_KV_WIDTH, A_KV_WIDTH, A_WIDTH, B_QKV_WIDTH, B_QKV_WIDTH, B_QKV_WIDTH,
            B_GROUP_WIDTH, D_MODEL, D_MODEL)
IN_OFFSETS = tuple(int(o) for o in np.cumsum((0,) + IN_SIZES[:-1]))
(OFF_QA, OFF_KA, OFF_VA, OFF_GATE_A, OFF_QB, OFF_KB, OFF_VB, OFF_GATE_B, OFF_MG_A, OFF_MG_B) = IN_OFFSETS
D_IN = sum(IN_SIZES)

VMEM_LIMIT_BYTES = 60 * 1024 * 1024

EP_NONE, EP_ROPE64, EP_ROPE128, EP_SILU, EP_SIGMOID, EP_ROPE64_Q, EP_ROPE128_Q = range(7)

LOG2_E = float(np.log2(np.e))
LN_2 = float(np.log(2.0))
Q_SCALE_A = A_HEAD_DIM ** -0.5 * LOG2_E
Q_SCALE_B = B_HEAD_DIM ** -0.5 * LOG2_E

PROJ_TM = 1024
PROJ_TN = 2304

PK_MG_A = 0
PK_MG_B = PK_MG_A + D_MODEL
PK_VB = PK_MG_B + D_MODEL
PK_GATE_B = PK_VB + B_QKV_WIDTH
PK_QKB = PK_GATE_B + B_GROUP_WIDTH
PK_A = PK_QKB + 2 * B_QKV_WIDTH
PK_A_WIDTH = 2 * A_WIDTH + 2 * A_KV_WIDTH
A_COL_K = A_WIDTH
A_COL_V = A_COL_K + A_KV_WIDTH
A_COL_GATE = A_COL_V + A_KV_WIDTH
assert PK_A + PK_A_WIDTH == D_IN and PK_A % PK_A_WIDTH == 0 and PK_VB % B_GROUP_WIDTH == 0
assert PK_GATE_B % B_GROUP_WIDTH == 0 and PK_QKB % B_GROUP_WIDTH == 0 and D_IN % PROJ_TN == 0


def _compiler_params(semantics):
    return pltpu.CompilerParams(dimension_semantics=semantics, vmem_limit_bytes=VMEM_LIMIT_BYTES)


N_ROPE_TABLES = 4


def _bf16_pieces(t):
    hi = t.astype(BF16)
    rest = t - hi.astype(F32)
    mid = rest.astype(BF16)
    low = (rest - mid.astype(F32)).astype(BF16)
    return jnp.concatenate([hi, mid, low], axis=1)


def _rope_tables(pos_rows, freq, expand_ref, select_ref, sign_ref):
    n_chunks = pos_rows.shape[0] // 2
    eye = (lax.broadcasted_iota(jnp.int32, (LANES, LANES), 0)
           == lax.broadcasted_iota(jnp.int32, (LANES, LANES), 1))
    pos2 = []
    for c in range(n_chunks):
        diag = jnp.concatenate([jnp.where(eye, pos_rows[c:c + 1, :], 0.0),
                                jnp.where(eye, pos_rows[n_chunks + c:n_chunks + c + 1, :], 0.0)], axis=1)
        pos2.append(jnp.dot(_bf16_pieces(diag), expand_ref[...], preferred_element_type=F32))
    ang = jnp.concatenate(pos2, axis=0) * freq

    def to_head_layout(t):
        r = jnp.dot(_bf16_pieces(t), select_ref[...], preferred_element_type=F32)
        return [jnp.concatenate([r[:, (2 * layout) * LANES:(2 * layout + 1) * LANES],
                                 r[:, (2 * layout + 1) * LANES:(2 * layout + 2) * LANES]], axis=0)
                for layout in range(2)]

    cos_a, cos_b = to_head_layout(jnp.cos(ang))
    sin_a, sin_b = to_head_layout(jnp.sin(ang))
    return [t * sign_ref[ti:ti + 1, :] for ti, t in enumerate((cos_a, sin_a, cos_b, sin_b))]


def _first_half_of_head64(lane):
    return jnp.bitwise_and(lane, A_HEAD_DIM - 1) < A_HEAD_DIM // 2


def _rope_inputs(positions):
    s_len = positions.shape[1]
    half_b = B_HEAD_DIM // 2
    assert A_HEAD_DIM * 2 == B_HEAD_DIM and 2 * half_b == LANES
    inv_b = ROPE_THETA ** (-jnp.arange(half_b, dtype=F32) / half_b)
    freq = jnp.concatenate([inv_b, inv_b])[None, :]
    pos_rows = positions.astype(F32).reshape(s_len // LANES, LANES)
    lane = np.arange(LANES)
    expand = np.zeros((2 * LANES, LANES), np.float32)
    expand[:LANES, :half_b] = 1.0
    expand[LANES:, half_b:] = 1.0
    expand = np.concatenate([expand] * 3, axis=0)
    src_lane = np.stack([2 * (lane % (A_HEAD_DIM // 2)), lane % half_b])
    select = np.zeros((LANES, 4 * LANES), np.float32)
    for layout in range(2):
        for hf in range(2):
            select[hf * half_b + src_lane[layout], (2 * layout + hf) * LANES + lane] = 1.0
    select = np.concatenate([select] * 3, axis=0)
    sign = np.ones((N_ROPE_TABLES, LANES), np.float32)
    sign[1, lane % A_HEAD_DIM < A_HEAD_DIM // 2] = -1.0
    sign[3, lane < half_b] = -1.0
    return pos_rows, freq, jnp.asarray(expand, BF16), jnp.asarray(select, BF16), jnp.asarray(sign)


def _sigmoid_of_twice(h):
    return 0.5 * jnp.tanh(h) + 0.5


def _proj_kernel(tile_kinds, x_ref, w_ref, bias_ref, cos_a_ref, sin_a_ref, cos_b_ref, sin_b_ref, out_ref, xb_ref):
    n = pl.program_id(1)

    @pl.when(n == 0)
    def _():
        xb_ref[...] = x_ref[...].astype(BF16)

    def body(kinds):
        acc = jnp.dot(xb_ref[...], w_ref[...], preferred_element_type=F32)
        first_half = _first_half_of_head64(lax.broadcasted_iota(jnp.int32, (x_ref.shape[0], LANES), 1))
        for ci, kind in enumerate(kinds):
            cols = slice(ci * LANES, (ci + 1) * LANES)
            t = acc[:, cols]
            if kind in (EP_ROPE64, EP_ROPE64_Q):
                rot = jnp.where(first_half, pltpu.roll(t, 96, axis=1), pltpu.roll(t, 32, axis=1))
                t = t * cos_a_ref[...] + rot * sin_a_ref[...]
                if kind == EP_ROPE64_Q:
                    t = t * Q_SCALE_A
            elif kind in (EP_ROPE128, EP_ROPE128_Q):
                t = t * cos_b_ref[...] + pltpu.roll(t, 64, axis=1) * sin_b_ref[...]
                if kind == EP_ROPE128_Q:
                    t = t * Q_SCALE_B
            elif kind == EP_SILU:
                t = t * jnp.tanh(t) + t
            elif kind == EP_SIGMOID:
                t = _sigmoid_of_twice(t + bias_ref[:, cols])
            out_ref[:, cols] = t.astype(out_ref.dtype)

    branches = []
    for i, kinds in enumerate(tile_kinds):
        same = [br for br in branches if br[0] == kinds]
        if same:
            same[0][1].append(i)
        else:
            branches.append((kinds, [i]))
    for kinds, tiles in branches:
        cond = n == tiles[0]
        for i in tiles[1:]:
            cond = cond | (n == i)
        pl.when(cond)(functools.partial(body, kinds))


def _in_projection(x2d, w, bias, tables, chunk_kinds):
    s_len, k_dim = x2d.shape
    n_cols = w.shape[1]
    n_tiles = n_cols // PROJ_TN
    chunks_per_tile = PROJ_TN // LANES
    tile_kinds = tuple(tuple(chunk_kinds[t * chunks_per_tile:(t + 1) * chunks_per_tile]) for t in range(n_tiles))
    tm = min(PROJ_TM, s_len)
    row_spec = lambda width: pl.BlockSpec((tm, width), lambda m, n: (m, 0))
    return pl.pallas_call(
        functools.partial(_proj_kernel, tile_kinds),
        grid=(s_len // tm, n_tiles),
        in_specs=[row_spec(k_dim),
                  pl.BlockSpec((k_dim, PROJ_TN), lambda m, n: (0, n)),
                  pl.BlockSpec((1, PROJ_TN), lambda m, n: (0, n))] + [row_spec(LANES)] * N_ROPE_TABLES,
        out_specs=pl.BlockSpec((tm, PROJ_TN), lambda m, n: (m, n)),
        out_shape=jax.ShapeDtypeStruct((s_len, n_cols), BF16),
        scratch_shapes=[pltpu.VMEM((tm, k_dim), BF16)],
        compiler_params=_compiler_params(("arbitrary", "arbitrary")),
        name="in_projection",
    )(x2d, w, bias, *tables)


PACK_COLS = 256
PACK_ROWS = 256


def _packed_layout():
    g = B_GROUP_WIDTH
    assert OFF_VA == OFF_KA + A_KV_WIDTH
    segments = [(OFF_MG_A, D_MODEL, [EP_SIGMOID]), (OFF_MG_B, D_MODEL, [EP_SIGMOID])]
    segments += [(OFF_VB, B_QKV_WIDTH, [EP_NONE]), (OFF_GATE_B, g, [EP_SILU])]
    for gi in range(B_N_GROUPS):
        segments += [(OFF_QB + gi * g, g, [EP_ROPE128_Q]), (OFF_KB + gi * g, g, [EP_ROPE128])]
    segments += [(OFF_QA, A_WIDTH, [EP_ROPE64_Q]),
                 (OFF_KA, 2 * A_KV_WIDTH, [EP_ROPE64] * (A_KV_WIDTH // LANES) + [EP_NONE] * (A_KV_WIDTH // LANES)),
                 (OFF_GATE_A, A_WIDTH, [EP_SILU])]
    perm, halve, kinds = [], [], []
    for off, width, seg_kinds in segments:
        assert off % PACK_COLS == 0 and width % PACK_COLS == 0
        perm += [off // PACK_COLS + j for j in range(width // PACK_COLS)]
        halve += [int(seg_kinds[0] in (EP_SIGMOID, EP_SILU))] * (width // PACK_COLS)
        kinds += seg_kinds * (width // LANES // len(seg_kinds))
    assert sorted(perm) == list(range(D_IN // PACK_COLS)) and len(kinds) == D_IN // LANES
    return np.asarray(perm, np.int32), np.asarray(halve, np.int32), kinds


def _pack_kernel(perm, halve, pos_ref, freq_ref, expand_ref, select_ref, sign_ref, w_ref, out_ref, *table_refs):
    for j, (src, half) in enumerate(zip(perm, halve)):
        block = w_ref[:, src * PACK_COLS:(src + 1) * PACK_COLS]
        if half:
            block = block * 0.5
        out_ref[:, j * PACK_COLS:(j + 1) * PACK_COLS] = block.astype(out_ref.dtype)
    tables = _rope_tables(pos_ref[...], freq_ref[...], expand_ref, select_ref, sign_ref)
    for t_ref, t in zip(table_refs, tables):
        t_ref[...] = t


def _pack_weights_and_tables(w_in, b_gate, positions):
    perm, halve, kinds = _packed_layout()
    k_dim = w_in.shape[0]
    n_steps = k_dim // PACK_ROWS
    pos_rows, freq, expand, select, sign = _rope_inputs(positions)
    s_len = positions.shape[1]
    assert s_len % (8 * LANES * n_steps) == 0
    const = lambda a: pl.BlockSpec(a.shape, lambda i: (0,) * a.ndim)
    table_rows = s_len // n_steps
    tab_spec = pl.BlockSpec((table_rows, LANES), lambda i: (i, 0))
    band = lambda: pl.BlockSpec((PACK_ROWS, D_IN), lambda i: (i, 0))
    tab = jax.ShapeDtypeStruct((s_len, LANES), F32)
    w, *tables = pl.pallas_call(
        functools.partial(_pack_kernel, tuple(int(p) for p in perm), tuple(int(f) for f in halve)),
        grid=(n_steps,),
        in_specs=[pl.BlockSpec((table_rows // LANES, LANES), lambda i: (i, 0)),
                  const(freq), const(expand), const(select), const(sign), band()],
        out_specs=[band()] + [tab_spec] * N_ROPE_TABLES,
        out_shape=[jax.ShapeDtypeStruct((k_dim, D_IN), BF16)] + [tab] * N_ROPE_TABLES,
        compiler_params=_compiler_params(("arbitrary",)),
        name="pack_weights",
    )(pos_rows, freq, expand, select, sign, w_in)
    bias = jnp.concatenate([0.5 * b_gate[0], 0.5 * b_gate[1], jnp.zeros((D_IN - 2 * D_MODEL,), F32)])[None, :]
    return w, kinds, bias, tables


ATT_TOKENS = 1024
MIXER_B_TOKENS = 2048
STRIDED_MAX_DILATION = 4
PERM_ROWS = 256


def _band_bias(max_dist, reps):
    q_idx = np.arange(BLOCK)[:, None] + BLOCK
    k_idx = np.arange(2 * BLOCK)[None, :]
    dist = q_idx - k_idx
    band = (dist >= 0) & (dist <= max_dist)
    first = band & (k_idx >= BLOCK)
    both = np.stack([first, band]).astype(bool)
    bias = np.where(both, 0.0, -np.inf).astype(np.float32)
    return jnp.asarray(np.tile(bias, (1, 1, reps)))


def _block_bias(bias_ref, step, b):
    if b == 0:
        return bias_ref[jnp.where(step == 0, 0, 1)]
    return bias_ref[1]


def _swap_lane_halves(t):
    return pltpu.roll(t, LANES // 2, axis=1)


def _mixer_b_head(q, kk, vv, bias):
    v2 = jnp.concatenate([vv, jnp.ones((2 * BLOCK, LANES), BF16)], axis=1)
    s = lax.dot_general(q, kk, (((1,), (1,)), ((), ())), preferred_element_type=F32) + bias
    mb = jnp.broadcast_to(jnp.max(s, axis=1, keepdims=True), (BLOCK, LANES))
    prob = jnp.concatenate([jnp.exp2(s[:, :LANES] - mb), jnp.exp2(s[:, LANES:] - mb)], axis=1).astype(BF16)
    o2 = jnp.dot(prob, v2, preferred_element_type=F32)
    denom = o2[:, LANES:]
    return o2[:, :LANES] / denom, mb * LN_2 + jnp.log(denom)


def _lse_lanes(lse_tile, lse_h, h):
    lane = lax.broadcasted_iota(jnp.int32, (BLOCK, LANES), 1)
    return jnp.where(jnp.logical_and(lane >= h * B_LSE_LANES, lane < (h + 1) * B_LSE_LANES), lse_h, lse_tile)


def _mixer_ab_kernel(a_ref, bias_ref, qb_ref, kb_ref, vb_ref, bias_b_ref, qd_ref, kd_ref, vd_ref,
                     out_ref, ob_ref, lse_ref, od_ref, lsed_ref,
                     kbuf_ref, vbuf_ref, kbbuf_ref, vbbuf_ref, *dilated_scratch):
    step = pl.program_id(0)
    tq = a_ref.shape[0]
    n_blocks = tq // BLOCK
    n_pairs = A_HEADS // 2
    pairs_per_group = n_pairs // A_KV_HEADS

    heads_b = B_HEADS_PER_GROUP
    assert n_pairs % heads_b == 0

    @pl.when(step == 0)
    def _():
        kbuf_ref[0:BLOCK, :] = jnp.zeros((BLOCK, A_KV_WIDTH), BF16)
        vbuf_ref[0:BLOCK, :] = jnp.zeros((BLOCK, A_KV_WIDTH), BF16)
        kbbuf_ref[0:BLOCK, :] = jnp.zeros((BLOCK, B_GROUP_WIDTH), BF16)
        vbbuf_ref[0:BLOCK, :] = jnp.zeros((BLOCK, B_GROUP_WIDTH), BF16)

    kbuf_ref[BLOCK:, :] = a_ref[:, A_COL_K:A_COL_K + A_KV_WIDTH]
    vbuf_ref[BLOCK:, :] = a_ref[:, A_COL_V:A_COL_V + A_KV_WIDTH]
    kbbuf_ref[BLOCK:, :] = kb_ref[...]
    vbbuf_ref[BLOCK:, :] = vb_ref[...]
    lane2 = lax.broadcasted_iota(jnp.int32, (2 * BLOCK, LANES), 1)
    low2 = lane2 < LANES // 2
    sink_slot = lax.broadcasted_iota(jnp.int32, (2 * BLOCK, LANES), 0) == 0
    denom_cols = jnp.concatenate([jnp.where(low2, 1.0, 0.0), jnp.where(low2, 0.0, 1.0)], axis=0).astype(BF16)

    for b in range(n_blocks):
        bias_kind = jnp.where(step == 0, 0, 1) if b == 0 else 1
        rows = slice(b * BLOCK, (b + 1) * BLOCK)
        kk = jnp.where(sink_slot, 0.0, kbuf_ref[b * BLOCK:(b + 2) * BLOCK, :].astype(F32))
        vv = jnp.where(sink_slot, 0.0, vbuf_ref[b * BLOCK:(b + 2) * BLOCK, :].astype(F32))
        kk_sw = _swap_lane_halves(kk)
        vv_sw = _swap_lane_halves(vv)
        k2, v2 = [], []
        for g in range(A_KV_HEADS):
            if g == 0:
                k_top, k_bot = jnp.where(low2, kk, 0.0), jnp.where(low2, 0.0, kk_sw)
                v_top, v_bot = jnp.where(low2, vv, 0.0), jnp.where(low2, 0.0, vv_sw)
            else:
                k_top, k_bot = jnp.where(low2, kk_sw, 0.0), jnp.where(low2, 0.0, kk)
                v_top, v_bot = jnp.where(low2, vv_sw, 0.0), jnp.where(low2, 0.0, vv)
            k2.append(jnp.concatenate([k_top, k_bot], axis=0).astype(BF16))
            v2.append(jnp.concatenate([jnp.concatenate([v_top, v_bot], axis=0).astype(BF16), denom_cols], axis=1))
        for p in range(n_pairs):
            g = p // pairs_per_group
            cols = slice(p * LANES, (p + 1) * LANES)
            s = (lax.dot_general(a_ref[rows, cols], k2[g], (((1,), (1,)), ((), ())), preferred_element_type=F32)
                 + bias_ref[bias_kind, p])
            m0b = jnp.broadcast_to(jnp.max(s[:, :2 * BLOCK], axis=1, keepdims=True), (BLOCK, LANES))
            m1b = jnp.broadcast_to(jnp.max(s[:, 2 * BLOCK:], axis=1, keepdims=True), (BLOCK, LANES))
            shifts = (m0b, m0b, m1b, m1b)
            prob = jnp.concatenate([jnp.exp2(s[:, j * LANES:(j + 1) * LANES] - shifts[j]) for j in range(4)],
                                   axis=1).astype(BF16)
            o2 = jnp.dot(prob, v2[g], preferred_element_type=F32)
            o = o2[:, :LANES] / o2[:, LANES:]
            gate = a_ref[rows, A_COL_GATE + p * LANES:A_COL_GATE + (p + 1) * LANES].astype(F32)
            out_ref[rows, cols] = (o * gate).astype(out_ref.dtype)
    for b in range(n_blocks):
        rows = slice(b * BLOCK, (b + 1) * BLOCK)
        bias_b = _block_bias(bias_b_ref, step, b)
        lse_tile = jnp.zeros((BLOCK, LANES), F32)
        for hb in range(heads_b):
            cols_b = slice(hb * B_HEAD_DIM, (hb + 1) * B_HEAD_DIM)
            o_b, lse_h = _mixer_b_head(qb_ref[rows, cols_b], kbbuf_ref[b * BLOCK:(b + 2) * BLOCK, cols_b],
                                       vbbuf_ref[b * BLOCK:(b + 2) * BLOCK, cols_b], bias_b)
            ob_ref[rows, cols_b] = o_b.astype(ob_ref.dtype)
            lse_tile = _lse_lanes(lse_tile, lse_h, hb)
        lse_ref[rows, :] = lse_tile

    kbuf_ref[0:BLOCK, :] = kbuf_ref[tq:tq + BLOCK, :]
    vbuf_ref[0:BLOCK, :] = vbuf_ref[tq:tq + BLOCK, :]
    kbbuf_ref[0:BLOCK, :] = kbbuf_ref[tq:tq + BLOCK, :]
    vbbuf_ref[0:BLOCK, :] = vbbuf_ref[tq:tq + BLOCK, :]

    d = B_PATTERNS[1][1]
    _mixer_b_kernel(d, tq // (BLOCK * d), qd_ref, kd_ref, vd_ref, bias_b_ref, od_ref, lsed_ref, *dilated_scratch)


def _mixer_a_and_b0(h, sinks):
    s_len = h.shape[0]
    tq = ATT_TOKENS
    n_pairs = A_HEADS // 2
    assert B_PATTERNS[0] == (BLOCK, 1)
    window1, d1 = B_PATTERNS[1]
    nb1 = tq // (BLOCK * d1)
    assert window1 // d1 == BLOCK and 1 < d1 <= STRIDED_MAX_DILATION and nb1 >= 1
    gw = B_GROUP_WIDTH
    bias_b = _band_bias(BLOCK, 1)
    blk_b = lambda c: pl.BlockSpec((tq, gw), lambda i: (i, c))
    band = _band_bias(A_WINDOW - 1, 2)
    sink_pairs = (sinks.astype(F32) * LOG2_E).reshape(n_pairs, 2)
    col = jnp.arange(4 * BLOCK)
    bias = jnp.broadcast_to(band[:, None], (2, n_pairs, BLOCK, 4 * BLOCK))
    bias = jnp.where(col == 0, sink_pairs[None, :, 0, None, None], bias)
    bias = jnp.where(col == 2 * BLOCK, sink_pairs[None, :, 1, None, None], bias)
    return pl.pallas_call(
        _mixer_ab_kernel,
        grid=(s_len // tq,),
        in_specs=[
            pl.BlockSpec((tq, PK_A_WIDTH), lambda i: (i, PK_A // PK_A_WIDTH)),
            pl.BlockSpec(bias.shape, lambda i: (0, 0, 0, 0), pipeline_mode=pl.Buffered(1)),
            blk_b(PK_QKB // gw), blk_b(PK_QKB // gw + 1), blk_b(PK_VB // gw),
            pl.BlockSpec(bias_b.shape, lambda i: (0, 0, 0), pipeline_mode=pl.Buffered(1)),
            blk_b(PK_QKB // gw + 2), blk_b(PK_QKB // gw + 3), blk_b(PK_VB // gw + 1),
        ],
        out_specs=[pl.BlockSpec((tq, A_WIDTH), lambda i: (i, 0))]
        + [pl.BlockSpec((tq, gw), lambda i: (i, 0)), pl.BlockSpec((tq, LANES), lambda i: (i, 0))] * 2,
        out_shape=[jax.ShapeDtypeStruct((s_len, A_WIDTH), BF16)]
        + [jax.ShapeDtypeStruct((s_len, gw), BF16), jax.ShapeDtypeStruct((s_len, LANES), F32)] * 2,
        scratch_shapes=[pltpu.VMEM((tq + BLOCK, A_KV_WIDTH), BF16), pltpu.VMEM((tq + BLOCK, A_KV_WIDTH), BF16),
                        pltpu.VMEM((tq + BLOCK, gw), BF16), pltpu.VMEM((tq + BLOCK, gw), BF16),
                        pltpu.VMEM((d1, nb1 * BLOCK, gw), BF16), pltpu.VMEM((d1, (nb1 + 1) * BLOCK, gw), BF16),
                        pltpu.VMEM((d1, (nb1 + 1) * BLOCK, gw), BF16),
                        pltpu.VMEM((B_HEADS_PER_GROUP, tq, LANES), F32),
                        pltpu.VMEM((B_HEADS_PER_GROUP, tq, LANES), F32), pltpu.VMEM((tq, LANES), F32)],
        compiler_params=_compiler_params(("arbitrary",)),
        name="mixer_a_b01",
    )(h, bias, h, h, h, bias_b, h, h, h)


def _mixer_b_kernel(d, nb, q_ref, k_ref, v_ref, bias_ref, *refs):
    by_mxu = d > STRIDED_MAX_DILATION
    if d == 1:
        o_ref, lse_ref, qsub, ksub, vsub = refs
    elif by_mxu:
        perm_ref, perm_t_ref, o_ref, lse_ref, qsub, ksub, vsub, ystage, lstage = refs
    else:
        o_ref, lse_ref, qsub, ksub, vsub, slab, ostage, lstage = refs
    step = pl.program_id(0)
    n_heads = B_HEADS_PER_GROUP
    span = BLOCK * d

    @pl.when(step == 0)
    def _():
        ksub[:, 0:BLOCK, :] = jnp.zeros((d, BLOCK, B_GROUP_WIDTH), BF16)
        vsub[:, 0:BLOCK, :] = jnp.zeros((d, BLOCK, B_GROUP_WIDTH), BF16)

    if d == 1:
        qsub[0] = q_ref[...]
        ksub[0, BLOCK:, :] = k_ref[...]
        vsub[0, BLOCK:, :] = v_ref[...]
    elif not by_mxu:
        for src, dst, row0 in ((q_ref, qsub, 0), (k_ref, ksub, BLOCK), (v_ref, vsub, BLOCK)):
            for c in range(n_heads):
                cols = slice(c * LANES, (c + 1) * LANES)
                sl = slab.at[c]
                sl[...] = src[:, cols].astype(F32)
                for r in range(d):
                    for b in range(nb):
                        piece = sl[pl.ds(b * span + r, BLOCK, stride=d), :]
                        dst[r, row0 + b * BLOCK:row0 + (b + 1) * BLOCK, cols] = piece.astype(BF16)
    else:
        per = PERM_ROWS // d
        n_perm_blocks = nb * span // PERM_ROWS
        for src, dst, row0 in ((q_ref, qsub, 0), (k_ref, ksub, BLOCK), (v_ref, vsub, BLOCK)):
            for j in range(n_perm_blocks):
                y = jnp.dot(perm_ref[...], src[j * PERM_ROWS:(j + 1) * PERM_ROWS, :],
                            preferred_element_type=F32).astype(BF16)
                for r in range(d):
                    dst[r, row0 + j * per:row0 + (j + 1) * per, :] = y[r * per:(r + 1) * per, :]

    for r in range(d):
        for b in range(nb):
            bias = _block_bias(bias_ref, step, b)
            rows = slice(b * BLOCK, (b + 1) * BLOCK)
            lse_tile = jnp.zeros((BLOCK, LANES), F32)
            for h in range(n_heads):
                cols = slice(h * B_HEAD_DIM, (h + 1) * B_HEAD_DIM)
                o, lse_h = _mixer_b_head(qsub[r, rows, cols], ksub[r, b * BLOCK:(b + 2) * BLOCK, cols],
                                         vsub[r, b * BLOCK:(b + 2) * BLOCK, cols], bias)
                lse_tile = _lse_lanes(lse_tile, lse_h, h)
                if d == 1:
                    o_ref[rows, cols] = o.astype(o_ref.dtype)
                elif not by_mxu:
                    ostage[h, pl.ds(b * span + r, BLOCK, stride=d), :] = o
                else:
                    o = o.astype(BF16)
                    for jj in range(BLOCK // per):
                        ystage[b * (BLOCK // per) + jj, r * per:(r + 1) * per, cols] = o[jj * per:(jj + 1) * per, :]
            if d == 1:
                lse_ref[rows, :] = lse_tile
            else:
                lstage[pl.ds(b * span + r, BLOCK, stride=d), :] = lse_tile

    if d > 1:
        if by_mxu:
            for j in range(n_perm_blocks):
                o_ref[j * PERM_ROWS:(j + 1) * PERM_ROWS, :] = jnp.dot(
                    perm_t_ref[...], ystage[j], preferred_element_type=F32).astype(o_ref.dtype)
        else:
            for h in range(n_heads):
                o_ref[:, h * B_HEAD_DIM:(h + 1) * B_HEAD_DIM] = ostage[h].astype(o_ref.dtype)
        lse_ref[...] = lstage[...]
    ksub[:, 0:BLOCK, :] = ksub[:, nb * BLOCK:(nb + 1) * BLOCK, :]
    vsub[:, 0:BLOCK, :] = vsub[:, nb * BLOCK:(nb + 1) * BLOCK, :]


def _mixer_b_group(h, gi):
    s_len = h.shape[0]
    window, d = B_PATTERNS[gi]
    assert window // d == BLOCK
    nb = max(1, MIXER_B_TOKENS // (BLOCK * d))
    t_rows = nb * BLOCK * d
    gw = B_GROUP_WIDTH
    col_q, col_v = (PK_QKB + gi * 2 * gw) // gw, (PK_VB + gi * gw) // gw
    bias = _band_bias(BLOCK, 1)
    blk = lambda c: pl.BlockSpec((t_rows, gw), lambda i: (i, c))
    scratch = [pltpu.VMEM((d, nb * BLOCK, gw), BF16), pltpu.VMEM((d, (nb + 1) * BLOCK, gw), BF16),
               pltpu.VMEM((d, (nb + 1) * BLOCK, gw), BF16)]
    operands = [h, h, h, bias]
    in_specs = [blk(col_q), blk(col_q + 1), blk(col_v), pl.BlockSpec(bias.shape, lambda i: (0, 0, 0))]
    if 1 < d <= STRIDED_MAX_DILATION:
        scratch += [pltpu.VMEM((B_HEADS_PER_GROUP, t_rows, LANES), F32),
                    pltpu.VMEM((B_HEADS_PER_GROUP, t_rows, LANES), F32), pltpu.VMEM((t_rows, LANES), F32)]
    elif d > 1:
        assert PERM_ROWS % d == 0 and (PERM_ROWS // d) % 16 == 0 and BLOCK % (PERM_ROWS // d) == 0
        per = PERM_ROWS // d
        perm = np.zeros((PERM_ROWS, PERM_ROWS), np.float32)
        l_idx, r_idx = np.meshgrid(np.arange(per), np.arange(d), indexing="ij")
        perm[r_idx * per + l_idx, l_idx * d + r_idx] = 1.0
        operands += [jnp.asarray(perm, BF16), jnp.asarray(perm.T, BF16)]
        in_specs += [pl.BlockSpec((PERM_ROWS, PERM_ROWS), lambda i: (0, 0))] * 2
        scratch += [pltpu.VMEM((t_rows // PERM_ROWS, PERM_ROWS, gw), BF16), pltpu.VMEM((t_rows, LANES), F32)]
    return pl.pallas_call(
        functools.partial(_mixer_b_kernel, d, nb),
        grid=(s_len // t_rows,),
        in_specs=in_specs,
        out_specs=[pl.BlockSpec((t_rows, gw), lambda i: (i, 0)), pl.BlockSpec((t_rows, LANES), lambda i: (i, 0))],
        out_shape=[jax.ShapeDtypeStruct((s_len, gw), BF16), jax.ShapeDtypeStruct((s_len, LANES), F32)],
        scratch_shapes=scratch,
        compiler_params=_compiler_params(("arbitrary",)),
        name=f"mixer_b_d{d}",
    )(*operands)


TAIL_TM = 512


def _tail_kernel(x_ref, ya_ref, o0_ref, o1_ref, o2_ref, l0_ref, l1_ref, l2_ref, gate_b_ref, sig_a_ref, sig_b_ref,
                 w_pa_ref, w_pb_ref, w_out_ref, ln_g_ref, ln_b_ref, out_ref):
    tm = x_ref.shape[0]
    outs = [o0_ref, o1_ref, o2_ref]
    lses = [l0_ref[...], l1_ref[...], l2_ref[...]]
    m = jnp.maximum(jnp.maximum(lses[0], lses[1]), lses[2])
    es = [jnp.exp(l - m) for l in lses]
    inv = 1.0 / (es[0] + es[1] + es[2])
    wts = [e * inv for e in es]
    yb_cols = []
    for h in range(B_HEADS_PER_GROUP):
        cols = slice(h * B_HEAD_DIM, (h + 1) * B_HEAD_DIM)
        acc = None
        for gi in range(B_N_GROUPS):
            w_h = jnp.broadcast_to(wts[gi][:, h * B_LSE_LANES:h * B_LSE_LANES + 1], (tm, B_HEAD_DIM))
            term = w_h * outs[gi][:, cols].astype(F32)
            acc = term if acc is None else acc + term
        yb_cols.append((acc * gate_b_ref[:, cols].astype(F32)).astype(BF16))
    yb = jnp.concatenate(yb_cols, axis=1)
    y_a = jnp.dot(ya_ref[...], w_pa_ref[...], preferred_element_type=F32)
    y_b = jnp.dot(yb, w_pb_ref[...], preferred_element_type=F32)
    merged = sig_a_ref[...].astype(F32) * y_a + sig_b_ref[...].astype(F32) * y_b
    sub = jnp.dot(merged.astype(BF16), w_out_ref[...], preferred_element_type=F32)
    z = DN_ALPHA * x_ref[...] + sub
    inv_d = 1.0 / z.shape[-1]
    mu = jnp.sum(z, axis=-1, keepdims=True) * inv_d
    var = jnp.sum(z * z, axis=-1, keepdims=True) * inv_d - mu * mu
    out_ref[...] = ((z - mu) * lax.rsqrt(var + LN_EPS) * ln_g_ref[...] + ln_b_ref[...]).astype(out_ref.dtype)


def _tail(x2d, ya, o_groups, lse_groups, h, w_pa, w_pb, w_out, ln_g, ln_b):
    s_len = x2d.shape[0]
    tm = TAIL_TM
    row = lambda width, cblk=0: pl.BlockSpec((tm, width), lambda i: (i, cblk))
    full = lambda a: pl.BlockSpec(a.shape, lambda i: (0,) * a.ndim, pipeline_mode=pl.Buffered(1))
    gw = B_GROUP_WIDTH
    return pl.pallas_call(
        _tail_kernel,
        grid=(s_len // tm,),
        in_specs=[row(D_MODEL), row(A_WIDTH), row(gw), row(gw), row(gw), row(LANES), row(LANES), row(LANES),
                  row(gw, PK_GATE_B // gw), row(D_MODEL, PK_MG_A // D_MODEL), row(D_MODEL, PK_MG_B // D_MODEL),
                  full(w_pa), full(w_pb), full(w_out), full(ln_g), full(ln_b)],
        out_specs=row(D_MODEL),
        out_shape=jax.ShapeDtypeStruct((s_len, D_MODEL), x2d.dtype),
        compiler_params=_compiler_params(("arbitrary",)),
        name="tail",
    )(x2d, ya, *o_groups, *lse_groups, h, h, h, w_pa, w_pb, w_out, ln_g, ln_b)


def _hybrid_layer(x, positions, w_in, b_gate, sinks, w_pa, w_pb, w_out, ln_g, ln_b):
    bn, s_len, d_model = x.shape
    assert bn == 1 and d_model == D_MODEL
    assert s_len % (BLOCK * B_PATTERNS[-1][1]) == 0 and s_len % PROJ_TM == 0
    x2d = x.reshape(s_len, d_model)
    w, kinds, bias, tables = _pack_weights_and_tables(w_in, b_gate, positions)
    h = _in_projection(x2d, w, bias, tables, kinds)
    ya, o0, lse0, o1, lse1 = _mixer_a_and_b0(h, sinks)
    o_groups, lse_groups = [o0, o1], [lse0, lse1]
    for gi in range(2, B_N_GROUPS):
        o, lse = _mixer_b_group(h, gi)
        o_groups.append(o)
        lse_groups.append(lse)
    out = _tail(x2d, ya, o_groups, lse_groups, h, w_pa.astype(BF16), w_pb.astype(BF16), w_out.astype(BF16),
                ln_g.reshape(1, d_model).astype(F32), ln_b.reshape(1, d_model).astype(F32))
    return out.reshape(bn, s_len, d_model)


def kernel(x, positions, w_in, b_gate, sinks, w_pa, w_pb, w_out, ln_g, ln_b):
    for layer in range(w_in.shape[0]):
        x = _hybrid_layer(x, positions, w_in[layer], b_gate[layer], sinks[layer], w_pa[layer], w_pb[layer],
                          w_out[layer], ln_g[layer], ln_b[layer])
    return x
```
